```python
import jax, jax.numpy as jnp
from jax import lax
import numpy as np

D_MODEL = 1024
BATCH = 8
SEQ = 2048
DEPTH = 1

CHUNK = 64
Q_BLOCK = 128
MLA_HEADS = 8
MLA_NOPE = 64
MLA_ROPE = 32
MLA_V = 64
Q_LORA = 384
KV_LORA = 256
ROPE_THETA = 10000.0
HG_HEADS = 4
HG_DK = 128
HG_DV = 128
D_MIX = MLA_HEADS * MLA_V + HG_HEADS * HG_DV
IN_SPLITS = (Q_LORA, KV_LORA, MLA_ROPE, HG_HEADS * HG_DK, HG_HEADS * HG_DK, HG_HEADS * HG_DV, HG_HEADS * HG_DV)
IN_COLS = Q_LORA + KV_LORA + MLA_ROPE + 2 * HG_HEADS * HG_DK + 2 * HG_HEADS * HG_DV
N_EXPERTS = 32
TOP_K = 4
D_EXPERT = 1024
SWIGLU_LIMIT = 7.0
SWIGLU_ALPHA = 1.702
MOE_BLOCK = 128
LN_EPS = 1e-5
RMS_EPS = 1e-6
DEEPNORM_ALPHA = (2.0 * DEPTH) ** 0.25
DEEPNORM_BETA = (8.0 * DEPTH) ** -0.25

kernel_name = "hymba_mla_hgrn2_gptoss_moe_deepnorm"


def layer_norm(x, g, b):
    xf = x.astype(jnp.float32)
    mu = jnp.mean(xf, axis=-1, keepdims=True)
    var = jnp.mean(jnp.square(xf - mu), axis=-1, keepdims=True)
    return ((xf - mu) * lax.rsqrt(var + LN_EPS) * g + b).astype(x.dtype)


def rms_norm(x, g):
    xf = x.astype(jnp.float32)
    return (xf * lax.rsqrt(jnp.mean(jnp.square(xf), axis=-1, keepdims=True) + RMS_EPS) * g).astype(x.dtype)


def rope_cos_sin(positions):
    inv_freq = ROPE_THETA ** (-jnp.arange(0, MLA_ROPE, 2, dtype=jnp.float32) / MLA_ROPE)
    ang = positions.astype(jnp.float32)[..., None] * inv_freq
    return jnp.cos(ang), jnp.sin(ang)


def apply_rope(x, cos, sin):
    x1, x2 = jnp.split(x.astype(jnp.float32), 2, axis=-1)
    return jnp.concatenate([x1 * cos - x2 * sin, x1 * sin + x2 * cos], axis=-1).astype(x.dtype)


def mla_attention(q_nope, q_rope, k_nope, k_rope, v):
    B, S, H, _ = q_nope.shape
    n_blocks = S // Q_BLOCK
    key_chunk = jnp.arange(S) // CHUNK
    scale = (MLA_NOPE + MLA_ROPE) ** -0.5

    def one_block(i):
        start = i * Q_BLOCK
        qn = lax.dynamic_slice_in_dim(q_nope, start, Q_BLOCK, axis=1)
        qr = lax.dynamic_slice_in_dim(q_rope, start, Q_BLOCK, axis=1)
        s = (jnp.einsum("bqhd,bkhd->bhqk", qn, k_nope)
             + jnp.einsum("bqhr,bkr->bhqk", qr, k_rope)).astype(jnp.float32) * scale
        q_chunk = (start + jnp.arange(Q_BLOCK)) // CHUNK
        mask = key_chunk[None, :] <= q_chunk[:, None]
        p = jax.nn.softmax(jnp.where(mask, s, -jnp.inf), axis=-1).astype(v.dtype)
        return jnp.einsum("bhqk,bkhd->bqhd", p, v)

    out = lax.map(one_block, jnp.arange(n_blocks))
    return out.transpose(1, 0, 2, 3, 4).reshape(B, S, H * v.shape[-1])


def hgrn2_chunkwise(q, k, log_f, v):
    B, S, H, DK = q.shape
    DV = v.shape[-1]
    N = S // CHUNK

    def to_chunks(a):
        return a.astype(jnp.float32).reshape(B, N, CHUNK, H, a.shape[-1]).swapaxes(0, 1)

    causal = jnp.tril(jnp.ones((CHUNK, CHUNK), dtype=bool))[None, :, :, None, None]

    def step(state, inp):
        qc, kc, lfc, vc = inp
        b = jnp.cumsum(lfc, axis=1)
        b_last = b[:, -1]
        diff = b[:, :, None] - b[:, None, :]
        decay = jnp.exp(jnp.where(causal, diff, -jnp.inf))
        scores = jnp.einsum("bthk,btshk,bshk->bhts", qc, decay, kc)
        o = (jnp.einsum("bhts,bshv->bthv", scores, vc)
             + jnp.einsum("bthk,bhkv->bthv", qc * jnp.exp(b), state))
        new_state = (jnp.exp(b_last)[..., None] * state
                     + jnp.einsum("bshk,bshv->bhkv", kc * jnp.exp(b_last[:, None] - b), vc))
        return new_state, o

    s0 = jnp.zeros((B, H, DK, DV), jnp.float32)
    _, o = lax.scan(step, s0, (to_chunks(q), to_chunks(k), to_chunks(log_f), to_chunks(v)))
    return o.swapaxes(0, 1).reshape(B, S, H, DV)


def hybrid_mixer(h, cos, sin, lb, w_in, q_a_norm_g, w_q_b, kv_a_norm_g, w_kv_b, mla_out_g, hgrn_out_g, w_o):
    B, S, _ = h.shape
    offsets = [int(o) for o in np.cumsum(IN_SPLITS)[:-1]]
    q_a, kv_a, k_rope, hq, hf, hi, hg = jnp.split(h @ w_in, offsets, axis=-1)

    q = (rms_norm(q_a, q_a_norm_g) @ w_q_b).reshape(B, S, MLA_HEADS, MLA_NOPE + MLA_ROPE)
    q_nope = q[..., :MLA_NOPE]
    q_rope = apply_rope(q[..., MLA_NOPE:], cos[:, :, None], sin[:, :, None])
    kv = (rms_norm(kv_a, kv_a_norm_g) @ w_kv_b).reshape(B, S, MLA_HEADS, MLA_NOPE + MLA_V)
    k_nope, v = kv[..., :MLA_NOPE], kv[..., MLA_NOPE:]
    k_rope = apply_rope(k_rope, cos, sin)
    attn = rms_norm(mla_attention(q_nope, q_rope, k_nope, k_rope, v), mla_out_g)

    def heads(a, d):
        return a.reshape(B, S, HG_HEADS, d)
    forget = lb + (1.0 - lb) * jax.nn.sigmoid(hf.astype(jnp.float32))
    o = hgrn2_chunkwise(heads(jax.nn.silu(hq), HG_DK), heads(1.0 - forget, HG_DK),
                        heads(jnp.log(forget), HG_DK), heads(hi, HG_DV))
    o = rms_norm(o, hgrn_out_g.reshape(HG_HEADS, HG_DV)) * jax.nn.silu(heads(hg, HG_DV).astype(jnp.float32))
    o = o.reshape(B, S, HG_HEADS * HG_DV).astype(h.dtype)

    return jnp.concatenate([attn, o], axis=-1) @ w_o


def moe_ffn(h, w_router, b_router, w_up, b_up, w_down, b_down):
    B, S, D = h.shape
    T = B * S
    A = T * TOP_K
    xt = h.reshape(T, D)
    logits = (xt @ w_router + b_router).astype(jnp.float32)
    top_logit, top_e = lax.top_k(logits, TOP_K)
    gate = jax.nn.softmax(top_logit, axis=-1)

    flat_e = top_e.reshape(A)
    order = jnp.argsort(flat_e, stable=True)
    sorted_e = flat_e[order]
    sorted_tok = (order // TOP_K).astype(jnp.int32)
    sorted_gate = gate.reshape(A)[order]
    counts = jnp.bincount(flat_e, length=N_EXPERTS)
    padded = (counts + MOE_BLOCK - 1) // MOE_BLOCK * MOE_BLOCK
    start = jnp.cumsum(counts) - counts
    pad_end = jnp.cumsum(padded)
    pad_start = pad_end - padded
    dest = pad_start[sorted_e] + (jnp.arange(A) - start[sorted_e])
    n_blocks = -(-A // MOE_BLOCK) + N_EXPERTS
    P = n_blocks * MOE_BLOCK
    slot_tok = jnp.zeros((P,), jnp.int32).at[dest].set(sorted_tok)
    slot_gate = jnp.zeros((P,), jnp.float32).at[dest].set(sorted_gate)
    block_e = jnp.minimum(jnp.searchsorted(pad_end, jnp.arange(n_blocks) * MOE_BLOCK, side="right"),
                          N_EXPERTS - 1)
    xs = xt[slot_tok].reshape(n_blocks, MOE_BLOCK, D)

    def expert_block(args):
        xb, e = args
        hb = xb @ w_up[e] + b_up[e]
        glu, lin = jnp.split(hb, 2, axis=-1)
        glu = jnp.minimum(glu, SWIGLU_LIMIT)
        lin = jnp.clip(lin, -SWIGLU_LIMIT, SWIGLU_LIMIT)
        act = glu * jax.nn.sigmoid(SWIGLU_ALPHA * glu) * (lin + 1.0)
        return act @ w_down[e] + b_down[e]

    ys = lax.map(expert_block, (xs, block_e)).reshape(P, D)
    ys = ys * slot_gate[:, None].astype(ys.dtype)
    out = jnp.zeros((T, D), h.dtype).at[slot_tok].add(ys.astype(h.dtype))
    return out.reshape(B, S, D)


def setup_inputs(seed: int = 0) -> dict:
    key = jax.random.key(seed)
    ks = jax.random.split(key, 24)
    f32 = jnp.float32

    def dense(k, shape, fan_in, scale=1.0):
        return jax.random.normal(k, shape, f32) * (scale * fan_in ** -0.5)

    def gain(k, shape):
        return 1.0 + 0.02 * jax.random.normal(k, shape, f32)

    def bias(k, shape, s=0.02):
        return s * jax.random.normal(k, shape, f32)

    x = jax.random.normal(ks[0], (BATCH, SEQ, D_MODEL), f32)
    positions = (jax.random.randint(ks[1], (BATCH, 1), 0, 4096, dtype=jnp.int32)
                 + jnp.arange(SEQ, dtype=jnp.int32)[None, :])
    hv = HG_HEADS * HG_DV
    in_col_scale = jnp.concatenate([jnp.ones((IN_COLS - 2 * hv,), f32),
                                    jnp.full((hv,), DEEPNORM_BETA, f32),
                                    jnp.ones((hv,), f32)])
    kvb_col_scale = jnp.tile(jnp.concatenate([jnp.ones((MLA_NOPE,), f32),
                                              jnp.full((MLA_V,), DEEPNORM_BETA, f32)]), MLA_HEADS)
    return {
        "x": x,
        "positions": positions,
        "ln_in_g": gain(ks[2], (D_MODEL,)),
        "ln_in_b": bias(ks[3], (D_MODEL,)),
        "w_in": dense(ks[4], (DEPTH, D_MODEL, IN_COLS), D_MODEL) * in_col_scale,
        "q_a_norm_g": gain(ks[5], (DEPTH, Q_LORA)),
        "w_q_b": dense(ks[6], (DEPTH, Q_LORA, MLA_HEADS * (MLA_NOPE + MLA_ROPE)), Q_LORA),
        "kv_a_norm_g": gain(ks[7], (DEPTH, KV_LORA)),
        "w_kv_b": dense(ks[8], (DEPTH, KV_LORA, MLA_HEADS * (MLA_NOPE + MLA_V)), KV_LORA) * kvb_col_scale,
        "hgrn_lb_logits": 0.5 * jax.random.normal(ks[9], (DEPTH + 1, HG_HEADS * HG_DK), f32),
        "mla_out_g": gain(ks[10], (DEPTH, MLA_HEADS * MLA_V)),
        "hgrn_out_g": gain(ks[11], (DEPTH, HG_HEADS * HG_DV)),
        "w_o": dense(ks[12], (DEPTH, D_MIX, D_MODEL), D_MIX, DEEPNORM_BETA),
        "ln1_g": gain(ks[13], (DEPTH, D_MODEL)),
        "ln1_b": bias(ks[14], (DEPTH, D_MODEL)),
        "w_router": dense(ks[15], (DEPTH, D_MODEL, N_EXPERTS), D_MODEL),
        "b_router": bias(ks[16], (DEPTH, N_EXPERTS), 0.01),
        "w_up": dense(ks[17], (DEPTH, N_EXPERTS, D_MODEL, 2 * D_EXPERT), D_MODEL),
        "b_up": bias(ks[18], (DEPTH, N_EXPERTS, 2 * D_EXPERT)),
        "w_down": dense(ks[19], (DEPTH, N_EXPERTS, D_EXPERT, D_MODEL), D_EXPERT, DEEPNORM_BETA),
        "b_down": bias(ks[20], (DEPTH, N_EXPERTS, D_MODEL)),
        "ln2_g": gain(ks[21], (DEPTH, D_MODEL)),
        "ln2_b": bias(ks[22], (DEPTH, D_MODEL)),
    }


def reference(x, positions, ln_in_g, ln_in_b, w_in, q_a_norm_g, w_q_b, kv_a_norm_g, w_kv_b,
              hgrn_lb_logits, mla_out_g, hgrn_out_g, w_o, ln1_g, ln1_b, w_router, b_router,
              w_up, b_up, w_down, b_down, ln2_g, ln2_b):
    cos, sin = rope_cos_sin(positions)
    lower_bounds = jnp.cumsum(jax.nn.softmax(hgrn_lb_logits.astype(jnp.float32), axis=0), axis=0)
    h = layer_norm(x, ln_in_g, ln_in_b)
    for l in range(DEPTH):
        mix = hybrid_mixer(h, cos, sin, lower_bounds[l], w_in[l], q_a_norm_g[l], w_q_b[l],
                           kv_a_norm_g[l], w_kv_b[l], mla_out_g[l], hgrn_out_g[l], w_o[l])
        h = layer_norm(DEEPNORM_ALPHA * h + mix, ln1_g[l], ln1_b[l])
        ffn = moe_ffn(h, w_router[l], b_router[l], w_up[l], b_up[l], w_down[l], b_down[l])
        h = layer_norm(DEEPNORM_ALPHA * h + ffn, ln2_g[l], ln2_b[l])
    return h
```

```python
import functools

import numpy as np
import jax
import jax.numpy as jnp
from jax import lax
from jax.experimental import pallas as pl
from jax.experimental.pallas import tpu as pltpu

F32 = jnp.float32
BF16 = jnp.bfloat16
I32 = jnp.int32

D_MODEL = 1024
CHUNK = 64
MLA_HEADS = 8
MLA_NOPE = 64
MLA_ROPE = 32
MLA_V = 64
Q_LORA = 384
KV_LORA = 256
ROPE_THETA = 10000.0
HG_HEADS = 4
HG_DK = 128
HG_DV = 128
HG_W = HG_HEADS * HG_DK
N_EXPERTS = 32
TOP_K = 4
D_EXPERT = 1024
SWIGLU_LIMIT = 7.0
SWIGLU_ALPHA = 1.702
LN_EPS = 1e-5
RMS_EPS = 1e-6
DEPTH = 1
DEEPNORM_ALPHA = (2.0 * DEPTH) ** 0.25
LOG2E = 1.4426950408889634

LANES = 128
HEAD_PAD = 128
HALF_ROPE = MLA_ROPE // 2
KR_OFF = Q_LORA + KV_LORA
HQ_OFF = KR_OFF + HEAD_PAD
IN_COLS_PAD = HQ_OFF + 4 * HG_W

TM_IN = 512
TQ = 256
HG_ROWS = 256
TM_OUT = 256
TM_DISP = 256
ROW_BLOCK = 256
VMEM_LIMIT = 52 * 1024 * 1024


def _cparams(sem, vmem=VMEM_LIMIT):
    return pltpu.CompilerParams(dimension_semantics=sem, vmem_limit_bytes=vmem)


def _layer_norm(x, g, b):
    mu = jnp.mean(x, axis=-1, keepdims=True)
    xc = x - mu
    var = jnp.mean(xc * xc, axis=-1, keepdims=True)
    return xc * lax.rsqrt(var + LN_EPS) * g + b


def _rms_norm(x, g):
    return x * lax.rsqrt(jnp.mean(x * x, axis=-1, keepdims=True) + RMS_EPS) * g


def _sigmoid_pair(x):
    e = jnp.exp(-jnp.abs(x))
    big = 1.0 / (1.0 + e)
    small = e * big
    pos = x >= 0
    return jnp.where(pos, big, small), jnp.where(pos, small, big)


def _cos_sin_body(ang_ref, cos_ref, sin_ref):
    a = ang_ref[...]
    cos_ref[...] = jnp.cos(a)
    sin_ref[...] = jnp.sin(a)


def _rope_tables(positions):
    T = positions.size
    inv_freq = ROPE_THETA ** (-jnp.arange(0, MLA_ROPE, 2, dtype=F32) / MLA_ROPE)
    ang = (positions.astype(F32)[..., None] * inv_freq).reshape(T * HALF_ROPE // LANES, LANES)
    cos, sin = pl.pallas_call(
        _cos_sin_body,
        out_shape=(jax.ShapeDtypeStruct(ang.shape, F32),) * 2,
        name="rope_cos_sin",
    )(ang)
    cos = cos.reshape(T, HALF_ROPE)
    sin = sin.reshape(T, HALF_ROPE)
    ones = jnp.ones((T, MLA_NOPE), F32)
    z_n = jnp.zeros((T, MLA_NOPE), F32)
    z_h = jnp.zeros((T, HALF_ROPE), F32)
    z_p = jnp.zeros((T, HEAD_PAD - MLA_NOPE - MLA_ROPE), F32)
    c_tab = jnp.concatenate([ones, cos, cos, z_p], axis=1)
    s1_tab = jnp.concatenate([z_n, z_h, sin, z_p], axis=1)
    s2_tab = jnp.concatenate([z_n, -sin, z_h, z_p], axis=1)
    return c_tab, s1_tab, s2_tab


def _rope(x, c, s1, s2, reps):
    width = reps * HEAD_PAD
    if reps > 1:
        c = jnp.concatenate([c] * reps, axis=1)
        s1 = jnp.concatenate([s1] * reps, axis=1)
        s2 = jnp.concatenate([s2] * reps, axis=1)
    return (x * c + pltpu.roll(x, HALF_ROPE, 1) * s1
            + pltpu.roll(x, width - HALF_ROPE, 1) * s2)


def _in_proj_body(x_ref, lng_ref, lnb_ref, win_ref, qag_ref, wqb_ref, kvg_ref, wkv_ref, lbl_ref,
                  c_ref, s1_ref, s2_ref,
                  q_out, k_out, v_out, hq_out, hk_out, hg_out, hv_out, gate_out):
    h = _layer_norm(x_ref[...], lng_ref[...], lnb_ref[...])
    p = jnp.dot(h.astype(BF16), win_ref[...], preferred_element_type=F32)
    c, s1, s2 = c_ref[...], s1_ref[...], s2_ref[...]

    qn = _rms_norm(p[:, :Q_LORA], qag_ref[...]).astype(BF16)
    q = jnp.dot(qn, wqb_ref[...], preferred_element_type=F32)
    scale = (MLA_NOPE + MLA_ROPE) ** -0.5
    q_out[...] = (_rope(q, c, s1, s2, MLA_HEADS) * scale).astype(BF16)
    kvn = _rms_norm(p[:, Q_LORA:KR_OFF], kvg_ref[...]).astype(BF16)
    kv = jnp.dot(kvn, wkv_ref[...], preferred_element_type=F32)
    kr = _rope(p[:, KR_OFF:HQ_OFF], c, s1, s2, 1)
    k_out[...] = (kv[:, :MLA_HEADS * HEAD_PAD] + jnp.concatenate([kr] * MLA_HEADS, axis=1)).astype(BF16)
    v_out[...] = kv[:, MLA_HEADS * HEAD_PAD:].astype(BF16)

    lbl = lbl_ref[...]
    le = jnp.exp(lbl - jnp.max(lbl, axis=0, keepdims=True))
    lb = le[0:1] / jnp.sum(le, axis=0, keepdims=True)
    hq = p[:, HQ_OFF:HQ_OFF + HG_W]
    hf = p[:, HQ_OFF + HG_W:HQ_OFF + 2 * HG_W]
    hi = p[:, HQ_OFF + 2 * HG_W:HQ_OFF + 3 * HG_W]
    hg = p[:, HQ_OFF + 3 * HG_W:HQ_OFF + 4 * HG_W]
    hq_out[...] = (hq * _sigmoid_pair(hq)[0]).astype(BF16)
    sg, one_minus_sg = _sigmoid_pair(hf)
    hk_out[...] = ((1.0 - lb) * one_minus_sg).astype(BF16)
    hg_out[...] = jnp.log(lb + (1.0 - lb) * sg)
    hv_out[...] = hi.astype(BF16)
    gate_out[...] = (hg * _sigmoid_pair(hg)[0]).astype(BF16)


def _in_proj(xt, ln_g, ln_b, w_in_p, qa_g, w_qb_p, kv_g, w_kv_p, lb_logits, c_tab, s1_tab, s2_tab):
    T = xt.shape[0]
    row = lambda w: pl.BlockSpec((TM_IN, w), lambda i: (i, 0))
    full = lambda a: pl.BlockSpec(a.shape, lambda i: (0, 0))
    outs = [(MLA_HEADS * HEAD_PAD, BF16), (MLA_HEADS * HEAD_PAD, BF16), (MLA_HEADS * MLA_V, BF16),
            (HG_W, BF16), (HG_W, BF16), (HG_W, F32), (HG_W, BF16), (HG_W, BF16)]
    return pl.pallas_call(
        _in_proj_body,
        grid=(T // TM_IN,),
        in_specs=[row(D_MODEL), full(ln_g), full(ln_b), full(w_in_p), full(qa_g), full(w_qb_p),
                  full(kv_g), full(w_kv_p), full(lb_logits), row(HEAD_PAD), row(HEAD_PAD), row(HEAD_PAD)],
        out_specs=[row(w) for w, _ in outs],
        out_shape=[jax.ShapeDtypeStruct((T, w), d) for w, d in outs],
        compiler_params=_cparams(("parallel",)),
        name="in_proj",
    )(xt, ln_g, ln_b, w_in_p, qa_g, w_qb_p, kv_g, w_kv_p, lb_logits, c_tab, s1_tab, s2_tab)


def _attn_body(q_ref, k_ref, v_ref, o_ref):
    qi = pl.program_id(2)
    dn = (((1,), (1,)), ((), ()))
    row_c = lax.broadcasted_iota(I32, (TQ, TQ), 0) // CHUNK
    col_c = lax.broadcasted_iota(I32, (TQ, TQ), 1) // CHUNK
    diag_mask = col_c <= row_c
    res = []
    for hh in range(2):
        hs = slice(hh * HEAD_PAD, (hh + 1) * HEAD_PAD)
        q = q_ref[0, :, hs]

        def step(j, carry, masked):
            m, l, acc = carry
            ks = k_ref[0, pl.ds(pl.multiple_of(j * TQ, TQ), TQ), hs]
            vs = v_ref[0, pl.ds(pl.multiple_of(j * TQ, TQ), TQ), :]
            s = lax.dot_general(q, ks, dn, preferred_element_type=F32)
            if masked:
                s = jnp.where(diag_mask, s, -jnp.inf)
            m_new = jnp.maximum(m, jnp.max(s, axis=-1, keepdims=True))
            p = jnp.exp(s - m_new)
            a = jnp.exp(m - m_new)
            l = a * l + jnp.sum(p, axis=-1, keepdims=True)
            acc = a * acc + jnp.dot(p.astype(BF16), vs, preferred_element_type=F32)
            return m_new, l, acc

        init = (jnp.full((TQ, 1), -jnp.inf, F32), jnp.zeros((TQ, 1), F32),
                jnp.zeros((TQ, 2 * MLA_V), F32))
        carry = lax.fori_loop(0, qi, functools.partial(step, masked=False), init)
        m, l, acc = step(qi, carry, True)
        res.append(acc / l)
    lane = lax.broadcasted_iota(I32, (TQ, 2 * MLA_V), 1)
    o_ref[0] = jnp.where(lane < MLA_V, res[0], res[1]).astype(BF16)


def _attention(q, k, v, B, S):
    q3 = q.reshape(B, S, MLA_HEADS * HEAD_PAD)
    k3 = k.reshape(B, S, MLA_HEADS * HEAD_PAD)
    v3 = v.reshape(B, S, MLA_HEADS * MLA_V)
    out = pl.pallas_call(
        _attn_body,
        grid=(B, MLA_HEADS // 2, S // TQ),
        in_specs=[pl.BlockSpec((1, TQ, 2 * HEAD_PAD), lambda b, h, i: (b, i, h)),
                  pl.BlockSpec((1, S, 2 * HEAD_PAD), lambda b, h, i: (b, 0, h)),
                  pl.BlockSpec((1, S, 2 * MLA_V), lambda b, h, i: (b, 0, h))],
        out_specs=pl.BlockSpec((1, TQ, 2 * MLA_V), lambda b, h, i: (b, i, h)),
        out_shape=jax.ShapeDtypeStruct((B, S, MLA_HEADS * MLA_V), BF16),
        compiler_params=_cparams(("parallel", "parallel", "arbitrary")),
        name="mla_attention",
    )(q3, k3, v3)
    return out.reshape(B * S, MLA_HEADS * MLA_V)


N_LEVELS = 6
Z_BLOCKS = 2 * N_LEVELS + 2


def _hgrn_constants():
    z = np.zeros((Z_BLOCKS, CHUNK, CHUNK), np.float32)
    masks = np.zeros((N_LEVELS, CHUNK, CHUNK), np.float32)
    idx = np.arange(CHUNK)
    for l in range(N_LEVELS):
        hs = CHUNK >> (l + 1)
        parent = idx // (2 * hs)
        right = (idx % (2 * hs)) >= hs
        mid = parent * 2 * hs + hs
        for t in range(CHUNK):
            if right[t]:
                z[l, t, mid[t]:t + 1] = 1.0
            else:
                z[N_LEVELS + l, t, t + 1:mid[t]] = 1.0
        masks[l] = (parent[:, None] == parent[None, :]) & right[:, None] & (~right)[None, :]
    z[2 * N_LEVELS] = np.tril(np.ones((CHUNK, CHUNK), np.float32))
    z[2 * N_LEVELS + 1] = np.triu(np.ones((CHUNK, CHUNK), np.float32), 1)
    z = z.reshape(Z_BLOCKS * CHUNK, CHUNK)
    return np.concatenate([z, z], axis=1), masks


def _hgrn_body(q_ref, k_ref, g_ref, v_ref, gate_ref, og_ref, z_ref, mask_ref, o_ref, state_ref):
    @pl.when(pl.program_id(1) == 0)
    def _():
        state_ref[...] = jnp.zeros_like(state_ref)

    dn_t = (((1,), (1,)), ((), ()))
    eye = (lax.broadcasted_iota(I32, (CHUNK, CHUNK), 0)
           == lax.broadcasted_iota(I32, (CHUNK, CHUNK), 1)).astype(F32)
    for c in range(HG_ROWS // CHUNK):
        rows = slice(c * CHUNK, (c + 1) * CHUNK)
        for h in range(HG_HEADS):
            cols = slice(h * HG_DK, (h + 1) * HG_DK)
            g2 = g_ref[0, rows, cols] * LOG2E
            g_hi = g2.astype(BF16)
            g_lo = (g2 - g_hi.astype(F32)).astype(BF16)
            expo = jnp.dot(z_ref[...], jnp.concatenate([g_hi, g_lo], axis=0),
                           preferred_element_type=F32)
            dec = jnp.exp2(expo)
            blk = lambda i: dec[i * CHUNK:(i + 1) * CHUNK]
            q = q_ref[0, rows, cols].astype(F32)
            k = k_ref[0, rows, cols].astype(F32)
            v = v_ref[0, rows, cols]
            a = eye * jnp.sum(q * k, axis=-1, keepdims=True)
            for l in range(N_LEVELS):
                qt = (q * blk(l)).astype(BF16)
                kt = (k * blk(N_LEVELS + l)).astype(BF16)
                a = a + mask_ref[l] * lax.dot_general(qt, kt, dn_t, preferred_element_type=F32)
            o = jnp.dot(a.astype(BF16), v, preferred_element_type=F32)
            st = state_ref[h]
            d_in = blk(2 * N_LEVELS)
            o = o + lax.dot_general((q * d_in).astype(BF16), st.astype(BF16), dn_t,
                                    preferred_element_type=F32)
            kh = (k * blk(2 * N_LEVELS + 1)).astype(BF16)
            upd = lax.dot_general(v, kh, (((0,), (0,)), ((), ())), preferred_element_type=F32)
            state_ref[h] = st * d_in[CHUNK - 1:CHUNK] + upd
            o = _rms_norm(o, og_ref[:, cols]) * gate_ref[0, rows, cols].astype(F32)
            o_ref[0, rows, cols] = o.astype(BF16)


def _hgrn(hq, hk, hg, hv, gate, out_g, B, S):
    zc, masks = _hgrn_constants()
    zc = jnp.asarray(zc, BF16)
    masks = jnp.asarray(masks, F32)
    blk = pl.BlockSpec((1, HG_ROWS, HG_W), lambda b, i: (b, i, 0))
    r3 = lambda a: a.reshape(B, S, HG_W)
    out = pl.pallas_call(
        _hgrn_body,
        grid=(B, S // HG_ROWS),
        in_specs=[blk, blk, blk, blk, blk,
                  pl.BlockSpec((1, HG_W), lambda b, i: (0, 0)),
                  pl.BlockSpec(zc.shape, lambda b, i: (0, 0)),
                  pl.BlockSpec(masks.shape, lambda b, i: (0, 0, 0))],
        out_specs=blk,
        out_shape=jax.ShapeDtypeStruct((B, S, HG_W), BF16),
        scratch_shapes=[pltpu.VMEM((HG_HEADS, HG_DV, HG_DK), F32)],
        compiler_params=_cparams(("parallel", "arbitrary")),
        name="hgrn2",
    )(r3(hq), r3(hk), r3(hg), r3(hv), r3(gate), out_g, zc, masks)
    return out.reshape(B * S, HG_W)


def _pack_bf16_pair(lo, hi):
    lo_b = pltpu.bitcast(lo.astype(BF16).astype(F32), I32)
    hi_b = pltpu.bitcast(hi.astype(BF16).astype(F32), I32)
    return hi_b | lax.shift_right_logical(lo_b, 16)


def _unpack_bf16_pair(w):
    lo = pltpu.bitcast(lax.shift_left(w, 16), F32)
    hi = pltpu.bitcast(w & jnp.int32(-65536), F32)
    return lo, hi


def _out_proj_body(x_ref, lng_ref, lnb_ref, attn_ref, hgo_ref, mg_ref, wo_ref, l1g_ref, l1b_ref,
                   wr_ref, br_ref,
                   h1_out, h1p_out, idx_out, gate_out, rank_out, cnt_out, carry_ref):
    @pl.when(pl.program_id(0) == 0)
    def _():
        carry_ref[...] = jnp.zeros_like(carry_ref)

    h = _layer_norm(x_ref[...], lng_ref[...], lnb_ref[...])
    attn = _rms_norm(attn_ref[...].astype(F32), mg_ref[...]).astype(BF16)
    mix_in = jnp.concatenate([attn, hgo_ref[...]], axis=1)
    mix = jnp.dot(mix_in, wo_ref[...], preferred_element_type=F32)
    h1 = _layer_norm(DEEPNORM_ALPHA * h + mix, l1g_ref[...], l1b_ref[...])
    h1_out[...] = h1
    h1p_out[...] = _pack_bf16_pair(h1[:, :D_MODEL // 2], h1[:, D_MODEL // 2:])

    logits = jnp.dot(h1, wr_ref[...], preferred_element_type=F32,
                     precision=lax.Precision.HIGHEST) + br_ref[...]
    tm = logits.shape[0]
    lane = lax.broadcasted_iota(I32, (tm, N_EXPERTS), 1)
    work = logits
    vals, idxs = [], []
    for _ in range(TOP_K):
        mx = jnp.max(work, axis=-1, keepdims=True)
        ix = jnp.min(jnp.where(work == mx, lane, N_EXPERTS), axis=-1, keepdims=True)
        vals.append(mx)
        idxs.append(ix)
        work = jnp.where(lane == ix, -jnp.inf, work)
    exps = [jnp.exp(vl - vals[0]) for vl in vals]
    den = exps[0] + exps[1] + exps[2] + exps[3]

    sel = jnp.zeros((tm, N_EXPERTS), F32)
    for ix in idxs:
        sel = sel + (lane == ix).astype(F32)
    tril = (lax.broadcasted_iota(I32, (tm, tm), 1)
            < lax.broadcasted_iota(I32, (tm, tm), 0)).astype(BF16)
    before = jnp.dot(tril, sel.astype(BF16), preferred_element_type=F32) + carry_ref[...]
    carry_ref[...] = carry_ref[...] + jnp.sum(sel, axis=0, keepdims=True)
    cnt_out[...] = carry_ref[...]

    lane4 = lax.broadcasted_iota(I32, (tm, TOP_K), 1)
    idx4 = jnp.zeros((tm, TOP_K), I32)
    gate4 = jnp.zeros((tm, TOP_K), F32)
    rank4 = jnp.zeros((tm, TOP_K), F32)
    for j in range(TOP_K):
        rk = jnp.sum(jnp.where(lane == idxs[j], before, 0.0), axis=-1, keepdims=True)
        idx4 = jnp.where(lane4 == j, idxs[j], idx4)
        gate4 = jnp.where(lane4 == j, exps[j] / den, gate4)
        rank4 = jnp.where(lane4 == j, rk, rank4)
    idx_out[...] = idx4
    gate_out[...] = gate4
    rank_out[...] = rank4.astype(I32)


def _out_proj(xt, ln_g, ln_b, attn, hgo, mla_g, w_o, l1g, l1b, w_r, b_r):
    T = xt.shape[0]
    row = lambda w: pl.BlockSpec((TM_OUT, w), lambda i: (i, 0))
    full = lambda a: pl.BlockSpec(a.shape, lambda i: (0, 0))
    return pl.pallas_call(
        _out_proj_body,
        grid=(T // TM_OUT,),
        in_specs=[row(D_MODEL), full(ln_g), full(ln_b), row(MLA_HEADS * MLA_V), row(HG_W), full(mla_g),
                  full(w_o), full(l1g), full(l1b), full(w_r), full(b_r)],
        out_specs=[row(D_MODEL), row(D_MODEL // 2), row(TOP_K), row(TOP_K), row(TOP_K),
                   pl.BlockSpec((1, N_EXPERTS), lambda i: (0, 0))],
        out_shape=[jax.ShapeDtypeStruct((T, D_MODEL), F32), jax.ShapeDtypeStruct((T, D_MODEL // 2), I32),
                   jax.ShapeDtypeStruct((T, TOP_K), I32), jax.ShapeDtypeStruct((T, TOP_K), F32),
                   jax.ShapeDtypeStruct((T, TOP_K), I32), jax.ShapeDtypeStruct((1, N_EXPERTS), F32)],
        scratch_shapes=[pltpu.VMEM((1, N_EXPERTS), F32)],
        compiler_params=_cparams(("arbitrary",)),
        name="out_proj_router",
    )(xt, ln_g, ln_b, attn, hgo, mla_g, w_o, l1g, l1b, w_r, b_r)


def _dispatch_body(dest_ref, h_ref, xs_in, xs_out, sem):
    del xs_in

    def copy(t, j):
        d = dest_ref[0, 0, t * TOP_K + j]
        return pltpu.make_async_copy(h_ref.at[pl.ds(t, 1), :], xs_out.at[pl.ds(d, 1), :], sem)

    def start(t, c):
        for j in range(TOP_K):
            copy(t, j).start()
        return c

    def wait(t, c):
        for j in range(TOP_K):
            copy(t, j).wait()
        return c

    lax.fori_loop(0, TM_DISP, start, 0)
    lax.fori_loop(0, TM_DISP, wait, 0)


def _dispatch(dest, h1p, n_rows):
    T = h1p.shape[0]
    nt = T // TM_DISP
    xs0 = jnp.zeros((n_rows, D_MODEL // 2), I32)
    return pl.pallas_call(
        _dispatch_body,
        grid=(nt,),
        in_specs=[pl.BlockSpec((1, 1, TM_DISP * TOP_K), lambda i: (i, 0, 0), memory_space=pltpu.SMEM),
                  pl.BlockSpec((TM_DISP, D_MODEL // 2), lambda i: (i, 0)),
                  pl.BlockSpec(memory_space=pl.ANY)],
        out_specs=pl.BlockSpec(memory_space=pl.ANY),
        out_shape=jax.ShapeDtypeStruct((n_rows, D_MODEL // 2), I32),
        scratch_shapes=[pltpu.SemaphoreType.DMA(())],
        input_output_aliases={2: 0},
        compiler_params=_cparams(("arbitrary",)),
        name="moe_dispatch",
    )(dest.reshape(nt, 1, TM_DISP * TOP_K), h1p, xs0)


def _experts_body(be_ref, nu_ref, xs_ref, wu_ref, bu_ref, wd_ref, bd_ref, ys_ref, wu_bf, wd_bf):
    i = pl.program_id(0)
    prev = be_ref[jnp.maximum(i - 1, 0)]
    fresh = jnp.logical_or(i == 0, be_ref[i] != prev)

    @pl.when(jnp.logical_and(fresh, i < nu_ref[0]))
    def _():
        wu_bf[...] = wu_ref[0].astype(BF16)
        wd_bf[...] = wd_ref[0].astype(BF16)

    @pl.when(i < nu_ref[0])
    def _():
        lo, hi = _unpack_bf16_pair(xs_ref[...])
        x = jnp.concatenate([lo, hi], axis=1).astype(BF16)
        hb = jnp.dot(x, wu_bf[...], preferred_element_type=F32) + bu_ref[0]
        glu = jnp.minimum(hb[:, :D_EXPERT], SWIGLU_LIMIT)
        lin = jnp.clip(hb[:, D_EXPERT:], -SWIGLU_LIMIT, SWIGLU_LIMIT)
        act = glu * (1.0 / (1.0 + jnp.exp(-SWIGLU_ALPHA * glu))) * (lin + 1.0)
        ys_ref[...] = jnp.dot(act.astype(BF16), wd_bf[...], preferred_element_type=F32) + bd_ref[0]

    @pl.when(i >= nu_ref[0])
    def _():
        ys_ref[...] = jnp.zeros_like(ys_ref)


def _experts(block_e, n_used, xs, w_up, b_up, w_down, b_down):
    n_rows = xs.shape[0]
    nb = n_rows // ROW_BLOCK
    grid_spec = pltpu.PrefetchScalarGridSpec(
        num_scalar_prefetch=2,
        grid=(nb,),
        in_specs=[pl.BlockSpec((ROW_BLOCK, D_MODEL // 2), lambda i, be, nu: (i, 0)),
                  pl.BlockSpec((1, D_MODEL, 2 * D_EXPERT), lambda i, be, nu: (be[i], 0, 0)),
                  pl.BlockSpec((1, 1, 2 * D_EXPERT), lambda i, be, nu: (be[i], 0, 0)),
                  pl.BlockSpec((1, D_EXPERT, D_MODEL), lambda i, be, nu: (be[i], 0, 0)),
                  pl.BlockSpec((1, 1, D_MODEL), lambda i, be, nu: (be[i], 0, 0))],
        out_specs=pl.BlockSpec((ROW_BLOCK, D_MODEL), lambda i, be, nu: (i, 0)),
        scratch_shapes=[pltpu.VMEM((D_MODEL, 2 * D_EXPERT), BF16), pltpu.VMEM((D_EXPERT, D_MODEL), BF16)],
    )
    return pl.pallas_call(
        _experts_body,
        grid_spec=grid_spec,
        out_shape=jax.ShapeDtypeStruct((n_rows, D_MODEL), F32),
        compiler_params=_cparams(("arbitrary",)),
        name="moe_experts",
    )(block_e, n_used, xs, w_up, b_up.reshape(N_EXPERTS, 1, 2 * D_EXPERT), w_down,
      b_down.reshape(N_EXPERTS, 1, D_MODEL))


def _combine_body(dest_ref, h1_ref, gate_ref, l2g_ref, l2b_ref, ys_hbm, o_ref, buf, sem):
    def copy(t, j):
        d = dest_ref[0, 0, t * TOP_K + j]
        return pltpu.make_async_copy(ys_hbm.at[pl.ds(d, 1), :], buf.at[j, pl.ds(t, 1), :], sem)

    def start(t, c):
        for j in range(TOP_K):
            copy(t, j).start()
        return c

    def wait(t, c):
        for j in range(TOP_K):
            copy(t, j).wait()
        return c

    lax.fori_loop(0, TM_DISP, start, 0)
    lax.fori_loop(0, TM_DISP, wait, 0)
    gate = gate_ref[...]
    ffn = buf[0] * gate[:, 0:1]
    for j in range(1, TOP_K):
        ffn = ffn + buf[j] * gate[:, j:j + 1]
    o_ref[...] = _layer_norm(DEEPNORM_ALPHA * h1_ref[...] + ffn, l2g_ref[...], l2b_ref[...])


def _combine(dest, h1, gate4, l2g, l2b, ys):
    T = h1.shape[0]
    nt = T // TM_DISP
    return pl.pallas_call(
        _combine_body,
        grid=(nt,),
        in_specs=[pl.BlockSpec((1, 1, TM_DISP * TOP_K), lambda i: (i, 0, 0), memory_space=pltpu.SMEM),
                  pl.BlockSpec((TM_DISP, D_MODEL), lambda i: (i, 0)),
                  pl.BlockSpec((TM_DISP, TOP_K), lambda i: (i, 0)),
                  pl.BlockSpec(l2g.shape, lambda i: (0, 0)),
                  pl.BlockSpec(l2b.shape, lambda i: (0, 0)),
                  pl.BlockSpec(memory_space=pl.ANY)],
        out_specs=pl.BlockSpec((TM_DISP, D_MODEL), lambda i: (i, 0)),
        out_shape=jax.ShapeDtypeStruct((T, D_MODEL), F32),
        scratch_shapes=[pltpu.VMEM((TOP_K, TM_DISP, D_MODEL), F32), pltpu.SemaphoreType.DMA(())],
        compiler_params=_cparams(("arbitrary",)),
        name="moe_combine",
    )(dest.reshape(nt, 1, TM_DISP * TOP_K), h1, gate4, l2g, l2b, ys)


def _prep_weights(w_in, w_q_b, w_kv_b):
    zc = lambda n: jnp.zeros((D_MODEL, n), F32)
    w_in_p = jnp.concatenate(
        [w_in[:, :KR_OFF], zc(MLA_NOPE), w_in[:, KR_OFF:KR_OFF + MLA_ROPE],
         zc(HEAD_PAD - MLA_NOPE - MLA_ROPE), w_in[:, KR_OFF + MLA_ROPE:]], axis=1).astype(BF16)
    qd = MLA_NOPE + MLA_ROPE
    wq = w_q_b.reshape(Q_LORA, MLA_HEADS, qd)
    wq = jnp.pad(wq, ((0, 0), (0, 0), (0, HEAD_PAD - qd))).reshape(Q_LORA, MLA_HEADS * HEAD_PAD)
    wkv = w_kv_b.reshape(KV_LORA, MLA_HEADS, MLA_NOPE + MLA_V)
    wk = jnp.pad(wkv[:, :, :MLA_NOPE], ((0, 0), (0, 0), (0, HEAD_PAD - MLA_NOPE)))
    wk = wk.reshape(KV_LORA, MLA_HEADS * HEAD_PAD)
    wv = wkv[:, :, MLA_NOPE:].reshape(KV_LORA, MLA_HEADS * MLA_V)
    return w_in_p, wq.astype(BF16), jnp.concatenate([wk, wv], axis=1).astype(BF16)


def kernel(x, positions, ln_in_g, ln_in_b, w_in, q_a_norm_g, w_q_b, kv_a_norm_g, w_kv_b, hgrn_lb_logits,
           mla_out_g, hgrn_out_g, w_o, ln1_g, ln1_b, w_router, b_router, w_up, b_up, w_down, b_down,
           ln2_g, ln2_b):
    B, S, D = x.shape
    assert D == D_MODEL and w_in.shape[0] == DEPTH and hgrn_lb_logits.shape[0] == DEPTH + 1
    T = B * S
    xt = x.reshape(T, D)
    r2 = lambda a: a.reshape(1, -1)

    c_tab, s1_tab, s2_tab = _rope_tables(positions)
    w_in_p, w_qb_p, w_kv_p = _prep_weights(w_in[0], w_q_b[0], w_kv_b[0])
    q, k, v, hq, hk, hg, hv, gate = _in_proj(
        xt, r2(ln_in_g), r2(ln_in_b), w_in_p, q_a_norm_g, w_qb_p, kv_a_norm_g, w_kv_p,
        hgrn_lb_logits, c_tab, s1_tab, s2_tab)
    attn = _attention(q, k, v, B, S)
    hgo = _hgrn(hq, hk, hg, hv, gate, hgrn_out_g, B, S)
    h1, h1p, idx4, gate4, rank4, counts = _out_proj(
        xt, r2(ln_in_g), r2(ln_in_b), attn, hgo, mla_out_g, w_o[0].astype(BF16), ln1_g, ln1_b,
        w_router[0], b_router)

    n_blocks = T * TOP_K // ROW_BLOCK + N_EXPERTS
    cnt = counts[0].astype(I32)
    padded = (cnt + ROW_BLOCK - 1) // ROW_BLOCK * ROW_BLOCK
    pad_end = jnp.cumsum(padded)
    pad_start = pad_end - padded
    dest = pad_start[idx4] + rank4
    block_e = jnp.minimum(jnp.searchsorted(pad_end, jnp.arange(n_blocks, dtype=I32) * ROW_BLOCK,
                                           side="right"), N_EXPERTS - 1).astype(I32)
    n_used = (pad_end[-1:] // ROW_BLOCK).astype(I32)

    xs = _dispatch(dest, h1p, n_blocks * ROW_BLOCK)
    ys = _experts(block_e, n_used, xs, w_up[0], b_up[0], w_down[0], b_down[0])
    out = _combine(dest, h1, gate4, ln2_g, ln2_b, ys)
    return out.reshape(B, S, D)
```

```python
import functools

import numpy as np
import jax
import jax.numpy as jnp
from jax import lax
from jax.experimental import pallas as pl
from jax.experimental.pallas import tpu as pltpu

F32 = jnp.float32
BF16 = jnp.bfloat16
I32 = jnp.int32

D_MODEL = 1024
CHUNK = 64
MLA_HEADS = 8
MLA_NOPE = 64
MLA_ROPE = 32
MLA_V = 64
Q_LORA = 384
KV_LORA = 256
ROPE_THETA = 10000.0
HG_HEADS = 4
HG_DK = 128
HG_DV = 128
HG_W = HG_HEADS * HG_DK
N_EXPERTS = 32
TOP_K = 4
D_EXPERT = 1024
SWIGLU_LIMIT = 7.0
SWIGLU_ALPHA = 1.702
LN_EPS = 1e-5
RMS_EPS = 1e-6
DEPTH = 1
DEEPNORM_ALPHA = (2.0 * DEPTH) ** 0.25
LOG2E = 1.4426950408889634

LANES = 128
HEAD_PAD = 128
HALF_ROPE = MLA_ROPE // 2
KR_OFF = Q_LORA + KV_LORA
HQ_OFF = KR_OFF + HEAD_PAD
IN_COLS_PAD = HQ_OFF + 4 * HG_W

TM_IN = 512
TQ = 256
HG_ROWS = 256
TM_OUT = 256
TM_DISP = 256
ROW_BLOCK = 256
VMEM_LIMIT = 52 * 1024 * 1024


def _cparams(sem, vmem=VMEM_LIMIT):
    return pltpu.CompilerParams(dimension_semantics=sem, vmem_limit_bytes=vmem)


def _layer_norm(x, g, b):
    mu = jnp.mean(x, axis=-1, keepdims=True)
    xc = x - mu
    var = jnp.mean(xc * xc, axis=-1, keepdims=True)
    return xc * lax.rsqrt(var + LN_EPS) * g + b


def _rms_norm(x, g):
    return x * lax.rsqrt(jnp.mean(x * x, axis=-1, keepdims=True) + RMS_EPS) * g


def _sigmoid_pair(x):
    e = jnp.exp(-jnp.abs(x))
    big = 1.0 / (1.0 + e)
    small = e * big
    pos = x >= 0
    return jnp.where(pos, big, small), jnp.where(pos, small, big)


def _cos_sin_body(ang_ref, cos_ref, sin_ref):
    a = ang_ref[...]
    cos_ref[...] = jnp.cos(a)
    sin_ref[...] = jnp.sin(a)


def _rope_tables(positions):
    T = positions.size
    inv_freq = ROPE_THETA ** (-jnp.arange(0, MLA_ROPE, 2, dtype=F32) / MLA_ROPE)
    ang = (positions.astype(F32)[..., None] * inv_freq).reshape(T * HALF_ROPE // LANES, LANES)
    cos, sin = pl.pallas_call(
        _cos_sin_body,
        out_shape=(jax.ShapeDtypeStruct(ang.shape, F32),) * 2,
        name="rope_cos_sin",
    )(ang)
    cos = cos.reshape(T, HALF_ROPE)
    sin = sin.reshape(T, HALF_ROPE)
    ones = jnp.ones((T, MLA_NOPE), F32)
    z_n = jnp.zeros((T, MLA_NOPE), F32)
    z_h = jnp.zeros((T, HALF_ROPE), F32)
    z_p = jnp.zeros((T, HEAD_PAD - MLA_NOPE - MLA_ROPE), F32)
    c_tab = jnp.concatenate([ones, cos, cos, z_p], axis=1)
    s1_tab = jnp.concatenate([z_n, z_h, sin, z_p], axis=1)
    s2_tab = jnp.concatenate([z_n, -sin, z_h, z_p], axis=1)
    return c_tab, s1_tab, s2_tab


def _rope(x, c, s1, s2, reps):
    width = reps * HEAD_PAD
    if reps > 1:
        c = jnp.concatenate([c] * reps, axis=1)
        s1 = jnp.concatenate([s1] * reps, axis=1)
        s2 = jnp.concatenate([s2] * reps, axis=1)
    return (x * c + pltpu.roll(x, HALF_ROPE, 1) * s1
            + pltpu.roll(x, width - HALF_ROPE, 1) * s2)


def _in_proj_body(x_ref, lng_ref, lnb_ref, win_ref, qag_ref, wqb_ref, kvg_ref, wkv_ref, lbl_ref,
                  c_ref, s1_ref, s2_ref,
                  q_out, k_out, v_out, hq_out, hk_out, hg_out, hv_out, gate_out):
    h = _layer_norm(x_ref[...], lng_ref[...], lnb_ref[...])
    p = jnp.dot(h.astype(BF16), win_ref[...], preferred_element_type=F32)
    c, s1, s2 = c_ref[...], s1_ref[...], s2_ref[...]

    qn = _rms_norm(p[:, :Q_LORA], qag_ref[...]).astype(BF16)
    q = jnp.dot(qn, wqb_ref[...], preferred_element_type=F32)
    scale = (MLA_NOPE + MLA_ROPE) ** -0.5 * LOG2E
    q_out[...] = (_rope(q, c, s1, s2, MLA_HEADS) * scale).astype(BF16)
    kvn = _rms_norm(p[:, Q_LORA:KR_OFF], kvg_ref[...]).astype(BF16)
    kv = jnp.dot(kvn, wkv_ref[...], preferred_element_type=F32)
    kr = _rope(p[:, KR_OFF:HQ_OFF], c, s1, s2, 1)
    k_out[...] = (kv[:, :MLA_HEADS * HEAD_PAD] + jnp.concatenate([kr] * MLA_HEADS, axis=1)).astype(BF16)
    lane = lax.broadcasted_iota(I32, (1, MLA_HEADS * HEAD_PAD), 1)
    ones_lane = (lane % HEAD_PAD == MLA_V).astype(F32)
    v_out[...] = (kv[:, MLA_HEADS * HEAD_PAD:] + ones_lane).astype(BF16)

    lbl = lbl_ref[...]
    le = jnp.exp(lbl - jnp.max(lbl, axis=0, keepdims=True))
    lb = le[0:1] / jnp.sum(le, axis=0, keepdims=True)
    hq = p[:, HQ_OFF:HQ_OFF + HG_W]
    hf = p[:, HQ_OFF + HG_W:HQ_OFF + 2 * HG_W]
    hi = p[:, HQ_OFF + 2 * HG_W:HQ_OFF + 3 * HG_W]
    hg = p[:, HQ_OFF + 3 * HG_W:HQ_OFF + 4 * HG_W]
    hq_out[...] = (hq * _sigmoid_pair(hq)[0]).astype(BF16)
    sg, one_minus_sg = _sigmoid_pair(hf)
    hk_out[...] = ((1.0 - lb) * one_minus_sg).astype(BF16)
    hg_out[...] = jnp.log(lb + (1.0 - lb) * sg)
    hv_out[...] = hi.astype(BF16)
    gate_out[...] = (hg * _sigmoid_pair(hg)[0]).astype(BF16)


def _in_proj(xt, ln_g, ln_b, w_in_p, qa_g, w_qb_p, kv_g, w_kv_p, lb_logits, c_tab, s1_tab, s2_tab):
    T = xt.shape[0]
    row = lambda w: pl.BlockSpec((TM_IN, w), lambda i: (i, 0))
    full = lambda a: pl.BlockSpec(a.shape, lambda i: (0, 0))
    outs = [(MLA_HEADS * HEAD_PAD, BF16), (MLA_HEADS * HEAD_PAD, BF16), (MLA_HEADS * HEAD_PAD, BF16),
            (HG_W, BF16), (HG_W, BF16), (HG_W, F32), (HG_W, BF16), (HG_W, BF16)]
    return pl.pallas_call(
        _in_proj_body,
        grid=(T // TM_IN,),
        in_specs=[row(D_MODEL), full(ln_g), full(ln_b), full(w_in_p), full(qa_g), full(w_qb_p),
                  full(kv_g), full(w_kv_p), full(lb_logits), row(HEAD_PAD), row(HEAD_PAD), row(HEAD_PAD)],
        out_specs=[row(w) for w, _ in outs],
        out_shape=[jax.ShapeDtypeStruct((T, w), d) for w, d in outs],
        compiler_params=_cparams(("parallel",)),
        name="in_proj",
    )(xt, ln_g, ln_b, w_in_p, qa_g, w_qb_p, kv_g, w_kv_p, lb_logits, c_tab, s1_tab, s2_tab)


def _attn_body(q_ref, k_ref, v_ref, o_ref):
    n_q = q_ref.shape[1] // TQ
    dn = (((1,), (1,)), ((), ()))
    row_c = lax.broadcasted_iota(I32, (TQ, TQ), 0) // CHUNK
    col_c = lax.broadcasted_iota(I32, (TQ, TQ), 1) // CHUNK
    diag_mask = col_c <= row_c
    heads = [slice(hh * HEAD_PAD, (hh + 1) * HEAD_PAD) for hh in range(2)]
    lane = lax.broadcasted_iota(I32, (TQ, 2 * MLA_V), 1)
    for qi in range(n_q):
        r0, kv = qi * TQ, (qi + 1) * TQ
        res = []
        for hs in heads:
            s = lax.dot_general(q_ref[0, r0:kv, hs], k_ref[0, :kv, hs], dn, preferred_element_type=F32)
            s_diag = jnp.where(diag_mask, s[:, r0:], -jnp.inf)
            m = jnp.max(s_diag, axis=-1, keepdims=True)
            if qi:
                m = jnp.maximum(m, jnp.max(s[:, :r0], axis=-1, keepdims=True))
                p = jnp.concatenate([jnp.exp2(s[:, :r0] - m), jnp.exp2(s_diag - m)], axis=1)
            else:
                p = jnp.exp2(s_diag - m)
            a = jnp.dot(p.astype(BF16), v_ref[0, :kv, hs], preferred_element_type=F32)
            res.append(a * (1.0 / a[:, MLA_V:MLA_V + 1]))
        o_ref[0, r0:kv, :] = jnp.where(lane < MLA_V, res[0], pltpu.roll(res[1], MLA_V, 1)).astype(BF16)


def _attention(q, k, v, B, S):
    w = MLA_HEADS * HEAD_PAD
    blk = pl.BlockSpec((1, S, 2 * HEAD_PAD), lambda b, h: (b, 0, h))
    out = pl.pallas_call(
        _attn_body,
        grid=(B, MLA_HEADS // 2),
        in_specs=[blk, blk, blk],
        out_specs=pl.BlockSpec((1, S, 2 * MLA_V), lambda b, h: (b, 0, h)),
        out_shape=jax.ShapeDtypeStruct((B, S, MLA_HEADS * MLA_V), BF16),
        compiler_params=_cparams(("parallel", "parallel")),
        name="mla_attention",
    )(q.reshape(B, S, w), k.reshape(B, S, w), v.reshape(B, S, w))
    return out.reshape(B * S, MLA_HEADS * MLA_V)


N_LEVELS = 6
Z_BLOCKS = 2 * N_LEVELS + 2


def _hgrn_constants():
    z = np.zeros((Z_BLOCKS, CHUNK, CHUNK), np.float32)
    masks = np.zeros((N_LEVELS, CHUNK, CHUNK), np.float32)
    idx = np.arange(CHUNK)
    for l in range(N_LEVELS):
        hs = CHUNK >> (l + 1)
        parent = idx // (2 * hs)
        right = (idx % (2 * hs)) >= hs
        mid = parent * 2 * hs + hs
        for t in range(CHUNK):
            if right[t]:
                z[l, t, mid[t]:t + 1] = 1.0
            else:
                z[N_LEVELS + l, t, t + 1:mid[t]] = 1.0
        masks[l] = (parent[:, None] == parent[None, :]) & right[:, None] & (~right)[None, :]
    z[2 * N_LEVELS] = np.tril(np.ones((CHUNK, CHUNK), np.float32))
    z[2 * N_LEVELS + 1] = np.triu(np.ones((CHUNK, CHUNK), np.float32), 1)
    z = z.reshape(Z_BLOCKS * CHUNK, CHUNK)
    return np.concatenate([z, z], axis=1), masks


def _hgrn_body(q_ref, k_ref, g_ref, v_ref, gate_ref, og_ref, z_ref, mask_ref, o_ref, state_ref):
    @pl.when(pl.program_id(1) == 0)
    def _():
        state_ref[...] = jnp.zeros_like(state_ref)

    dn_t = (((1,), (1,)), ((), ()))
    eye = (lax.broadcasted_iota(I32, (CHUNK, CHUNK), 0)
           == lax.broadcasted_iota(I32, (CHUNK, CHUNK), 1)).astype(F32)
    for c in range(HG_ROWS // CHUNK):
        rows = slice(c * CHUNK, (c + 1) * CHUNK)
        for h in range(HG_HEADS):
            cols = slice(h * HG_DK, (h + 1) * HG_DK)
            g2 = g_ref[0, rows, cols] * LOG2E
            g_hi = g2.astype(BF16)
            g_lo = (g2 - g_hi.astype(F32)).astype(BF16)
            expo = jnp.dot(z_ref[...], jnp.concatenate([g_hi, g_lo], axis=0),
                           preferred_element_type=F32)
            dec = jnp.exp2(expo)
            blk = lambda i: dec[i * CHUNK:(i + 1) * CHUNK]
            q = q_ref[0, rows, cols].astype(F32)
            k = k_ref[0, rows, cols].astype(F32)
            v = v_ref[0, rows, cols]
            a = eye * jnp.sum(q * k, axis=-1, keepdims=True)
            for l in range(N_LEVELS):
                qt = (q * blk(l)).astype(BF16)
                kt = (k * blk(N_LEVELS + l)).astype(BF16)
                a = a + mask_ref[l] * lax.dot_general(qt, kt, dn_t, preferred_element_type=F32)
            o = jnp.dot(a.astype(BF16), v, preferred_element_type=F32)
            st = state_ref[h]
            d_in = blk(2 * N_LEVELS)
            o = o + lax.dot_general((q * d_in).astype(BF16), st.astype(BF16), dn_t,
                                    preferred_element_type=F32)
            kh = (k * blk(2 * N_LEVELS + 1)).astype(BF16)
            upd = lax.dot_general(v, kh, (((0,), (0,)), ((), ())), preferred_element_type=F32)
            state_ref[h] = st * d_in[CHUNK - 1:CHUNK] + upd
            o = _rms_norm(o, og_ref[:, cols]) * gate_ref[0, rows, cols].astype(F32)
            o_ref[0, rows, cols] = o.astype(BF16)


def _hgrn(hq, hk, hg, hv, gate, out_g, B, S):
    zc, masks = _hgrn_constants()
    zc = jnp.asarray(zc, BF16)
    masks = jnp.asarray(masks, F32)
    blk = pl.BlockSpec((1, HG_ROWS, HG_W), lambda b, i: (b, i, 0))
    r3 = lambda a: a.reshape(B, S, HG_W)
    out = pl.pallas_call(
        _hgrn_body,
        grid=(B, S // HG_ROWS),
        in_specs=[blk, blk, blk, blk, blk,
                  pl.BlockSpec((1, HG_W), lambda b, i: (0, 0)),
                  pl.BlockSpec(zc.shape, lambda b, i: (0, 0)),
                  pl.BlockSpec(masks.shape, lambda b, i: (0, 0, 0))],
        out_specs=blk,
        out_shape=jax.ShapeDtypeStruct((B, S, HG_W), BF16),
        scratch_shapes=[pltpu.VMEM((HG_HEADS, HG_DV, HG_DK), F32)],
        compiler_params=_cparams(("parallel", "arbitrary")),
        name="hgrn2",
    )(r3(hq), r3(hk), r3(hg), r3(hv), r3(gate), out_g, zc, masks)
    return out.reshape(B * S, HG_W)


def _pack_bf16_pair(lo, hi):
    lo_b = pltpu.bitcast(lo.astype(BF16).astype(F32), I32)
    hi_b = pltpu.bitcast(hi.astype(BF16).astype(F32), I32)
    return hi_b | lax.shift_right_logical(lo_b, 16)


def _unpack_bf16_pair(w):
    lo = pltpu.bitcast(lax.shift_left(w, 16), F32)
    hi = pltpu.bitcast(w & jnp.int32(-65536), F32)
    return lo, hi


def _out_proj_body(x_ref, lng_ref, lnb_ref, attn_ref, hgo_ref, mg_ref, wo_ref, l1g_ref, l1b_ref,
                   wr_ref, br_ref,
                   h1_out, h1p_out, idx_out, gate_out, rank_out, cnt_out, carry_ref):
    @pl.when(pl.program_id(0) == 0)
    def _():
        carry_ref[...] = jnp.zeros_like(carry_ref)

    h = _layer_norm(x_ref[...], lng_ref[...], lnb_ref[...])
    attn = _rms_norm(attn_ref[...].astype(F32), mg_ref[...]).astype(BF16)
    mix_in = jnp.concatenate([attn, hgo_ref[...]], axis=1)
    mix = jnp.dot(mix_in, wo_ref[...], preferred_element_type=F32)
    h1 = _layer_norm(DEEPNORM_ALPHA * h + mix, l1g_ref[...], l1b_ref[...])
    h1_out[...] = h1
    h1p_out[...] = _pack_bf16_pair(h1[:, :D_MODEL // 2], h1[:, D_MODEL // 2:])

    logits_t = lax.dot_general(wr_ref[...], h1, (((1,), (1,)), ((), ())), preferred_element_type=F32,
                               precision=lax.Precision.HIGHEST) + br_ref[...]
    tm = h1.shape[0]
    sub = lax.broadcasted_iota(I32, (N_EXPERTS, tm), 0)
    work = logits_t
    vals, idxs = [], []
    for _ in range(TOP_K):
        mx = jnp.max(work, axis=0, keepdims=True)
        ix = jnp.min(jnp.where(work == mx, sub, N_EXPERTS), axis=0, keepdims=True)
        vals.append(mx)
        idxs.append(ix)
        work = jnp.where(sub == ix, -jnp.inf, work)
    exps = [jnp.exp(vl - vals[0]) for vl in vals]
    den = exps[0] + exps[1] + exps[2] + exps[3]

    sel = jnp.zeros((N_EXPERTS, tm), F32)
    for ix in idxs:
        sel = sel + (sub == ix).astype(F32)
    earlier = (lax.broadcasted_iota(I32, (tm, tm), 0)
               < lax.broadcasted_iota(I32, (tm, tm), 1)).astype(BF16)
    before = jnp.dot(sel.astype(BF16), earlier, preferred_element_type=F32) + carry_ref[...]
    carry_ref[...] = carry_ref[...] + jnp.sum(sel, axis=1, keepdims=True)
    cnt_out[...] = carry_ref[...]

    ranks = [jnp.sum(jnp.where(sub == ix, before, 0.0), axis=0, keepdims=True) for ix in idxs]
    idx_out[...] = jnp.concatenate(idxs, axis=0)
    gate_out[...] = jnp.concatenate([e / den for e in exps], axis=0)
    rank_out[...] = jnp.concatenate(ranks, axis=0).astype(I32)


def _out_proj(xt, ln_g, ln_b, attn, hgo, mla_g, w_o, l1g, l1b, w_r, b_r):
    T = xt.shape[0]
    row = lambda w: pl.BlockSpec((TM_OUT, w), lambda i: (i, 0))
    full = lambda a: pl.BlockSpec(a.shape, lambda i: (0, 0))
    col = pl.BlockSpec((TOP_K, TM_OUT), lambda i: (0, i))
    return pl.pallas_call(
        _out_proj_body,
        grid=(T // TM_OUT,),
        in_specs=[row(D_MODEL), full(ln_g), full(ln_b), row(MLA_HEADS * MLA_V), row(HG_W), full(mla_g),
                  full(w_o), full(l1g), full(l1b), full(w_r), full(b_r)],
        out_specs=[row(D_MODEL), row(D_MODEL // 2), col, col, col,
                   pl.BlockSpec((N_EXPERTS, 1), lambda i: (0, 0))],
        out_shape=[jax.ShapeDtypeStruct((T, D_MODEL), F32), jax.ShapeDtypeStruct((T, D_MODEL // 2), I32),
                   jax.ShapeDtypeStruct((TOP_K, T), I32), jax.ShapeDtypeStruct((TOP_K, T), F32),
                   jax.ShapeDtypeStruct((TOP_K, T), I32), jax.ShapeDtypeStruct((N_EXPERTS, 1), F32)],
        scratch_shapes=[pltpu.VMEM((N_EXPERTS, 1), F32)],
        compiler_params=_cparams(("arbitrary",)),
        name="out_proj_router",
    )(xt, ln_g, ln_b, attn, hgo, mla_g, w_o, l1g, l1b, w_r, b_r)


def _dispatch_body(dest_ref, h_ref, xs_in, xs_out, sem):
    del xs_in

    def copy(t, j):
        d = dest_ref[0, 0, j * TM_DISP + t]
        return pltpu.make_async_copy(h_ref.at[pl.ds(t, 1), :], xs_out.at[pl.ds(d, 1), :], sem)

    def start(t, c):
        for j in range(TOP_K):
            copy(t, j).start()
        return c

    def wait(t, c):
        for j in range(TOP_K):
            copy(t, j).wait()
        return c

    lax.fori_loop(0, TM_DISP, start, 0)
    lax.fori_loop(0, TM_DISP, wait, 0)


def _dispatch(dest, h1p, n_rows):
    T = h1p.shape[0]
    nt = T // TM_DISP
    xs0 = jnp.zeros((n_rows, D_MODEL // 2), I32)
    return pl.pallas_call(
        _dispatch_body,
        grid=(nt,),
        in_specs=[pl.BlockSpec((1, 1, TM_DISP * TOP_K), lambda i: (i, 0, 0), memory_space=pltpu.SMEM),
                  pl.BlockSpec((TM_DISP, D_MODEL // 2), lambda i: (i, 0)),
                  pl.BlockSpec(memory_space=pl.ANY)],
        out_specs=pl.BlockSpec(memory_space=pl.ANY),
        out_shape=jax.ShapeDtypeStruct((n_rows, D_MODEL // 2), I32),
        scratch_shapes=[pltpu.SemaphoreType.DMA(())],
        input_output_aliases={2: 0},
        compiler_params=_cparams(("arbitrary",)),
        name="moe_dispatch",
    )(dest, h1p, xs0)


def _experts_body(be_ref, nu_ref, xs_ref, wu_ref, bu_ref, wd_ref, bd_ref, ys_ref, wu_bf, wd_bf):
    i = pl.program_id(0)
    prev = be_ref[jnp.maximum(i - 1, 0)]
    fresh = jnp.logical_or(i == 0, be_ref[i] != prev)

    @pl.when(jnp.logical_and(fresh, i < nu_ref[0]))
    def _():
        wu_bf[...] = wu_ref[0].astype(BF16)
        wd_bf[...] = wd_ref[0].astype(BF16)

    @pl.when(i < nu_ref[0])
    def _():
        lo, hi = _unpack_bf16_pair(xs_ref[...])
        x = jnp.concatenate([lo, hi], axis=1).astype(BF16)
        hb = jnp.dot(x, wu_bf[...], preferred_element_type=F32) + bu_ref[0]
        glu = jnp.minimum(hb[:, :D_EXPERT], SWIGLU_LIMIT)
        lin = jnp.clip(hb[:, D_EXPERT:], -SWIGLU_LIMIT, SWIGLU_LIMIT)
        act = glu * (1.0 / (1.0 + jnp.exp(-SWIGLU_ALPHA * glu))) * (lin + 1.0)
        ys_ref[...] = jnp.dot(act.astype(BF16), wd_bf[...], preferred_element_type=F32) + bd_ref[0]

    @pl.when(i >= nu_ref[0])
    def _():
        ys_ref[...] = jnp.zeros_like(ys_ref)


def _experts(block_e, n_used, xs, w_up, b_up, w_down, b_down):
    n_rows = xs.shape[0]
    nb = n_rows // ROW_BLOCK
    grid_spec = pltpu.PrefetchScalarGridSpec(
        num_scalar_prefetch=2,
        grid=(nb,),
        in_specs=[pl.BlockSpec((ROW_BLOCK, D_MODEL // 2), lambda i, be, nu: (i, 0)),
                  pl.BlockSpec((1, D_MODEL, 2 * D_EXPERT), lambda i, be, nu: (be[i], 0, 0)),
                  pl.BlockSpec((1, 1, 2 * D_EXPERT), lambda i, be, nu: (be[i], 0, 0)),
                  pl.BlockSpec((1, D_EXPERT, D_MODEL), lambda i, be, nu: (be[i], 0, 0)),
                  pl.BlockSpec((1, 1, D_MODEL), lambda i, be, nu: (be[i], 0, 0))],
        out_specs=pl.BlockSpec((ROW_BLOCK, D_MODEL), lambda i, be, nu: (i, 0)),
        scratch_shapes=[pltpu.VMEM((D_MODEL, 2 * D_EXPERT), BF16), pltpu.VMEM((D_EXPERT, D_MODEL), BF16)],
    )
    return pl.pallas_call(
        _experts_body,
        grid_spec=grid_spec,
        out_shape=jax.ShapeDtypeStruct((n_rows, D_MODEL), F32),
        compiler_params=_cparams(("arbitrary",)),
        name="moe_experts",
    )(block_e, n_used, xs, w_up, b_up.reshape(N_EXPERTS, 1, 2 * D_EXPERT), w_down,
      b_down.reshape(N_EXPERTS, 1, D_MODEL))


def _combine_body(dest_ref, h1_ref, gate_ref, l2g_ref, l2b_ref, ys_hbm, o_ref, buf, sem):
    def copy(t, j):
        d = dest_ref[0, 0, j * TM_DISP + t]
        return pltpu.make_async_copy(ys_hbm.at[pl.ds(d, 1), :], buf.at[j, pl.ds(t, 1), :], sem)

    def start(t, c):
        for j in range(TOP_K):
            copy(t, j).start()
        return c

    def wait(t, c):
        for j in range(TOP_K):
            copy(t, j).wait()
        return c

    lax.fori_loop(0, TM_DISP, start, 0)
    lax.fori_loop(0, TM_DISP, wait, 0)
    gate = gate_ref[...]
    ffn = buf[0] * gate[:, 0:1]
    for j in range(1, TOP_K):
        ffn = ffn + buf[j] * gate[:, j:j + 1]
    o_ref[...] = _layer_norm(DEEPNORM_ALPHA * h1_ref[...] + ffn, l2g_ref[...], l2b_ref[...])


def _combine(dest, h1, gate4, l2g, l2b, ys):
    T = h1.shape[0]
    nt = T // TM_DISP
    return pl.pallas_call(
        _combine_body,
        grid=(nt,),
        in_specs=[pl.BlockSpec((1, 1, TM_DISP * TOP_K), lambda i: (i, 0, 0), memory_space=pltpu.SMEM),
                  pl.BlockSpec((TM_DISP, D_MODEL), lambda i: (i, 0)),
                  pl.BlockSpec((TM_DISP, TOP_K), lambda i: (i, 0)),
                  pl.BlockSpec(l2g.shape, lambda i: (0, 0)),
                  pl.BlockSpec(l2b.shape, lambda i: (0, 0)),
                  pl.BlockSpec(memory_space=pl.ANY)],
        out_specs=pl.BlockSpec((TM_DISP, D_MODEL), lambda i: (i, 0)),
        out_shape=jax.ShapeDtypeStruct((T, D_MODEL), F32),
        scratch_shapes=[pltpu.VMEM((TOP_K, TM_DISP, D_MODEL), F32), pltpu.SemaphoreType.DMA(())],
        compiler_params=_cparams(("arbitrary",)),
        name="moe_combine",
    )(dest, h1, gate4, l2g, l2b, ys)


def _prep_weights(w_in, w_q_b, w_kv_b):
    zc = lambda n: jnp.zeros((D_MODEL, n), F32)
    w_in_p = jnp.concatenate(
        [w_in[:, :KR_OFF], zc(MLA_NOPE), w_in[:, KR_OFF:KR_OFF + MLA_ROPE],
         zc(HEAD_PAD - MLA_NOPE - MLA_ROPE), w_in[:, KR_OFF + MLA_ROPE:]], axis=1).astype(BF16)
    qd = MLA_NOPE + MLA_ROPE
    wq = w_q_b.reshape(Q_LORA, MLA_HEADS, qd)
    wq = jnp.pad(wq, ((0, 0), (0, 0), (0, HEAD_PAD - qd))).reshape(Q_LORA, MLA_HEADS * HEAD_PAD)
    wkv = w_kv_b.reshape(KV_LORA, MLA_HEADS, MLA_NOPE + MLA_V)
    wk = jnp.pad(wkv[:, :, :MLA_NOPE], ((0, 0), (0, 0), (0, HEAD_PAD - MLA_NOPE)))
    wk = wk.reshape(KV_LORA, MLA_HEADS * HEAD_PAD)
    wv = jnp.pad(wkv[:, :, MLA_NOPE:], ((0, 0), (0, 0), (0, HEAD_PAD - MLA_V)))
    wv = wv.reshape(KV_LORA, MLA_HEADS * HEAD_PAD)
    return w_in_p, wq.astype(BF16), jnp.concatenate([wk, wv], axis=1).astype(BF16)


def kernel(x, positions, ln_in_g, ln_in_b, w_in, q_a_norm_g, w_q_b, kv_a_norm_g, w_kv_b, hgrn_lb_logits,
           mla_out_g, hgrn_out_g, w_o, ln1_g, ln1_b, w_router, b_router, w_up, b_up, w_down, b_down,
           ln2_g, ln2_b):
    B, S, D = x.shape
    assert D == D_MODEL and w_in.shape[0] == DEPTH and hgrn_lb_logits.shape[0] == DEPTH + 1
    T = B * S
    xt = x.reshape(T, D)
    r2 = lambda a: a.reshape(1, -1)

    c_tab, s1_tab, s2_tab = _rope_tables(positions)
    w_in_p, w_qb_p, w_kv_p = _prep_weights(w_in[0], w_q_b[0], w_kv_b[0])
    q, k, v, hq, hk, hg, hv, gate = _in_proj(
        xt, r2(ln_in_g), r2(ln_in_b), w_in_p, q_a_norm_g, w_qb_p, kv_a_norm_g, w_kv_p,
        hgrn_lb_logits, c_tab, s1_tab, s2_tab)
    attn = _attention(q, k, v, B, S)
    hgo = _hgrn(hq, hk, hg, hv, gate, hgrn_out_g, B, S)
    h1, h1p, idx_t, gate_t, rank_t, counts = _out_proj(
        xt, r2(ln_in_g), r2(ln_in_b), attn, hgo, mla_out_g, w_o[0].astype(BF16), ln1_g, ln1_b,
        w_router[0].T, b_router.reshape(N_EXPERTS, 1))

    n_blocks = T * TOP_K // ROW_BLOCK + N_EXPERTS
    cnt = counts[:, 0].astype(I32)
    padded = (cnt + ROW_BLOCK - 1) // ROW_BLOCK * ROW_BLOCK
    pad_end = jnp.cumsum(padded)
    pad_start = pad_end - padded
    experts = jnp.arange(N_EXPERTS, dtype=I32)
    dest_t = rank_t + jnp.sum(jnp.where(idx_t[..., None] == experts, pad_start, 0), axis=-1)
    block_row = jnp.arange(n_blocks, dtype=I32) * ROW_BLOCK
    block_e = jnp.minimum(jnp.sum((pad_end[None, :] <= block_row[:, None]).astype(I32), axis=1),
                          N_EXPERTS - 1)
    n_used = pad_end[-1:] // ROW_BLOCK
    nt = T // TM_DISP
    dest = dest_t.reshape(TOP_K, nt, TM_DISP).transpose(1, 0, 2).reshape(nt, 1, TOP_K * TM_DISP)

    xs = _dispatch(dest, h1p, n_blocks * ROW_BLOCK)
    ys = _experts(block_e, n_used, xs, w_up[0], b_up[0], w_down[0], b_down[0])
    out = _combine(dest, h1, gate_t.T, ln2_g, ln2_b, ys)
    return out.reshape(B, S, D)
```

```python
import numpy as np
import jax
import jax.numpy as jnp
from jax import lax
from jax.experimental import pallas as pl
from jax.experimental.pallas import tpu as pltpu

F32 = jnp.float32
BF16 = jnp.bfloat16
I32 = jnp.int32

D_MODEL = 1024
CHUNK = 64
MLA_HEADS = 8
MLA_NOPE = 64
MLA_ROPE = 32
MLA_V = 64
Q_LORA = 384
KV_LORA = 256
ROPE_THETA = 10000.0
HG_HEADS = 4
HG_DK = 128
HG_DV = 128
HG_W = HG_HEADS * HG_DK
N_EXPERTS = 32
TOP_K = 4
D_EXPERT = 1024
SWIGLU_LIMIT = 7.0
SWIGLU_ALPHA = 1.702
LN_EPS = 1e-5
RMS_EPS = 1e-6
DEPTH = 1
DEEPNORM_ALPHA = (2.0 * DEPTH) ** 0.25
LOG2E = 1.4426950408889634

LANES = 128
SUBLANES = 8
HEAD_PAD = 128
HALF_ROPE = MLA_ROPE // 2
KR_OFF = Q_LORA + KV_LORA
HQ_OFF = KR_OFF + HEAD_PAD
IN_COLS_PAD = HQ_OFF + 4 * HG_W

TM_IN = 512
TQ = 256
HG_ROWS = 256
TM_OUT = 256
TM_DISP = 256
ROW_BLOCK = 256
VMEM_LIMIT = 52 * 1024 * 1024


def _cparams(sem, vmem=VMEM_LIMIT):
    return pltpu.CompilerParams(dimension_semantics=sem, vmem_limit_bytes=vmem)


def _layer_norm(x, g, b):
    mu = jnp.mean(x, axis=-1, keepdims=True)
    xc = x - mu
    var = jnp.mean(xc * xc, axis=-1, keepdims=True)
    return xc * lax.rsqrt(var + LN_EPS) * g + b


def _rms_norm(x, g):
    return x * lax.rsqrt(jnp.mean(x * x, axis=-1, keepdims=True) + RMS_EPS) * g


def _sigmoid_pair(x):
    e = jnp.exp(-jnp.abs(x))
    big = 1.0 / (1.0 + e)
    small = e * big
    pos = x >= 0
    return jnp.where(pos, big, small), jnp.where(pos, small, big)


def _cos_sin_body(ang_ref, cos_ref, sin_ref):
    a = ang_ref[...]
    cos_ref[...] = jnp.cos(a)
    sin_ref[...] = jnp.sin(a)


def _rope_tables(positions):
    T = positions.size
    inv_freq = ROPE_THETA ** (-jnp.arange(0, MLA_ROPE, 2, dtype=F32) / MLA_ROPE)
    ang = (positions.astype(F32)[..., None] * inv_freq).reshape(T * HALF_ROPE // LANES, LANES)
    cos, sin = pl.pallas_call(
        _cos_sin_body,
        out_shape=(jax.ShapeDtypeStruct(ang.shape, F32),) * 2,
        name="rope_cos_sin",
    )(ang)
    cos = cos.reshape(T, HALF_ROPE)
    sin = sin.reshape(T, HALF_ROPE)
    ones = jnp.ones((T, MLA_NOPE), F32)
    z_n = jnp.zeros((T, MLA_NOPE), F32)
    z_h = jnp.zeros((T, HALF_ROPE), F32)
    z_p = jnp.zeros((T, HEAD_PAD - MLA_NOPE - MLA_ROPE), F32)
    c_tab = jnp.concatenate([ones, cos, cos, z_p], axis=1)
    s1_tab = jnp.concatenate([z_n, z_h, sin, z_p], axis=1)
    s2_tab = jnp.concatenate([z_n, -sin, z_h, z_p], axis=1)
    return c_tab, s1_tab, s2_tab


def _rope(x, c, s1, s2, reps):
    width = reps * HEAD_PAD
    if reps > 1:
        c = jnp.concatenate([c] * reps, axis=1)
        s1 = jnp.concatenate([s1] * reps, axis=1)
        s2 = jnp.concatenate([s2] * reps, axis=1)
    return (x * c + pltpu.roll(x, HALF_ROPE, 1) * s1
            + pltpu.roll(x, width - HALF_ROPE, 1) * s2)


def _in_proj_body(x_ref, lng_ref, lnb_ref, win_ref, qag_ref, wqb_ref, kvg_ref, wkv_ref, lbl_ref,
                  c_ref, s1_ref, s2_ref,
                  q_out, k_out, v_out, hq_out, hk_out, hg_out, hv_out, gate_out):
    h = _layer_norm(x_ref[...], lng_ref[...], lnb_ref[...])
    p = jnp.dot(h.astype(BF16), win_ref[...], preferred_element_type=F32)
    c, s1, s2 = c_ref[...], s1_ref[...], s2_ref[...]

    qn = _rms_norm(p[:, :Q_LORA], qag_ref[...]).astype(BF16)
    q = jnp.dot(qn, wqb_ref[...], preferred_element_type=F32)
    scale = (MLA_NOPE + MLA_ROPE) ** -0.5 * LOG2E
    q_out[...] = (_rope(q, c, s1, s2, MLA_HEADS) * scale).astype(BF16)
    kvn = _rms_norm(p[:, Q_LORA:KR_OFF], kvg_ref[...]).astype(BF16)
    kv = jnp.dot(kvn, wkv_ref[...], preferred_element_type=F32)
    kr = _rope(p[:, KR_OFF:HQ_OFF], c, s1, s2, 1)
    k_out[...] = (kv[:, :MLA_HEADS * HEAD_PAD] + jnp.concatenate([kr] * MLA_HEADS, axis=1)).astype(BF16)
    lane = lax.broadcasted_iota(I32, (1, MLA_HEADS * HEAD_PAD), 1)
    ones_lane = (lane % HEAD_PAD == MLA_V).astype(F32)
    v_out[...] = (kv[:, MLA_HEADS * HEAD_PAD:] + ones_lane).astype(BF16)

    lbl = lbl_ref[...]
    le = jnp.exp(lbl - jnp.max(lbl, axis=0, keepdims=True))
    lb = le[0:1] / jnp.sum(le, axis=0, keepdims=True)
    hq = p[:, HQ_OFF:HQ_OFF + HG_W]
    hf = p[:, HQ_OFF + HG_W:HQ_OFF + 2 * HG_W]
    hi = p[:, HQ_OFF + 2 * HG_W:HQ_OFF + 3 * HG_W]
    hg = p[:, HQ_OFF + 3 * HG_W:HQ_OFF + 4 * HG_W]
    hq_out[...] = (hq * _sigmoid_pair(hq)[0]).astype(BF16)
    sg, one_minus_sg = _sigmoid_pair(hf)
    hk_out[...] = ((1.0 - lb) * one_minus_sg).astype(BF16)
    hg_out[...] = jnp.log(lb + (1.0 - lb) * sg)
    hv_out[...] = hi.astype(BF16)
    gate_out[...] = (hg * _sigmoid_pair(hg)[0]).astype(BF16)


def _in_proj(xt, ln_g, ln_b, w_in_p, qa_g, w_qb_p, kv_g, w_kv_p, lb_logits, c_tab, s1_tab, s2_tab):
    T = xt.shape[0]
    row = lambda w: pl.BlockSpec((TM_IN, w), lambda i: (i, 0))
    full = lambda a: pl.BlockSpec(a.shape, lambda i: (0, 0))
    outs = [(MLA_HEADS * HEAD_PAD, BF16), (MLA_HEADS * HEAD_PAD, BF16), (MLA_HEADS * HEAD_PAD, BF16),
            (HG_W, BF16), (HG_W, BF16), (HG_W, F32), (HG_W, BF16), (HG_W, BF16)]
    return pl.pallas_call(
        _in_proj_body,
        grid=(T // TM_IN,),
        in_specs=[row(D_MODEL), full(ln_g), full(ln_b), full(w_in_p), full(qa_g), full(w_qb_p),
                  full(kv_g), full(w_kv_p), full(lb_logits), row(HEAD_PAD), row(HEAD_PAD), row(HEAD_PAD)],
        out_specs=[row(w) for w, _ in outs],
        out_shape=[jax.ShapeDtypeStruct((T, w), d) for w, d in outs],
        compiler_params=_cparams(("parallel",)),
        name="in_proj",
    )(xt, ln_g, ln_b, w_in_p, qa_g, w_qb_p, kv_g, w_kv_p, lb_logits, c_tab, s1_tab, s2_tab)


def _attn_body(q_ref, k_ref, v_ref, o_ref):
    n_q = q_ref.shape[1] // TQ
    dn = (((1,), (1,)), ((), ()))
    row_c = lax.broadcasted_iota(I32, (TQ, TQ), 0) // CHUNK
    col_c = lax.broadcasted_iota(I32, (TQ, TQ), 1) // CHUNK
    diag_mask = col_c <= row_c
    heads = [slice(hh * HEAD_PAD, (hh + 1) * HEAD_PAD) for hh in range(2)]
    lane = lax.broadcasted_iota(I32, (TQ, 2 * MLA_V), 1)
    for qi in range(n_q):
        r0, kv = qi * TQ, (qi + 1) * TQ
        res = []
        for hs in heads:
            s = lax.dot_general(q_ref[0, r0:kv, hs], k_ref[0, :kv, hs], dn, preferred_element_type=F32)
            s_diag = jnp.where(diag_mask, s[:, r0:], -jnp.inf)
            m = jnp.max(s_diag, axis=-1, keepdims=True)
            if qi:
                m = jnp.maximum(m, jnp.max(s[:, :r0], axis=-1, keepdims=True))
                p = jnp.concatenate([jnp.exp2(s[:, :r0] - m), jnp.exp2(s_diag - m)], axis=1)
            else:
                p = jnp.exp2(s_diag - m)
            a = jnp.dot(p.astype(BF16), v_ref[0, :kv, hs], preferred_element_type=F32)
            res.append(a * (1.0 / a[:, MLA_V:MLA_V + 1]))
        o_ref[0, r0:kv, :] = jnp.where(lane < MLA_V, res[0], pltpu.roll(res[1], MLA_V, 1)).astype(BF16)


def _attention(q, k, v, B, S):
    w = MLA_HEADS * HEAD_PAD
    blk = pl.BlockSpec((1, S, 2 * HEAD_PAD), lambda b, h: (b, 0, h))
    out = pl.pallas_call(
        _attn_body,
        grid=(B, MLA_HEADS // 2),
        in_specs=[blk, blk, blk],
        out_specs=pl.BlockSpec((1, S, 2 * MLA_V), lambda b, h: (b, 0, h)),
        out_shape=jax.ShapeDtypeStruct((B, S, MLA_HEADS * MLA_V), BF16),
        compiler_params=_cparams(("parallel", "parallel")),
        name="mla_attention",
    )(q.reshape(B, S, w), k.reshape(B, S, w), v.reshape(B, S, w))
    return out.reshape(B * S, MLA_HEADS * MLA_V)


N_LEVELS = 6
Z_BLOCKS = N_LEVELS + 1


def _hgrn_constants():
    z = np.zeros((Z_BLOCKS, CHUNK, CHUNK), np.float32)
    masks = np.zeros((N_LEVELS, CHUNK, CHUNK), np.float32)
    idx = np.arange(CHUNK)
    for l in range(N_LEVELS):
        hs = CHUNK >> (l + 1)
        parent = idx // (2 * hs)
        right = (idx % (2 * hs)) >= hs
        mid = parent * 2 * hs + hs
        for t in range(CHUNK):
            z[l, t, :mid[t]] = 1.0
        masks[l] = (parent[:, None] == parent[None, :]) & right[:, None] & (~right)[None, :]
    z[N_LEVELS] = np.tril(np.ones((CHUNK, CHUNK), np.float32))
    z = z.reshape(Z_BLOCKS * CHUNK, CHUNK)
    return np.concatenate([z, z], axis=1), masks


def _hgrn_body(q_ref, k_ref, g_ref, v_ref, gate_ref, og_ref, z_ref, mask_ref, o_ref, state_ref):
    @pl.when(pl.program_id(1) == 0)
    def _():
        state_ref[...] = jnp.zeros_like(state_ref)

    dn_t = (((1,), (1,)), ((), ()))
    eye = (lax.broadcasted_iota(I32, (CHUNK, CHUNK), 0)
           == lax.broadcasted_iota(I32, (CHUNK, CHUNK), 1)).astype(F32)
    for c in range(HG_ROWS // CHUNK):
        rows = slice(c * CHUNK, (c + 1) * CHUNK)
        for h in range(HG_HEADS):
            cols = slice(h * HG_DK, (h + 1) * HG_DK)
            g2 = g_ref[0, rows, cols] * LOG2E
            g_hi = g2.astype(BF16)
            g_lo = (g2 - g_hi.astype(F32)).astype(BF16)
            sums = jnp.dot(z_ref[...], jnp.concatenate([g_hi, g_lo], axis=0),
                           preferred_element_type=F32)
            b = sums[N_LEVELS * CHUNK:]
            q = q_ref[0, rows, cols].astype(F32)
            k = k_ref[0, rows, cols].astype(F32)
            v = v_ref[0, rows, cols]
            a = eye * jnp.sum(q * k, axis=-1, keepdims=True)
            for l in range(N_LEVELS):
                d = jnp.exp2(-jnp.abs(b - sums[l * CHUNK:(l + 1) * CHUNK]))
                a = a + mask_ref[l] * lax.dot_general((q * d).astype(BF16), (k * d).astype(BF16), dn_t,
                                                      preferred_element_type=F32)
            o = jnp.dot(a.astype(BF16), v, preferred_element_type=F32)
            st = state_ref[h]
            d_in = jnp.exp2(b)
            o = o + lax.dot_general((q * d_in).astype(BF16), st.astype(BF16), dn_t,
                                    preferred_element_type=F32)
            kh = (k * jnp.exp2(b[CHUNK - 1:CHUNK] - b)).astype(BF16)
            upd = lax.dot_general(v, kh, (((0,), (0,)), ((), ())), preferred_element_type=F32)
            state_ref[h] = st * d_in[CHUNK - 1:CHUNK] + upd
            o = _rms_norm(o, og_ref[:, cols]) * gate_ref[0, rows, cols].astype(F32)
            o_ref[0, rows, cols] = o.astype(BF16)


def _hgrn(hq, hk, hg, hv, gate, out_g, B, S):
    zc, masks = _hgrn_constants()
    zc = jnp.asarray(zc, BF16)
    masks = jnp.asarray(masks, F32)
    blk = pl.BlockSpec((1, HG_ROWS, HG_W), lambda b, i: (b, i, 0))
    r3 = lambda a: a.reshape(B, S, HG_W)
    out = pl.pallas_call(
        _hgrn_body,
        grid=(B, S // HG_ROWS),
        in_specs=[blk, blk, blk, blk, blk,
                  pl.BlockSpec((1, HG_W), lambda b, i: (0, 0)),
                  pl.BlockSpec(zc.shape, lambda b, i: (0, 0)),
                  pl.BlockSpec(masks.shape, lambda b, i: (0, 0, 0))],
        out_specs=blk,
        out_shape=jax.ShapeDtypeStruct((B, S, HG_W), BF16),
        scratch_shapes=[pltpu.VMEM((HG_HEADS, HG_DV, HG_DK), F32)],
        compiler_params=_cparams(("parallel", "arbitrary")),
        name="hgrn2",
    )(r3(hq), r3(hk), r3(hg), r3(hv), r3(gate), out_g, zc, masks)
    return out.reshape(B * S, HG_W)


def _pack_bf16_pair(lo, hi):
    lo_b = pltpu.bitcast(lo.astype(BF16).astype(F32), I32)
    hi_b = pltpu.bitcast(hi.astype(BF16).astype(F32), I32)
    return hi_b | lax.shift_right_logical(lo_b, 16)


def _unpack_bf16_pair(w):
    lo = pltpu.bitcast(lax.shift_left(w, 16), F32)
    hi = pltpu.bitcast(w & jnp.int32(-65536), F32)
    return lo, hi


def _out_proj_body(x_ref, lng_ref, lnb_ref, attn_ref, hgo_ref, mg_ref, wo_ref, l1g_ref, l1b_ref,
                   wr_ref, br_ref,
                   h1_out, h1b_out, pos_out, gate_out, cnt_out):
    h = _layer_norm(x_ref[...], lng_ref[...], lnb_ref[...])
    attn = _rms_norm(attn_ref[...].astype(F32), mg_ref[...]).astype(BF16)
    mix_in = jnp.concatenate([attn, hgo_ref[...]], axis=1)
    mix = jnp.dot(mix_in, wo_ref[...], preferred_element_type=F32)
    h1 = _layer_norm(DEEPNORM_ALPHA * h + mix, l1g_ref[...], l1b_ref[...])
    h1_out[...] = h1
    h1b_out[...] = h1.astype(BF16)

    logits_t = lax.dot_general(wr_ref[...], h1, (((1,), (1,)), ((), ())), preferred_element_type=F32,
                               precision=lax.Precision.HIGHEST) + br_ref[...]
    tm = h1.shape[0]
    sub = lax.broadcasted_iota(I32, (N_EXPERTS, tm), 0)
    work = logits_t
    vals, idxs = [], []
    for _ in range(TOP_K):
        mx = jnp.max(work, axis=0, keepdims=True)
        ix = jnp.min(jnp.where(work == mx, sub, N_EXPERTS), axis=0, keepdims=True)
        vals.append(mx)
        idxs.append(ix)
        work = jnp.where(sub == ix, -jnp.inf, work)
    exps = [jnp.exp(vl - vals[0]) for vl in vals]
    den = exps[0] + exps[1] + exps[2] + exps[3]

    sel = jnp.zeros((N_EXPERTS, tm), F32)
    for ix in idxs:
        sel = sel + (sub == ix).astype(F32)
    selb = sel.astype(BF16)
    earlier = (lax.broadcasted_iota(I32, (tm, tm), 0)
               < lax.broadcasted_iota(I32, (tm, tm), 1)).astype(BF16)
    lower = (lax.broadcasted_iota(I32, (N_EXPERTS, N_EXPERTS), 1)
             < lax.broadcasted_iota(I32, (N_EXPERTS, N_EXPERTS), 0)).astype(BF16)
    cnt = jnp.sum(sel, axis=1, keepdims=True)
    run = jnp.floor((cnt + (SUBLANES - 1)) * (1.0 / SUBLANES)) * SUBLANES
    run_b = jnp.broadcast_to(run, (N_EXPERTS, LANES)).astype(BF16)
    seg_off = jnp.dot(lower, run_b, preferred_element_type=F32)[:, 0:1]
    posmat = jnp.dot(selb, earlier, preferred_element_type=F32) + seg_off
    pos = [jnp.sum(jnp.where(sub == ix, posmat, 0.0), axis=0, keepdims=True) for ix in idxs]
    pos_out[...] = jnp.concatenate(pos, axis=0).astype(I32)
    gate_out[...] = jnp.concatenate([e / den for e in exps], axis=0)
    cnt_out[0] = cnt


def _out_proj(xt, ln_g, ln_b, attn, hgo, mla_g, w_o, l1g, l1b, w_r, b_r):
    T = xt.shape[0]
    row = lambda w: pl.BlockSpec((TM_OUT, w), lambda i: (i, 0))
    full = lambda a: pl.BlockSpec(a.shape, lambda i: (0, 0))
    col = pl.BlockSpec((TOP_K, TM_OUT), lambda i: (0, i))
    nt = T // TM_OUT
    return pl.pallas_call(
        _out_proj_body,
        grid=(nt,),
        in_specs=[row(D_MODEL), full(ln_g), full(ln_b), row(MLA_HEADS * MLA_V), row(HG_W), full(mla_g),
                  full(w_o), full(l1g), full(l1b), full(w_r), full(b_r)],
        out_specs=[row(D_MODEL), row(D_MODEL), col, col,
                   pl.BlockSpec((1, N_EXPERTS, 1), lambda i: (i, 0, 0))],
        out_shape=[jax.ShapeDtypeStruct((T, D_MODEL), F32), jax.ShapeDtypeStruct((T, D_MODEL), BF16),
                   jax.ShapeDtypeStruct((TOP_K, T), I32), jax.ShapeDtypeStruct((TOP_K, T), F32),
                   jax.ShapeDtypeStruct((nt, N_EXPERTS, 1), F32)],
        compiler_params=_cparams(("parallel",)),
        name="out_proj_router",
    )(xt, ln_g, ln_b, attn, hgo, mla_g, w_o, l1g, l1b, w_r, b_r)


TILE_SLOTS = TOP_K * TM_DISP + N_EXPERTS * SUBLANES
SEG_BITS = TM_DISP.bit_length()


def _ceil_to(n, m):
    return (n + m - 1) // m * m


def _rows(start, size):
    return pl.ds(pl.multiple_of(start, SUBLANES), size)


def _for_each_piece(n, fn):
    for kbit in reversed(range(SUBLANES.bit_length() - 1, SEG_BITS)):
        size = 1 << kbit

        @pl.when((n & size) != 0)
        def _():
            fn(pl.multiple_of((n >> (kbit + 1)) << (kbit + 1), SUBLANES), size)


def _pack_exact_pair(lo, hi):
    return pltpu.bitcast(hi, I32) | lax.shift_right_logical(pltpu.bitcast(lo, I32), 16)


def _zero_tail_blocks(zeros_ref, out_hbm, n_used, zsem):
    def blk_copy(g):
        return pltpu.make_async_copy(zeros_ref, out_hbm.at[_rows(g * ROW_BLOCK, ROW_BLOCK)], zsem)

    n_total = out_hbm.shape[0] // ROW_BLOCK
    lax.fori_loop(n_used, n_total, lambda g, c: (blk_copy(g).start(), c)[1], 0)
    lax.fori_loop(n_used, n_total, lambda g, c: (blk_copy(g).wait(), c)[1], 0)


def _dispatch_body(segn_ref, segsrc_ref, segdst_ref, padn_ref, padd_ref, nused_ref, pos_ref, xb_ref, xs_out,
                   stage, zeros_ref, sem, zsem):
    i = pl.program_id(0)

    @pl.when(i == 0)
    def _():
        zeros_ref[...] = jnp.zeros_like(zeros_ref)

        def pad_copy(e, off, size):
            return pltpu.make_async_copy(zeros_ref.at[pl.ds(0, size)],
                                         xs_out.at[_rows(padd_ref[e] + off, size)], zsem)

        def fill(e, c):
            _for_each_piece(padn_ref[e], lambda off, size: pad_copy(e, off, size).start())
            return c

        def drain(e, c):
            _for_each_piece(padn_ref[e], lambda off, size: pad_copy(e, off, size).wait())
            return c

        lax.fori_loop(0, N_EXPERTS, fill, 0)
        lax.fori_loop(0, N_EXPERTS, drain, 0)
        _zero_tail_blocks(zeros_ref, xs_out, nused_ref[0], zsem)

    pos = pos_ref[...]
    slot = lax.broadcasted_iota(I32, (TILE_SLOTS, TM_DISP), 0)
    sel = jnp.zeros((TILE_SLOTS, TM_DISP), F32)
    for j in range(TOP_K):
        sel = sel + (slot == pos[j:j + 1, :]).astype(F32)
    rows = jnp.dot(sel.astype(BF16), xb_ref[...], preferred_element_type=F32)
    stage[...] = _pack_exact_pair(rows[:, :D_MODEL // 2], rows[:, D_MODEL // 2:])

    def seg_copy(e, off, size):
        k = i * N_EXPERTS + e
        return pltpu.make_async_copy(stage.at[_rows(segsrc_ref[k] + off, size)],
                                     xs_out.at[_rows(segdst_ref[k] + off, size)], sem)

    def send(e, c):
        _for_each_piece(segn_ref[i * N_EXPERTS + e], lambda off, size: seg_copy(e, off, size).start())
        return c

    def settle(e, c):
        _for_each_piece(segn_ref[i * N_EXPERTS + e], lambda off, size: seg_copy(e, off, size).wait())
        return c

    lax.fori_loop(0, N_EXPERTS, send, 0)
    lax.fori_loop(0, N_EXPERTS, settle, 0)


def _dispatch(seg_n, seg_src, seg_dst, pad_n, pad_dst, n_used, pos_t, h1b, n_rows):
    T = h1b.shape[0]
    grid_spec = pltpu.PrefetchScalarGridSpec(
        num_scalar_prefetch=6,
        grid=(T // TM_DISP,),
        in_specs=[pl.BlockSpec((TOP_K, TM_DISP), lambda i, *_: (0, i)),
                  pl.BlockSpec((TM_DISP, D_MODEL), lambda i, *_: (i, 0))],
        out_specs=pl.BlockSpec(memory_space=pl.ANY),
        scratch_shapes=[pltpu.VMEM((TILE_SLOTS, D_MODEL // 2), I32),
                        pltpu.VMEM((ROW_BLOCK, D_MODEL // 2), I32),
                        pltpu.SemaphoreType.DMA(()), pltpu.SemaphoreType.DMA(())],
    )
    return pl.pallas_call(
        _dispatch_body,
        grid_spec=grid_spec,
        out_shape=jax.ShapeDtypeStruct((n_rows, D_MODEL // 2), I32),
        compiler_params=_cparams(("arbitrary",)),
        name="moe_dispatch",
    )(seg_n, seg_src, seg_dst, pad_n, pad_dst, n_used, pos_t, h1b)


def _experts_body(bstart_ref, bcount_ref, nused_ref, xs_hbm, wu_ref, bu_ref, wd_ref, bd_ref, ys_hbm,
                  wu_bf, wd_bf, xbuf, ybuf, sem_in, sem_out):
    e = pl.program_id(0)
    n_used = nused_ref[0]
    nb = bcount_ref[e]

    def fetch(g, slot):
        rows = pl.ds(pl.multiple_of(g * ROW_BLOCK, ROW_BLOCK), ROW_BLOCK)
        return pltpu.make_async_copy(xs_hbm.at[rows], xbuf.at[slot], sem_in.at[slot])

    def put(g, slot):
        rows = pl.ds(pl.multiple_of(g * ROW_BLOCK, ROW_BLOCK), ROW_BLOCK)
        return pltpu.make_async_copy(ybuf.at[slot], ys_hbm.at[rows], sem_out.at[slot])

    @pl.when(jnp.logical_and(e == 0, n_used > 0))
    def _():
        fetch(0, 0).start()

    @pl.when(nb > 0)
    def _():
        wu_bf[...] = wu_ref[0].astype(BF16)
        wd_bf[...] = wd_ref[0].astype(BF16)

    def block(b, c):
        g = bstart_ref[e] + b
        slot = lax.rem(g, 2)
        fetch(g, slot).wait()

        @pl.when(g + 1 < n_used)
        def _():
            fetch(g + 1, 1 - slot).start()

        lo, hi = _unpack_bf16_pair(xbuf[slot])
        x = jnp.concatenate([lo, hi], axis=1).astype(BF16)
        hb = jnp.dot(x, wu_bf[...], preferred_element_type=F32) + bu_ref[0]
        glu = jnp.minimum(hb[:, :D_EXPERT], SWIGLU_LIMIT)
        lin = jnp.clip(hb[:, D_EXPERT:], -SWIGLU_LIMIT, SWIGLU_LIMIT)
        act = glu * (1.0 / (1.0 + jnp.exp(-SWIGLU_ALPHA * glu))) * (lin + 1.0)
        y = jnp.dot(act.astype(BF16), wd_bf[...], preferred_element_type=F32) + bd_ref[0]

        @pl.when(g >= 2)
        def _():
            put(g - 2, slot).wait()

        ybuf[slot] = _pack_bf16_pair(y[:, :D_MODEL // 2], y[:, D_MODEL // 2:])
        put(g, slot).start()
        return c

    lax.fori_loop(0, nb, block, 0)

    @pl.when(e == N_EXPERTS - 1)
    def _():
        for back in (2, 1):
            @pl.when(n_used >= back)
            def _():
                put(n_used - back, lax.rem(n_used - back, 2)).wait()
        ybuf[0] = jnp.zeros_like(ybuf[0])
        _zero_tail_blocks(ybuf.at[0], ys_hbm, n_used, sem_out.at[0])


def _experts(block_start, block_count, n_used, xs, w_up, b_up, w_down, b_down):
    n_rows = xs.shape[0]
    grid_spec = pltpu.PrefetchScalarGridSpec(
        num_scalar_prefetch=3,
        grid=(N_EXPERTS,),
        in_specs=[pl.BlockSpec(memory_space=pl.ANY),
                  pl.BlockSpec((1, D_MODEL, 2 * D_EXPERT), lambda e, *_: (e, 0, 0)),
                  pl.BlockSpec((1, 1, 2 * D_EXPERT), lambda e, *_: (e, 0, 0)),
                  pl.BlockSpec((1, D_EXPERT, D_MODEL), lambda e, *_: (e, 0, 0)),
                  pl.BlockSpec((1, 1, D_MODEL), lambda e, *_: (e, 0, 0))],
        out_specs=pl.BlockSpec(memory_space=pl.ANY),
        scratch_shapes=[pltpu.VMEM((D_MODEL, 2 * D_EXPERT), BF16), pltpu.VMEM((D_EXPERT, D_MODEL), BF16),
                        pltpu.VMEM((2, ROW_BLOCK, D_MODEL // 2), I32),
                        pltpu.VMEM((2, ROW_BLOCK, D_MODEL // 2), I32),
                        pltpu.SemaphoreType.DMA((2,)), pltpu.SemaphoreType.DMA((2,))],
    )
    return pl.pallas_call(
        _experts_body,
        grid_spec=grid_spec,
        out_shape=jax.ShapeDtypeStruct((n_rows, D_MODEL // 2), I32),
        compiler_params=_cparams(("arbitrary",)),
        name="moe_experts",
    )(block_start, block_count, n_used, xs, w_up, b_up.reshape(N_EXPERTS, 1, 2 * D_EXPERT), w_down,
      b_down.reshape(N_EXPERTS, 1, D_MODEL))


def _combine_body(segn_ref, segsrc_ref, segdst_ref, pos_ref, gate_ref, h1_ref, l2g_ref, l2b_ref, ys_hbm,
                  o_ref, stage, sem):
    i = pl.program_id(0)

    @pl.when(i == 0)
    def _():
        stage[...] = jnp.zeros_like(stage)

    def seg_copy(e, off, size):
        k = i * N_EXPERTS + e
        return pltpu.make_async_copy(ys_hbm.at[_rows(segdst_ref[k] + off, size)],
                                     stage.at[_rows(segsrc_ref[k] + off, size)], sem)

    def fetch(e, c):
        _for_each_piece(segn_ref[i * N_EXPERTS + e], lambda off, size: seg_copy(e, off, size).start())
        return c

    def settle(e, c):
        _for_each_piece(segn_ref[i * N_EXPERTS + e], lambda off, size: seg_copy(e, off, size).wait())
        return c

    lax.fori_loop(0, N_EXPERTS, fetch, 0)
    pos = pos_ref[...]
    gate = gate_ref[...]
    slot = lax.broadcasted_iota(I32, (TM_DISP, TILE_SLOTS), 1)
    w = jnp.zeros((TM_DISP, TILE_SLOTS), F32)
    for j in range(TOP_K):
        w = w + jnp.where(slot == pos[:, j:j + 1], gate[:, j:j + 1], 0.0)
    w_hi = w.astype(BF16)
    w_lo = (w - w_hi.astype(F32)).astype(BF16)
    lax.fori_loop(0, N_EXPERTS, settle, 0)
    lo, hi = _unpack_bf16_pair(stage[...])
    y = jnp.concatenate([lo, hi], axis=1).astype(BF16)
    ffn = (jnp.dot(w_hi, y, preferred_element_type=F32) + jnp.dot(w_lo, y, preferred_element_type=F32))
    o_ref[...] = _layer_norm(DEEPNORM_ALPHA * h1_ref[...] + ffn, l2g_ref[...], l2b_ref[...])


def _combine(seg_n, seg_src, seg_dst, pos, gate, h1, l2g, l2b, ys):
    T = h1.shape[0]
    grid_spec = pltpu.PrefetchScalarGridSpec(
        num_scalar_prefetch=3,
        grid=(T // TM_DISP,),
        in_specs=[pl.BlockSpec((TM_DISP, TOP_K), lambda i, *_: (i, 0)),
                  pl.BlockSpec((TM_DISP, TOP_K), lambda i, *_: (i, 0)),
                  pl.BlockSpec((TM_DISP, D_MODEL), lambda i, *_: (i, 0)),
                  pl.BlockSpec(l2g.shape, lambda i, *_: (0, 0)),
                  pl.BlockSpec(l2b.shape, lambda i, *_: (0, 0)),
                  pl.BlockSpec(memory_space=pl.ANY)],
        out_specs=pl.BlockSpec((TM_DISP, D_MODEL), lambda i, *_: (i, 0)),
        scratch_shapes=[pltpu.VMEM((TILE_SLOTS, D_MODEL // 2), I32), pltpu.SemaphoreType.DMA(())],
    )
    return pl.pallas_call(
        _combine_body,
        grid_spec=grid_spec,
        out_shape=jax.ShapeDtypeStruct((T, D_MODEL), F32),
        compiler_params=_cparams(("arbitrary",)),
        name="moe_combine",
    )(seg_n, seg_src, seg_dst, pos, gate, h1, l2g, l2b, ys)


def _prep_weights(w_in, w_q_b, w_kv_b):
    zc = lambda n: jnp.zeros((D_MODEL, n), F32)
    w_in_p = jnp.concatenate(
        [w_in[:, :KR_OFF], zc(MLA_NOPE), w_in[:, KR_OFF:KR_OFF + MLA_ROPE],
         zc(HEAD_PAD - MLA_NOPE - MLA_ROPE), w_in[:, KR_OFF + MLA_ROPE:]], axis=1).astype(BF16)
    qd = MLA_NOPE + MLA_ROPE
    wq = w_q_b.reshape(Q_LORA, MLA_HEADS, qd)
    wq = jnp.pad(wq, ((0, 0), (0, 0), (0, HEAD_PAD - qd))).reshape(Q_LORA, MLA_HEADS * HEAD_PAD)
    wkv = w_kv_b.reshape(KV_LORA, MLA_HEADS, MLA_NOPE + MLA_V)
    wk = jnp.pad(wkv[:, :, :MLA_NOPE], ((0, 0), (0, 0), (0, HEAD_PAD - MLA_NOPE)))
    wk = wk.reshape(KV_LORA, MLA_HEADS * HEAD_PAD)
    wv = jnp.pad(wkv[:, :, MLA_NOPE:], ((0, 0), (0, 0), (0, HEAD_PAD - MLA_V)))
    wv = wv.reshape(KV_LORA, MLA_HEADS * HEAD_PAD)
    return w_in_p, wq.astype(BF16), jnp.concatenate([wk, wv], axis=1).astype(BF16)


def kernel(x, positions, ln_in_g, ln_in_b, w_in, q_a_norm_g, w_q_b, kv_a_norm_g, w_kv_b, hgrn_lb_logits,
           mla_out_g, hgrn_out_g, w_o, ln1_g, ln1_b, w_router, b_router, w_up, b_up, w_down, b_down,
           ln2_g, ln2_b):
    B, S, D = x.shape
    assert D == D_MODEL and w_in.shape[0] == DEPTH and hgrn_lb_logits.shape[0] == DEPTH + 1
    T = B * S
    xt = x.reshape(T, D)
    r2 = lambda a: a.reshape(1, -1)

    c_tab, s1_tab, s2_tab = _rope_tables(positions)
    w_in_p, w_qb_p, w_kv_p = _prep_weights(w_in[0], w_q_b[0], w_kv_b[0])
    q, k, v, hq, hk, hg, hv, gate = _in_proj(
        xt, r2(ln_in_g), r2(ln_in_b), w_in_p, q_a_norm_g, w_qb_p, kv_a_norm_g, w_kv_p,
        hgrn_lb_logits, c_tab, s1_tab, s2_tab)
    attn = _attention(q, k, v, B, S)
    hgo = _hgrn(hq, hk, hg, hv, gate, hgrn_out_g, B, S)
    h1, h1b, pos_t, gate_t, tile_cnt = _out_proj(
        xt, r2(ln_in_g), r2(ln_in_b), attn, hgo, mla_out_g, w_o[0].astype(BF16), ln1_g, ln1_b,
        w_router[0].T, b_router.reshape(N_EXPERTS, 1))

    n_tiles = T // TM_DISP
    n_rows = T * TOP_K + n_tiles * N_EXPERTS * (SUBLANES - 1) + N_EXPERTS * ROW_BLOCK
    n_rows = _ceil_to(n_rows, ROW_BLOCK)
    tc = _ceil_to(tile_cnt[:, :, 0].astype(I32), SUBLANES)
    cnt = jnp.sum(tc, axis=0)
    padded = _ceil_to(cnt, ROW_BLOCK)
    pad_start = jnp.cumsum(padded) - padded
    seg_n = tc.reshape(-1)
    seg_src = (jnp.cumsum(tc, axis=1) - tc).reshape(-1)
    seg_dst = (pad_start[None, :] + jnp.cumsum(tc, axis=0) - tc).reshape(-1)
    n_used = jnp.sum(padded, keepdims=True) // ROW_BLOCK

    xs = _dispatch(seg_n, seg_src, seg_dst, padded - cnt, pad_start + cnt, n_used, pos_t, h1b, n_rows)
    ys = _experts(pad_start // ROW_BLOCK, padded // ROW_BLOCK, n_used, xs,
                  w_up[0], b_up[0], w_down[0], b_down[0])
    out = _combine(seg_n, seg_src, seg_dst, pos_t.T, gate_t.T, h1, ln2_g, ln2_b, ys)
    return out.reshape(B, S, D)
```

```python
import numpy as np
import jax
import jax.numpy as jnp
from jax import lax
from jax.experimental import pallas as pl
from jax.experimental.pallas import tpu as pltpu

F32 = jnp.float32
BF16 = jnp.bfloat16
I32 = jnp.int32

D_MODEL = 1024
CHUNK = 64
MLA_HEADS = 8
MLA_NOPE = 64
MLA_ROPE = 32
MLA_V = 64
Q_LORA = 384
KV_LORA = 256
ROPE_THETA = 10000.0
HG_HEADS = 4
HG_DK = 128
HG_DV = 128
HG_W = HG_HEADS * HG_DK
N_EXPERTS = 32
TOP_K = 4
D_EXPERT = 1024
SWIGLU_LIMIT = 7.0
SWIGLU_ALPHA = 1.702
LN_EPS = 1e-5
RMS_EPS = 1e-6
DEPTH = 1
DEEPNORM_ALPHA = (2.0 * DEPTH) ** 0.25
LOG2E = 1.4426950408889634

LANES = 128
SUBLANES = 8
HEAD_PAD = 128
HALF_ROPE = MLA_ROPE // 2
KR_OFF = Q_LORA + KV_LORA
HQ_OFF = KR_OFF + HEAD_PAD
IN_COLS_PAD = HQ_OFF + 4 * HG_W

TM_IN = 512
TQ = 256
HG_ROWS = 256
TM_OUT = 512
TM_DISP = 256
ROW_BLOCK = 512
VMEM_LIMIT = 52 * 1024 * 1024


def _cparams(sem, vmem=VMEM_LIMIT):
    return pltpu.CompilerParams(dimension_semantics=sem, vmem_limit_bytes=vmem)


def _layer_norm(x, g, b):
    mu = jnp.mean(x, axis=-1, keepdims=True)
    xc = x - mu
    var = jnp.mean(xc * xc, axis=-1, keepdims=True)
    return xc * lax.rsqrt(var + LN_EPS) * g + b


def _rms_norm(x, g):
    return x * lax.rsqrt(jnp.mean(x * x, axis=-1, keepdims=True) + RMS_EPS) * g


def _split_bf16(a):
    hi = a.astype(BF16)
    return hi, (a - hi.astype(F32)).astype(BF16)


def _sigmoid_pair(x):
    e = jnp.exp(-jnp.abs(x))
    big = 1.0 / (1.0 + e)
    small = e * big
    pos = x >= 0
    return jnp.where(pos, big, small), jnp.where(pos, small, big)


def _cos_sin_body(ang_ref, cos_ref, sin_ref):
    a = ang_ref[...]
    cos_ref[...] = jnp.cos(a)
    sin_ref[...] = jnp.sin(a)


def _rope_tables(positions):
    T = positions.size
    inv_freq = ROPE_THETA ** (-jnp.arange(0, MLA_ROPE, 2, dtype=F32) / MLA_ROPE)
    ang = (positions.astype(F32)[..., None] * inv_freq).reshape(T * HALF_ROPE // LANES, LANES)
    cos, sin = pl.pallas_call(
        _cos_sin_body,
        out_shape=(jax.ShapeDtypeStruct(ang.shape, F32),) * 2,
        name="rope_cos_sin",
    )(ang)
    cos = cos.reshape(T, HALF_ROPE)
    sin = sin.reshape(T, HALF_ROPE)
    ones = jnp.ones((T, MLA_NOPE), F32)
    z_n = jnp.zeros((T, MLA_NOPE), F32)
    z_h = jnp.zeros((T, HALF_ROPE), F32)
    z_p = jnp.zeros((T, HEAD_PAD - MLA_NOPE - MLA_ROPE), F32)
    c_tab = jnp.concatenate([ones, cos, cos, z_p], axis=1)
    s1_tab = jnp.concatenate([z_n, z_h, sin, z_p], axis=1)
    s2_tab = jnp.concatenate([z_n, -sin, z_h, z_p], axis=1)
    return c_tab, s1_tab, s2_tab


def _rope(x, c, s1, s2, reps):
    width = reps * HEAD_PAD
    if reps > 1:
        c = jnp.concatenate([c] * reps, axis=1)
        s1 = jnp.concatenate([s1] * reps, axis=1)
        s2 = jnp.concatenate([s2] * reps, axis=1)
    return (x * c + pltpu.roll(x, HALF_ROPE, 1) * s1
            + pltpu.roll(x, width - HALF_ROPE, 1) * s2)


def _in_proj_body(x_ref, lng_ref, lnb_ref, win_ref, qag_ref, wqb_ref, kvg_ref, wkv_ref, lbl_ref,
                  c_ref, s1_ref, s2_ref,
                  q_out, k_out, v_out, hq_out, hk_out, hg_out, hv_out, gate_out):
    h = _layer_norm(x_ref[...], lng_ref[...], lnb_ref[...])
    p = jnp.dot(h.astype(BF16), win_ref[...], preferred_element_type=F32)
    c, s1, s2 = c_ref[...], s1_ref[...], s2_ref[...]

    qn = _rms_norm(p[:, :Q_LORA], qag_ref[...]).astype(BF16)
    q = jnp.dot(qn, wqb_ref[...], preferred_element_type=F32)
    scale = (MLA_NOPE + MLA_ROPE) ** -0.5 * LOG2E
    q_out[...] = (_rope(q, c, s1, s2, MLA_HEADS) * scale).astype(BF16)
    kvn = _rms_norm(p[:, Q_LORA:KR_OFF], kvg_ref[...]).astype(BF16)
    kv = jnp.dot(kvn, wkv_ref[...], preferred_element_type=F32)
    kr = _rope(p[:, KR_OFF:HQ_OFF], c, s1, s2, 1)
    k_out[...] = (kv[:, :MLA_HEADS * HEAD_PAD] + jnp.concatenate([kr] * MLA_HEADS, axis=1)).astype(BF16)
    lane = lax.broadcasted_iota(I32, (1, MLA_HEADS * HEAD_PAD), 1)
    ones_lane = (lane % HEAD_PAD == MLA_V).astype(F32)
    v_out[...] = (kv[:, MLA_HEADS * HEAD_PAD:] + ones_lane).astype(BF16)

    lbl = lbl_ref[...]
    le = jnp.exp(lbl - jnp.max(lbl, axis=0, keepdims=True))
    lb = le[0:1] / jnp.sum(le, axis=0, keepdims=True)
    hq = p[:, HQ_OFF:HQ_OFF + HG_W]
    hf = p[:, HQ_OFF + HG_W:HQ_OFF + 2 * HG_W]
    hi = p[:, HQ_OFF + 2 * HG_W:HQ_OFF + 3 * HG_W]
    hg = p[:, HQ_OFF + 3 * HG_W:HQ_OFF + 4 * HG_W]
    hq_out[...] = (hq * _sigmoid_pair(hq)[0]).astype(BF16)
    sg, one_minus_sg = _sigmoid_pair(hf)
    hk_out[...] = ((1.0 - lb) * one_minus_sg).astype(BF16)
    hg_out[...] = jnp.log(lb + (1.0 - lb) * sg)
    hv_out[...] = hi.astype(BF16)
    gate_out[...] = (hg * _sigmoid_pair(hg)[0]).astype(BF16)


def _in_proj(xt, ln_g, ln_b, w_in_p, qa_g, w_qb_p, kv_g, w_kv_p, lb_logits, c_tab, s1_tab, s2_tab):
    T = xt.shape[0]
    row = lambda w: pl.BlockSpec((TM_IN, w), lambda i: (i, 0))
    full = lambda a: pl.BlockSpec(a.shape, lambda i: (0, 0))
    outs = [(MLA_HEADS * HEAD_PAD, BF16), (MLA_HEADS * HEAD_PAD, BF16), (MLA_HEADS * HEAD_PAD, BF16),
            (HG_W, BF16), (HG_W, BF16), (HG_W, F32), (HG_W, BF16), (HG_W, BF16)]
    return pl.pallas_call(
        _in_proj_body,
        grid=(T // TM_IN,),
        in_specs=[row(D_MODEL), full(ln_g), full(ln_b), full(w_in_p), full(qa_g), full(w_qb_p),
                  full(kv_g), full(w_kv_p), full(lb_logits), row(HEAD_PAD), row(HEAD_PAD), row(HEAD_PAD)],
        out_specs=[row(w) for w, _ in outs],
        out_shape=[jax.ShapeDtypeStruct((T, w), d) for w, d in outs],
        compiler_params=_cparams(("parallel",)),
        name="in_proj",
    )(xt, ln_g, ln_b, w_in_p, qa_g, w_qb_p, kv_g, w_kv_p, lb_logits, c_tab, s1_tab, s2_tab)


def _attn_body(q_ref, k_ref, v_ref, o_ref):
    n_q = q_ref.shape[1] // TQ
    dn = (((1,), (1,)), ((), ()))
    row_c = lax.broadcasted_iota(I32, (TQ, TQ), 0) // CHUNK
    col_c = lax.broadcasted_iota(I32, (TQ, TQ), 1) // CHUNK
    diag_mask = col_c <= row_c
    heads = [slice(hh * HEAD_PAD, (hh + 1) * HEAD_PAD) for hh in range(2)]
    lane = lax.broadcasted_iota(I32, (TQ, 2 * MLA_V), 1)
    for qi in range(n_q):
        r0, kv = qi * TQ, (qi + 1) * TQ
        res = []
        for hs in heads:
            s = lax.dot_general(q_ref[0, r0:kv, hs], k_ref[0, :kv, hs], dn, preferred_element_type=F32)
            s_diag = jnp.where(diag_mask, s[:, r0:], -jnp.inf)
            m = jnp.max(s_diag, axis=-1, keepdims=True)
            if qi:
                m = jnp.maximum(m, jnp.max(s[:, :r0], axis=-1, keepdims=True))
                p = jnp.concatenate([jnp.exp2(s[:, :r0] - m), jnp.exp2(s_diag - m)], axis=1)
            else:
                p = jnp.exp2(s_diag - m)
            a = jnp.dot(p.astype(BF16), v_ref[0, :kv, hs], preferred_element_type=F32)
            res.append(a * (1.0 / a[:, MLA_V:MLA_V + 1]))
        o_ref[0, r0:kv, :] = jnp.where(lane < MLA_V, res[0], pltpu.roll(res[1], MLA_V, 1)).astype(BF16)


def _attention(q, k, v, B, S):
    w = MLA_HEADS * HEAD_PAD
    blk = pl.BlockSpec((1, S, 2 * HEAD_PAD), lambda b, h: (b, 0, h))
    out = pl.pallas_call(
        _attn_body,
        grid=(B, MLA_HEADS // 2),
        in_specs=[blk, blk, blk],
        out_specs=pl.BlockSpec((1, S, 2 * MLA_V), lambda b, h: (b, 0, h)),
        out_shape=jax.ShapeDtypeStruct((B, S, MLA_HEADS * MLA_V), BF16),
        compiler_params=_cparams(("parallel", "parallel")),
        name="mla_attention",
    )(q.reshape(B, S, w), k.reshape(B, S, w), v.reshape(B, S, w))
    return out.reshape(B * S, MLA_HEADS * MLA_V)


N_LEVELS = 6
Z_BLOCKS = N_LEVELS + 1


def _hgrn_constants():
    z = np.zeros((Z_BLOCKS, CHUNK, CHUNK), np.float32)
    masks = np.zeros((N_LEVELS, CHUNK, CHUNK), np.float32)
    idx = np.arange(CHUNK)
    for l in range(N_LEVELS):
        hs = CHUNK >> (l + 1)
        parent = idx // (2 * hs)
        right = (idx % (2 * hs)) >= hs
        mid = parent * 2 * hs + hs
        for t in range(CHUNK):
            z[l, t, :mid[t]] = 1.0
        masks[l] = (parent[:, None] == parent[None, :]) & right[:, None] & (~right)[None, :]
    z[N_LEVELS] = np.tril(np.ones((CHUNK, CHUNK), np.float32))
    z = z.reshape(Z_BLOCKS * CHUNK, CHUNK)
    n_c = HG_ROWS // CHUNK
    step_masks = np.zeros((N_LEVELS, HG_ROWS, HG_ROWS), np.float32)
    for c in range(n_c):
        step_masks[:, c * CHUNK:(c + 1) * CHUNK, c * CHUNK:(c + 1) * CHUNK] = masks
    return np.concatenate([z, z], axis=1), step_masks


def _hgrn_body(q_ref, k_ref, g_ref, v_ref, gate_ref, og_ref, z_ref, mask_ref, o_ref, state_ref):
    @pl.when(pl.program_id(1) == 0)
    def _():
        state_ref[...] = jnp.zeros_like(state_ref)

    dn_t = (((1,), (1,)), ((), ()))
    n_c = HG_ROWS // CHUNK
    eye = (lax.broadcasted_iota(I32, (HG_ROWS, HG_ROWS), 0)
           == lax.broadcasted_iota(I32, (HG_ROWS, HG_ROWS), 1)).astype(F32)
    sums = []
    for c in range(n_c):
        g2 = g_ref[0, c * CHUNK:(c + 1) * CHUNK, :] * LOG2E
        g_hi, g_lo = _split_bf16(g2)
        sums.append(jnp.dot(z_ref[...], jnp.concatenate([g_hi, g_lo], axis=0),
                            preferred_element_type=F32))
    level = lambda l, cols: jnp.concatenate([s[l * CHUNK:(l + 1) * CHUNK, cols] for s in sums], axis=0)
    for h in range(HG_HEADS):
        cols = slice(h * HG_DK, (h + 1) * HG_DK)
        b = level(N_LEVELS, cols)
        q = q_ref[0, :, cols].astype(F32)
        k = k_ref[0, :, cols].astype(F32)
        v = v_ref[0, :, cols]
        a = eye * jnp.sum(q * k, axis=-1, keepdims=True)
        for l in range(N_LEVELS):
            d = jnp.exp2(-jnp.abs(b - level(l, cols)))
            a = a + mask_ref[l] * lax.dot_general((q * d).astype(BF16), (k * d).astype(BF16), dn_t,
                                                  preferred_element_type=F32)
        o_intra = jnp.dot(a.astype(BF16), v, preferred_element_type=F32)
        d_in = jnp.exp2(b)
        qd = (q * d_in).astype(BF16)
        st = state_ref[h]
        o_parts = []
        for c in range(n_c):
            rows = slice(c * CHUNK, (c + 1) * CHUNK)
            last = slice((c + 1) * CHUNK - 1, (c + 1) * CHUNK)
            o_parts.append(o_intra[rows] + lax.dot_general(qd[rows], st.astype(BF16), dn_t,
                                                           preferred_element_type=F32))
            kh = (k[rows] * jnp.exp2(b[last] - b[rows])).astype(BF16)
            upd = lax.dot_general(v[rows], kh, (((0,), (0,)), ((), ())), preferred_element_type=F32)
            st = st * d_in[last] + upd
        state_ref[h] = st
        o = jnp.concatenate(o_parts, axis=0)
        o = _rms_norm(o, og_ref[:, cols]) * gate_ref[0, :, cols].astype(F32)
        o_ref[0, :, cols] = o.astype(BF16)


def _hgrn(hq, hk, hg, hv, gate, out_g, B, S):
    zc, masks = _hgrn_constants()
    zc = jnp.asarray(zc, BF16)
    masks = jnp.asarray(masks, F32)
    blk = pl.BlockSpec((1, HG_ROWS, HG_W), lambda b, i: (b, i, 0))
    r3 = lambda a: a.reshape(B, S, HG_W)
    out = pl.pallas_call(
        _hgrn_body,
        grid=(B, S // HG_ROWS),
        in_specs=[blk, blk, blk, blk, blk,
                  pl.BlockSpec((1, HG_W), lambda b, i: (0, 0)),
                  pl.BlockSpec(zc.shape, lambda b, i: (0, 0)),
                  pl.BlockSpec(masks.shape, lambda b, i: (0, 0, 0))],
        out_specs=blk,
        out_shape=jax.ShapeDtypeStruct((B, S, HG_W), BF16),
        scratch_shapes=[pltpu.VMEM((HG_HEADS, HG_DV, HG_DK), F32)],
        compiler_params=_cparams(("parallel", "arbitrary")),
        name="hgrn2",
    )(r3(hq), r3(hk), r3(hg), r3(hv), r3(gate), out_g, zc, masks)
    return out.reshape(B * S, HG_W)


def _pack_bf16_pair(lo, hi):
    lo_b = pltpu.bitcast(lo.astype(BF16).astype(F32), I32)
    hi_b = pltpu.bitcast(hi.astype(BF16).astype(F32), I32)
    return hi_b | lax.shift_right_logical(lo_b, 16)


def _unpack_bf16_pair(w):
    lo = pltpu.bitcast(lax.shift_left(w, 16), F32)
    hi = pltpu.bitcast(w & jnp.int32(-65536), F32)
    return lo, hi


def _out_proj_body(x_ref, lng_ref, lnb_ref, attn_ref, hgo_ref, mg_ref, wo_ref, l1g_ref, l1b_ref,
                   wr_ref, br_ref,
                   h1_out, h1b_out, pos_out, gate_out, cnt_out):
    for sub_tile in range(TM_OUT // TM_DISP):
        rs = slice(sub_tile * TM_DISP, (sub_tile + 1) * TM_DISP)
        _out_proj_tile(x_ref[rs, :], lng_ref, lnb_ref, attn_ref[rs, :], hgo_ref[rs, :], mg_ref, wo_ref,
                       l1g_ref, l1b_ref, wr_ref, br_ref,
                       h1_out.at[rs, :], h1b_out.at[rs, :], pos_out.at[:, rs], gate_out.at[:, rs],
                       cnt_out.at[sub_tile])


def _out_proj_tile(x, lng_ref, lnb_ref, attn, hgo, mg_ref, wo_ref, l1g_ref, l1b_ref, wr_ref, br_ref,
                   h1_out, h1b_out, pos_out, gate_out, cnt_out):
    h = _layer_norm(x, lng_ref[...], lnb_ref[...])
    attn = _rms_norm(attn.astype(F32), mg_ref[...]).astype(BF16)
    mix_in = jnp.concatenate([attn, hgo], axis=1)
    mix = jnp.dot(mix_in, wo_ref[...], preferred_element_type=F32)
    h1 = _layer_norm(DEEPNORM_ALPHA * h + mix, l1g_ref[...], l1b_ref[...])
    h1_out[...] = h1
    h1_hi, h1_lo = _split_bf16(h1)
    h1b_out[...] = h1_hi

    w_hi, w_lo = _split_bf16(wr_ref[...])
    nt_dims = (((1,), (1,)), ((), ()))
    logits_t = (lax.dot_general(w_hi, h1_hi, nt_dims, preferred_element_type=F32)
                + lax.dot_general(w_hi, h1_lo, nt_dims, preferred_element_type=F32)
                + lax.dot_general(w_lo, h1_hi, nt_dims, preferred_element_type=F32)) + br_ref[...]
    tm = h1.shape[0]
    sub = lax.broadcasted_iota(I32, (N_EXPERTS, tm), 0)
    work = logits_t
    vals, idxs = [], []
    for _ in range(TOP_K):
        mx = jnp.max(work, axis=0, keepdims=True)
        ix = jnp.min(jnp.where(work == mx, sub, N_EXPERTS), axis=0, keepdims=True)
        vals.append(mx)
        idxs.append(ix)
        work = jnp.where(sub == ix, -jnp.inf, work)
    exps = [jnp.exp(vl - vals[0]) for vl in vals]
    den = exps[0] + exps[1] + exps[2] + exps[3]

    sel = jnp.zeros((N_EXPERTS, tm), F32)
    for ix in idxs:
        sel = sel + (sub == ix).astype(F32)
    selb = sel.astype(BF16)
    earlier = (lax.broadcasted_iota(I32, (tm, tm), 0)
               < lax.broadcasted_iota(I32, (tm, tm), 1)).astype(BF16)
    lower = (lax.broadcasted_iota(I32, (N_EXPERTS, N_EXPERTS), 1)
             < lax.broadcasted_iota(I32, (N_EXPERTS, N_EXPERTS), 0)).astype(BF16)
    cnt = jnp.sum(sel, axis=1, keepdims=True)
    run = jnp.floor((cnt + (SUBLANES - 1)) * (1.0 / SUBLANES)) * SUBLANES
    run_b = jnp.broadcast_to(run, (N_EXPERTS, LANES)).astype(BF16)
    seg_off = jnp.dot(lower, run_b, preferred_element_type=F32)[:, 0:1]
    posmat = jnp.dot(selb, earlier, preferred_element_type=F32) + seg_off
    pos = [jnp.sum(jnp.where(sub == ix, posmat, 0.0), axis=0, keepdims=True) for ix in idxs]
    pos_out[...] = jnp.concatenate(pos, axis=0).astype(I32)
    gate_out[...] = jnp.concatenate([e / den for e in exps], axis=0)
    cnt_out[...] = cnt


def _out_proj(xt, ln_g, ln_b, attn, hgo, mla_g, w_o, l1g, l1b, w_r, b_r):
    T = xt.shape[0]
    row = lambda w: pl.BlockSpec((TM_OUT, w), lambda i: (i, 0))
    full = lambda a: pl.BlockSpec(a.shape, lambda i: (0, 0))
    col = pl.BlockSpec((TOP_K, TM_OUT), lambda i: (0, i))
    sub_tiles = TM_OUT // TM_DISP
    return pl.pallas_call(
        _out_proj_body,
        grid=(T // TM_OUT,),
        in_specs=[row(D_MODEL), full(ln_g), full(ln_b), row(MLA_HEADS * MLA_V), row(HG_W), full(mla_g),
                  full(w_o), full(l1g), full(l1b), full(w_r), full(b_r)],
        out_specs=[row(D_MODEL), row(D_MODEL), col, col,
                   pl.BlockSpec((sub_tiles, N_EXPERTS, 1), lambda i: (i, 0, 0))],
        out_shape=[jax.ShapeDtypeStruct((T, D_MODEL), F32), jax.ShapeDtypeStruct((T, D_MODEL), BF16),
                   jax.ShapeDtypeStruct((TOP_K, T), I32), jax.ShapeDtypeStruct((TOP_K, T), F32),
                   jax.ShapeDtypeStruct((T // TM_DISP, N_EXPERTS, 1), F32)],
        compiler_params=_cparams(("parallel",)),
        name="out_proj_router",
    )(xt, ln_g, ln_b, attn, hgo, mla_g, w_o, l1g, l1b, w_r, b_r)


TILE_SLOTS = TOP_K * TM_DISP + N_EXPERTS * SUBLANES
SEG_BITS = TM_DISP.bit_length()


def _ceil_to(n, m):
    return (n + m - 1) // m * m


def _rows(start, size):
    return pl.ds(pl.multiple_of(start, SUBLANES), size)


def _for_each_piece(n, fn):
    for kbit in reversed(range(SUBLANES.bit_length() - 1, SEG_BITS)):
        size = 1 << kbit

        @pl.when((n & size) != 0)
        def _():
            fn(pl.multiple_of((n >> (kbit + 1)) << (kbit + 1), SUBLANES), size)


def _pack_exact_pair(lo, hi):
    return pltpu.bitcast(hi, I32) | lax.shift_right_logical(pltpu.bitcast(lo, I32), 16)


def _zero_tail_blocks(zeros_ref, out_hbm, n_used, zsem):
    def blk_copy(g):
        return pltpu.make_async_copy(zeros_ref, out_hbm.at[_rows(g * ROW_BLOCK, ROW_BLOCK)], zsem)

    n_total = out_hbm.shape[0] // ROW_BLOCK
    lax.fori_loop(n_used, n_total, lambda g, c: (blk_copy(g).start(), c)[1], 0)
    lax.fori_loop(n_used, n_total, lambda g, c: (blk_copy(g).wait(), c)[1], 0)


def _dispatch_body(segn_ref, segsrc_ref, segdst_ref, padn_ref, padd_ref, nused_ref, pos_ref, xb_ref, xs_out,
                   stage, zeros_ref, sem, zsem):
    i = pl.program_id(0)

    @pl.when(i == 0)
    def _():
        zeros_ref[...] = jnp.zeros_like(zeros_ref)

        def pad_copy(e, off, size):
            return pltpu.make_async_copy(zeros_ref.at[pl.ds(0, size)],
                                         xs_out.at[_rows(padd_ref[e] + off, size)], zsem)

        def fill(e, c):
            _for_each_piece(padn_ref[e], lambda off, size: pad_copy(e, off, size).start())
            return c

        def drain(e, c):
            _for_each_piece(padn_ref[e], lambda off, size: pad_copy(e, off, size).wait())
            return c

        lax.fori_loop(0, N_EXPERTS, fill, 0)
        lax.fori_loop(0, N_EXPERTS, drain, 0)
        _zero_tail_blocks(zeros_ref, xs_out, nused_ref[0], zsem)

    pos = pos_ref[...]
    slot = lax.broadcasted_iota(I32, (TILE_SLOTS, TM_DISP), 0)
    sel = jnp.zeros((TILE_SLOTS, TM_DISP), F32)
    for j in range(TOP_K):
        sel = sel + (slot == pos[j:j + 1, :]).astype(F32)
    rows = jnp.dot(sel.astype(BF16), xb_ref[...], preferred_element_type=F32)
    stage[...] = _pack_exact_pair(rows[:, :D_MODEL // 2], rows[:, D_MODEL // 2:])

    def seg_copy(e, off, size):
        k = i * N_EXPERTS + e
        return pltpu.make_async_copy(stage.at[_rows(segsrc_ref[k] + off, size)],
                                     xs_out.at[_rows(segdst_ref[k] + off, size)], sem)

    def send(e, c):
        _for_each_piece(segn_ref[i * N_EXPERTS + e], lambda off, size: seg_copy(e, off, size).start())
        return c

    def settle(e, c):
        _for_each_piece(segn_ref[i * N_EXPERTS + e], lambda off, size: seg_copy(e, off, size).wait())
        return c

    lax.fori_loop(0, N_EXPERTS, send, 0)
    lax.fori_loop(0, N_EXPERTS, settle, 0)


def _dispatch(seg_n, seg_src, seg_dst, pad_n, pad_dst, n_used, pos_t, h1b, n_rows):
    T = h1b.shape[0]
    grid_spec = pltpu.PrefetchScalarGridSpec(
        num_scalar_prefetch=6,
        grid=(T // TM_DISP,),
        in_specs=[pl.BlockSpec((TOP_K, TM_DISP), lambda i, *_: (0, i)),
                  pl.BlockSpec((TM_DISP, D_MODEL), lambda i, *_: (i, 0))],
        out_specs=pl.BlockSpec(memory_space=pl.ANY),
        scratch_shapes=[pltpu.VMEM((TILE_SLOTS, D_MODEL // 2), I32),
                        pltpu.VMEM((ROW_BLOCK, D_MODEL // 2), I32),
                        pltpu.SemaphoreType.DMA(()), pltpu.SemaphoreType.DMA(())],
    )
    return pl.pallas_call(
        _dispatch_body,
        grid_spec=grid_spec,
        out_shape=jax.ShapeDtypeStruct((n_rows, D_MODEL // 2), I32),
        compiler_params=_cparams(("arbitrary",)),
        name="moe_dispatch",
    )(seg_n, seg_src, seg_dst, pad_n, pad_dst, n_used, pos_t, h1b)


def _experts_body(bstart_ref, bcount_ref, nused_ref, xs_hbm, wu_ref, bu_ref, wd_ref, bd_ref, ys_hbm,
                  wu_bf, wd_bf, xbuf, ybuf, sem_in, sem_out):
    e = pl.program_id(0)
    n_used = nused_ref[0]
    nb = bcount_ref[e]

    def fetch(g, slot):
        rows = pl.ds(pl.multiple_of(g * ROW_BLOCK, ROW_BLOCK), ROW_BLOCK)
        return pltpu.make_async_copy(xs_hbm.at[rows], xbuf.at[slot], sem_in.at[slot])

    def put(g, slot):
        rows = pl.ds(pl.multiple_of(g * ROW_BLOCK, ROW_BLOCK), ROW_BLOCK)
        return pltpu.make_async_copy(ybuf.at[slot], ys_hbm.at[rows], sem_out.at[slot])

    @pl.when(jnp.logical_and(e == 0, n_used > 0))
    def _():
        fetch(0, 0).start()

    @pl.when(nb > 0)
    def _():
        wu_bf[...] = wu_ref[0].astype(BF16)
        wd_bf[...] = wd_ref[0].astype(BF16)

    def block(b, c):
        g = bstart_ref[e] + b
        slot = lax.rem(g, 2)
        fetch(g, slot).wait()

        @pl.when(g + 1 < n_used)
        def _():
            fetch(g + 1, 1 - slot).start()

        lo, hi = _unpack_bf16_pair(xbuf[slot])
        x = jnp.concatenate([lo, hi], axis=1).astype(BF16)
        hb = jnp.dot(x, wu_bf[...], preferred_element_type=F32) + bu_ref[0]
        glu = jnp.minimum(hb[:, :D_EXPERT], SWIGLU_LIMIT)
        lin = jnp.clip(hb[:, D_EXPERT:], -SWIGLU_LIMIT, SWIGLU_LIMIT)
        act = glu * (1.0 / (1.0 + jnp.exp(-SWIGLU_ALPHA * glu))) * (lin + 1.0)
        y = jnp.dot(act.astype(BF16), wd_bf[...], preferred_element_type=F32) + bd_ref[0]

        @pl.when(g >= 2)
        def _():
            put(g - 2, slot).wait()

        ybuf[slot] = _pack_bf16_pair(y[:, :D_MODEL // 2], y[:, D_MODEL // 2:])
        put(g, slot).start()
        return c

    lax.fori_loop(0, nb, block, 0)

    @pl.when(e == N_EXPERTS - 1)
    def _():
        for back in (2, 1):
            @pl.when(n_used >= back)
            def _():
                put(n_used - back, lax.rem(n_used - back, 2)).wait()
        ybuf[0] = jnp.zeros_like(ybuf[0])
        _zero_tail_blocks(ybuf.at[0], ys_hbm, n_used, sem_out.at[0])


def _experts(block_start, block_count, n_used, xs, w_up, b_up, w_down, b_down):
    n_rows = xs.shape[0]
    grid_spec = pltpu.PrefetchScalarGridSpec(
        num_scalar_prefetch=3,
        grid=(N_EXPERTS,),
        in_specs=[pl.BlockSpec(memory_space=pl.ANY),
                  pl.BlockSpec((1, D_MODEL, 2 * D_EXPERT), lambda e, *_: (e, 0, 0)),
                  pl.BlockSpec((1, 1, 2 * D_EXPERT), lambda e, *_: (e, 0, 0)),
                  pl.BlockSpec((1, D_EXPERT, D_MODEL), lambda e, *_: (e, 0, 0)),
                  pl.BlockSpec((1, 1, D_MODEL), lambda e, *_: (e, 0, 0))],
        out_specs=pl.BlockSpec(memory_space=pl.ANY),
        scratch_shapes=[pltpu.VMEM((D_MODEL, 2 * D_EXPERT), BF16), pltpu.VMEM((D_EXPERT, D_MODEL), BF16),
                        pltpu.VMEM((2, ROW_BLOCK, D_MODEL // 2), I32),
                        pltpu.VMEM((2, ROW_BLOCK, D_MODEL // 2), I32),
                        pltpu.SemaphoreType.DMA((2,)), pltpu.SemaphoreType.DMA((2,))],
    )
    return pl.pallas_call(
        _experts_body,
        grid_spec=grid_spec,
        out_shape=jax.ShapeDtypeStruct((n_rows, D_MODEL // 2), I32),
        compiler_params=_cparams(("arbitrary",)),
        name="moe_experts",
    )(block_start, block_count, n_used, xs, w_up, b_up.reshape(N_EXPERTS, 1, 2 * D_EXPERT), w_down,
      b_down.reshape(N_EXPERTS, 1, D_MODEL))


def _combine_body(segn_ref, segsrc_ref, segdst_ref, pos_ref, gate_ref, h1_ref, l2g_ref, l2b_ref, ys_hbm,
                  o_ref, stage, sem):
    i = pl.program_id(0)

    @pl.when(i == 0)
    def _():
        stage[...] = jnp.zeros_like(stage)

    def seg_copy(e, off, size):
        k = i * N_EXPERTS + e
        return pltpu.make_async_copy(ys_hbm.at[_rows(segdst_ref[k] + off, size)],
                                     stage.at[_rows(segsrc_ref[k] + off, size)], sem)

    def fetch(e, c):
        _for_each_piece(segn_ref[i * N_EXPERTS + e], lambda off, size: seg_copy(e, off, size).start())
        return c

    def settle(e, c):
        _for_each_piece(segn_ref[i * N_EXPERTS + e], lambda off, size: seg_copy(e, off, size).wait())
        return c

    lax.fori_loop(0, N_EXPERTS, fetch, 0)
    pos = pos_ref[...]
    gate = gate_ref[...]
    slot = lax.broadcasted_iota(I32, (TM_DISP, TILE_SLOTS), 1)
    w = jnp.zeros((TM_DISP, TILE_SLOTS), F32)
    for j in range(TOP_K):
        w = w + jnp.where(slot == pos[:, j:j + 1], gate[:, j:j + 1], 0.0)
    w_hi = w.astype(BF16)
    w_lo = (w - w_hi.astype(F32)).astype(BF16)
    lax.fori_loop(0, N_EXPERTS, settle, 0)
    lo, hi = _unpack_bf16_pair(stage[...])
    y = jnp.concatenate([lo, hi], axis=1).astype(BF16)
    ffn = (jnp.dot(w_hi, y, preferred_element_type=F32) + jnp.dot(w_lo, y, preferred_element_type=F32))
    o_ref[...] = _layer_norm(DEEPNORM_ALPHA * h1_ref[...] + ffn, l2g_ref[...], l2b_ref[...])


def _combine(seg_n, seg_src, seg_dst, pos, gate, h1, l2g, l2b, ys):
    T = h1.shape[0]
    grid_spec = pltpu.PrefetchScalarGridSpec(
        num_scalar_prefetch=3,
        grid=(T // TM_DISP,),
        in_specs=[pl.BlockSpec((TM_DISP, TOP_K), lambda i, *_: (i, 0)),
                  pl.BlockSpec((TM_DISP, TOP_K), lambda i, *_: (i, 0)),
                  pl.BlockSpec((TM_DISP, D_MODEL), lambda i, *_: (i, 0)),
                  pl.BlockSpec(l2g.shape, lambda i, *_: (0, 0)),
                  pl.BlockSpec(l2b.shape, lambda i, *_: (0, 0)),
                  pl.BlockSpec(memory_space=pl.ANY)],
        out_specs=pl.BlockSpec((TM_DISP, D_MODEL), lambda i, *_: (i, 0)),
        scratch_shapes=[pltpu.VMEM((TILE_SLOTS, D_MODEL // 2), I32), pltpu.SemaphoreType.DMA(())],
    )
    return pl.pallas_call(
        _combine_body,
        grid_spec=grid_spec,
        out_shape=jax.ShapeDtypeStruct((T, D_MODEL), F32),
        compiler_params=_cparams(("arbitrary",)),
        name="moe_combine",
    )(seg_n, seg_src, seg_dst, pos, gate, h1, l2g, l2b, ys)


def _prep_weights(w_in, w_q_b, w_kv_b):
    zc = lambda n: jnp.zeros((D_MODEL, n), F32)
    w_in_p = jnp.concatenate(
        [w_in[:, :KR_OFF], zc(MLA_NOPE), w_in[:, KR_OFF:KR_OFF + MLA_ROPE],
         zc(HEAD_PAD - MLA_NOPE - MLA_ROPE), w_in[:, KR_OFF + MLA_ROPE:]], axis=1).astype(BF16)
    qd = MLA_NOPE + MLA_ROPE
    wq = w_q_b.reshape(Q_LORA, MLA_HEADS, qd)
    wq = jnp.pad(wq, ((0, 0), (0, 0), (0, HEAD_PAD - qd))).reshape(Q_LORA, MLA_HEADS * HEAD_PAD)
    wkv = w_kv_b.reshape(KV_LORA, MLA_HEADS, MLA_NOPE + MLA_V)
    wk = jnp.pad(wkv[:, :, :MLA_NOPE], ((0, 0), (0, 0), (0, HEAD_PAD - MLA_NOPE)))
    wk = wk.reshape(KV_LORA, MLA_HEADS * HEAD_PAD)
    wv = jnp.pad(wkv[:, :, MLA_NOPE:], ((0, 0), (0, 0), (0, HEAD_PAD - MLA_V)))
    wv = wv.reshape(KV_LORA, MLA_HEADS * HEAD_PAD)
    return w_in_p, wq.astype(BF16), jnp.concatenate([wk, wv], axis=1).astype(BF16)


def kernel(x, positions, ln_in_g, ln_in_b, w_in, q_a_norm_g, w_q_b, kv_a_norm_g, w_kv_b, hgrn_lb_logits,
           mla_out_g, hgrn_out_g, w_o, ln1_g, ln1_b, w_router, b_router, w_up, b_up, w_down, b_down,
           ln2_g, ln2_b):
    B, S, D = x.shape
    assert D == D_MODEL and w_in.shape[0] == DEPTH and hgrn_lb_logits.shape[0] == DEPTH + 1
    T = B * S
    xt = x.reshape(T, D)
    r2 = lambda a: a.reshape(1, -1)

    c_tab, s1_tab, s2_tab = _rope_tables(positions)
    w_in_p, w_qb_p, w_kv_p = _prep_weights(w_in[0], w_q_b[0], w_kv_b[0])
    q, k, v, hq, hk, hg, hv, gate = _in_proj(
        xt, r2(ln_in_g), r2(ln_in_b), w_in_p, q_a_norm_g, w_qb_p, kv_a_norm_g, w_kv_p,
        hgrn_lb_logits, c_tab, s1_tab, s2_tab)
    attn = _attention(q, k, v, B, S)
    hgo = _hgrn(hq, hk, hg, hv, gate, hgrn_out_g, B, S)
    h1, h1b, pos_t, gate_t, tile_cnt = _out_proj(
        xt, r2(ln_in_g), r2(ln_in_b), attn, hgo, mla_out_g, w_o[0].astype(BF16), ln1_g, ln1_b,
        w_router[0].T, b_router.reshape(N_EXPERTS, 1))

    n_tiles = T // TM_DISP
    n_rows = T * TOP_K + n_tiles * N_EXPERTS * (SUBLANES - 1) + N_EXPERTS * ROW_BLOCK
    n_rows = _ceil_to(n_rows, ROW_BLOCK)
    tc = _ceil_to(tile_cnt[:, :, 0].astype(I32), SUBLANES)
    cnt = jnp.sum(tc, axis=0)
    padded = _ceil_to(cnt, ROW_BLOCK)
    pad_start = jnp.cumsum(padded) - padded
    seg_n = tc.reshape(-1)
    seg_src = (jnp.cumsum(tc, axis=1) - tc).reshape(-1)
    seg_dst = (pad_start[None, :] + jnp.cumsum(tc, axis=0) - tc).reshape(-1)
    n_used = jnp.sum(padded, keepdims=True) // ROW_BLOCK

    xs = _dispatch(seg_n, seg_src, seg_dst, padded - cnt, pad_start + cnt, n_used, pos_t, h1b, n_rows)
    ys = _experts(pad_start // ROW_BLOCK, padded // ROW_BLOCK, n_used, xs,
                  w_up[0], b_up[0], w_down[0], b_down[0])
    out = _combine(seg_n, seg_src, seg_dst, pos_t.T, gate_t.T, h1, ln2_g, ln2_b, ys)
    return out.reshape(B, S, D)
```

```python
import numpy as np
import jax
import jax.numpy as jnp
from jax import lax
from jax.experimental import pallas as pl
from jax.experimental.pallas import tpu as pltpu

F32 = jnp.float32
BF16 = jnp.bfloat16
I32 = jnp.int32

D_MODEL = 1024
CHUNK = 64
MLA_HEADS = 8
MLA_NOPE = 64
MLA_ROPE = 32
MLA_V = 64
Q_LORA = 384
KV_LORA = 256
ROPE_THETA = 10000.0
HG_HEADS = 4
HG_DK = 128
HG_DV = 128
HG_W = HG_HEADS * HG_DK
N_EXPERTS = 32
TOP_K = 4
D_EXPERT = 1024
SWIGLU_LIMIT = 7.0
SWIGLU_ALPHA = 1.702
LN_EPS = 1e-5
RMS_EPS = 1e-6
DEPTH = 1
DEEPNORM_ALPHA = (2.0 * DEPTH) ** 0.25
LOG2E = 1.4426950408889634

LANES = 128
SUBLANES = 8
HEAD_PAD = 128
HALF_ROPE = MLA_ROPE // 2
KR_OFF = Q_LORA + KV_LORA
HQ_OFF = KR_OFF + HEAD_PAD
IN_COLS_PAD = HQ_OFF + 4 * HG_W

TM_IN = 512
TQ = 256
HG_ROWS = 256
TM_OUT = 512
TM_DISP = 256
ROW_BLOCK = 512
VMEM_LIMIT = 52 * 1024 * 1024


def _cparams(sem, vmem=VMEM_LIMIT):
    return pltpu.CompilerParams(dimension_semantics=sem, vmem_limit_bytes=vmem)


def _layer_norm(x, g, b):
    mu = jnp.mean(x, axis=-1, keepdims=True)
    xc = x - mu
    var = jnp.mean(xc * xc, axis=-1, keepdims=True)
    return xc * lax.rsqrt(var + LN_EPS) * g + b


def _rms_norm(x, g):
    return x * lax.rsqrt(jnp.mean(x * x, axis=-1, keepdims=True) + RMS_EPS) * g


def _split_bf16(a):
    hi = a.astype(BF16)
    return hi, (a - hi.astype(F32)).astype(BF16)


def _sigmoid_pair(x):
    e = jnp.exp(-jnp.abs(x))
    big = 1.0 / (1.0 + e)
    small = e * big
    pos = x >= 0
    return jnp.where(pos, big, small), jnp.where(pos, small, big)


def _cos_sin_body(ang_ref, cos_ref, sin_ref):
    a = ang_ref[...]
    cos_ref[...] = jnp.cos(a)
    sin_ref[...] = jnp.sin(a)


def _rope_cos_sin(positions):
    T = positions.size
    inv_freq = ROPE_THETA ** (-jnp.arange(0, MLA_ROPE, 2, dtype=F32) / MLA_ROPE)
    ang = (positions.astype(F32)[..., None] * inv_freq).reshape(T * HALF_ROPE // LANES, LANES)
    cos, sin = pl.pallas_call(
        _cos_sin_body,
        out_shape=(jax.ShapeDtypeStruct(ang.shape, F32),) * 2,
        name="rope_cos_sin",
    )(ang)
    return cos.reshape(T, HALF_ROPE), sin.reshape(T, HALF_ROPE)


def _rope_placement():
    pc = np.zeros((HALF_ROPE, HEAD_PAD), np.float32)
    ps = np.zeros((HALF_ROPE, 2 * HEAD_PAD), np.float32)
    for j in range(HALF_ROPE):
        pc[j, MLA_NOPE + j] = pc[j, MLA_NOPE + HALF_ROPE + j] = 1.0
        ps[j, MLA_NOPE + HALF_ROPE + j] = 1.0
        ps[j, HEAD_PAD + MLA_NOPE + j] = -1.0
    return jnp.asarray(pc, BF16), jnp.asarray(ps, BF16)


def _rope_tables(cos, sin, pc, ps):
    place = lambda a, p: sum(jnp.dot(part, p, preferred_element_type=F32) for part in _split_bf16(a))
    lane = lax.broadcasted_iota(I32, (1, HEAD_PAD), 1)
    c = place(cos, pc) + (lane < MLA_NOPE).astype(F32)
    s = place(sin, ps)
    return c, s[:, :HEAD_PAD], s[:, HEAD_PAD:]


def _rope(x, c, s1, s2, reps):
    width = reps * HEAD_PAD
    if reps > 1:
        c = jnp.concatenate([c] * reps, axis=1)
        s1 = jnp.concatenate([s1] * reps, axis=1)
        s2 = jnp.concatenate([s2] * reps, axis=1)
    return (x * c + pltpu.roll(x, HALF_ROPE, 1) * s1
            + pltpu.roll(x, width - HALF_ROPE, 1) * s2)


def _in_proj_body(x_ref, lng_ref, lnb_ref, win_ref, qag_ref, wqb_ref, kvg_ref, wkv_ref, lbl_ref,
                  cos_ref, sin_ref, pc_ref, ps_ref,
                  q_out, k_out, v_out, hq_out, hk_out, hg_out, hv_out, gate_out):
    h = _layer_norm(x_ref[...], lng_ref[...], lnb_ref[...])
    p = jnp.dot(h.astype(BF16), win_ref[...], preferred_element_type=F32)
    c, s1, s2 = _rope_tables(cos_ref[...], sin_ref[...], pc_ref[...], ps_ref[...])

    qn = _rms_norm(p[:, :Q_LORA], qag_ref[...]).astype(BF16)
    q = jnp.dot(qn, wqb_ref[...], preferred_element_type=F32)
    scale = (MLA_NOPE + MLA_ROPE) ** -0.5 * LOG2E
    q_out[...] = (_rope(q, c, s1, s2, MLA_HEADS) * scale).astype(BF16)
    kvn = _rms_norm(p[:, Q_LORA:KR_OFF], kvg_ref[...]).astype(BF16)
    kv = jnp.dot(kvn, wkv_ref[...], preferred_element_type=F32)
    kr = _rope(p[:, KR_OFF:HQ_OFF], c, s1, s2, 1)
    k_out[...] = (kv[:, :MLA_HEADS * HEAD_PAD] + jnp.concatenate([kr] * MLA_HEADS, axis=1)).astype(BF16)
    lane = lax.broadcasted_iota(I32, (1, MLA_HEADS * HEAD_PAD), 1)
    ones_lane = (lane % HEAD_PAD == MLA_V).astype(F32)
    v_out[...] = (kv[:, MLA_HEADS * HEAD_PAD:] + ones_lane).astype(BF16)

    lbl = lbl_ref[...]
    le = jnp.exp(lbl - jnp.max(lbl, axis=0, keepdims=True))
    lb = le[0:1] / jnp.sum(le, axis=0, keepdims=True)
    hq = p[:, HQ_OFF:HQ_OFF + HG_W]
    hf = p[:, HQ_OFF + HG_W:HQ_OFF + 2 * HG_W]
    hi = p[:, HQ_OFF + 2 * HG_W:HQ_OFF + 3 * HG_W]
    hg = p[:, HQ_OFF + 3 * HG_W:HQ_OFF + 4 * HG_W]
    hq_out[...] = (hq * _sigmoid_pair(hq)[0]).astype(BF16)
    sg, one_minus_sg = _sigmoid_pair(hf)
    hk_out[...] = ((1.0 - lb) * one_minus_sg).astype(BF16)
    hg_out[...] = jnp.log(lb + (1.0 - lb) * sg)
    hv_out[...] = hi.astype(BF16)
    gate_out[...] = (hg * _sigmoid_pair(hg)[0]).astype(BF16)


def _in_proj(xt, ln_g, ln_b, w_in_p, qa_g, w_qb_p, kv_g, w_kv_p, lb_logits, cos, sin):
    T = xt.shape[0]
    pc, ps = _rope_placement()
    row = lambda w: pl.BlockSpec((TM_IN, w), lambda i: (i, 0))
    full = lambda a: pl.BlockSpec(a.shape, lambda i: (0, 0))
    outs = [(MLA_HEADS * HEAD_PAD, BF16), (MLA_HEADS * HEAD_PAD, BF16), (MLA_HEADS * HEAD_PAD, BF16),
            (HG_W, BF16), (HG_W, BF16), (HG_W, F32), (HG_W, BF16), (HG_W, BF16)]
    return pl.pallas_call(
        _in_proj_body,
        grid=(T // TM_IN,),
        in_specs=[row(D_MODEL), full(ln_g), full(ln_b), full(w_in_p), full(qa_g), full(w_qb_p),
                  full(kv_g), full(w_kv_p), full(lb_logits), row(HALF_ROPE), row(HALF_ROPE),
                  full(pc), full(ps)],
        out_specs=[row(w) for w, _ in outs],
        out_shape=[jax.ShapeDtypeStruct((T, w), d) for w, d in outs],
        compiler_params=_cparams(("parallel",)),
        name="in_proj",
    )(xt, ln_g, ln_b, w_in_p, qa_g, w_qb_p, kv_g, w_kv_p, lb_logits, cos, sin, pc, ps)


def _attn_body(q_ref, k_ref, v_ref, o_ref):
    n_q = q_ref.shape[1] // TQ
    dn = (((1,), (1,)), ((), ()))
    row_c = lax.broadcasted_iota(I32, (TQ, TQ), 0) // CHUNK
    col_c = lax.broadcasted_iota(I32, (TQ, TQ), 1) // CHUNK
    diag_mask = col_c <= row_c
    heads = [slice(hh * HEAD_PAD, (hh + 1) * HEAD_PAD) for hh in range(2)]
    lane = lax.broadcasted_iota(I32, (TQ, 2 * MLA_V), 1)
    for qi in range(n_q):
        r0, kv = qi * TQ, (qi + 1) * TQ
        res = []
        for hs in heads:
            s = lax.dot_general(q_ref[0, r0:kv, hs], k_ref[0, :kv, hs], dn, preferred_element_type=F32)
            s_diag = jnp.where(diag_mask, s[:, r0:], -jnp.inf)
            m = jnp.max(s_diag, axis=-1, keepdims=True)
            if qi:
                m = jnp.maximum(m, jnp.max(s[:, :r0], axis=-1, keepdims=True))
                p = jnp.concatenate([jnp.exp2(s[:, :r0] - m), jnp.exp2(s_diag - m)], axis=1)
            else:
                p = jnp.exp2(s_diag - m)
            a = jnp.dot(p.astype(BF16), v_ref[0, :kv, hs], preferred_element_type=F32)
            res.append(a * (1.0 / a[:, MLA_V:MLA_V + 1]))
        o_ref[0, r0:kv, :] = jnp.where(lane < MLA_V, res[0], pltpu.roll(res[1], MLA_V, 1)).astype(BF16)


def _attention(q, k, v, B, S):
    w = MLA_HEADS * HEAD_PAD
    blk = pl.BlockSpec((1, S, 2 * HEAD_PAD), lambda b, h: (b, 0, h))
    out = pl.pallas_call(
        _attn_body,
        grid=(B, MLA_HEADS // 2),
        in_specs=[blk, blk, blk],
        out_specs=pl.BlockSpec((1, S, 2 * MLA_V), lambda b, h: (b, 0, h)),
        out_shape=jax.ShapeDtypeStruct((B, S, MLA_HEADS * MLA_V), BF16),
        compiler_params=_cparams(("parallel", "parallel")),
        name="mla_attention",
    )(q.reshape(B, S, w), k.reshape(B, S, w), v.reshape(B, S, w))
    return out.reshape(B * S, MLA_HEADS * MLA_V)


N_LEVELS = 6
Z_BLOCKS = N_LEVELS + 1


def _hgrn_constants():
    z = np.zeros((Z_BLOCKS, CHUNK, CHUNK), np.float32)
    masks = np.zeros((N_LEVELS, CHUNK, CHUNK), np.float32)
    idx = np.arange(CHUNK)
    for l in range(N_LEVELS):
        hs = CHUNK >> (l + 1)
        parent = idx // (2 * hs)
        right = (idx % (2 * hs)) >= hs
        mid = parent * 2 * hs + hs
        for t in range(CHUNK):
            z[l, t, :mid[t]] = 1.0
        masks[l] = (parent[:, None] == parent[None, :]) & right[:, None] & (~right)[None, :]
    z[N_LEVELS] = np.tril(np.ones((CHUNK, CHUNK), np.float32))
    z = z.reshape(Z_BLOCKS * CHUNK, CHUNK)
    n_c = HG_ROWS // CHUNK
    step_masks = np.zeros((N_LEVELS, HG_ROWS, HG_ROWS), np.float32)
    for c in range(n_c):
        step_masks[:, c * CHUNK:(c + 1) * CHUNK, c * CHUNK:(c + 1) * CHUNK] = masks
    return np.concatenate([z, z], axis=1), step_masks


def _hgrn_body(q_ref, k_ref, g_ref, v_ref, gate_ref, og_ref, z_ref, mask_ref, o_ref, state_ref):
    @pl.when(pl.program_id(1) == 0)
    def _():
        state_ref[...] = jnp.zeros_like(state_ref)

    dn_t = (((1,), (1,)), ((), ()))
    n_c = HG_ROWS // CHUNK
    eye = (lax.broadcasted_iota(I32, (HG_ROWS, HG_ROWS), 0)
           == lax.broadcasted_iota(I32, (HG_ROWS, HG_ROWS), 1)).astype(F32)
    sums = []
    for c in range(n_c):
        g2 = g_ref[0, c * CHUNK:(c + 1) * CHUNK, :] * LOG2E
        g_hi, g_lo = _split_bf16(g2)
        sums.append(jnp.dot(z_ref[...], jnp.concatenate([g_hi, g_lo], axis=0),
                            preferred_element_type=F32))
    level = lambda l, cols: jnp.concatenate([s[l * CHUNK:(l + 1) * CHUNK, cols] for s in sums], axis=0)
    for h in range(HG_HEADS):
        cols = slice(h * HG_DK, (h + 1) * HG_DK)
        b = level(N_LEVELS, cols)
        q = q_ref[0, :, cols].astype(F32)
        k = k_ref[0, :, cols].astype(F32)
        v = v_ref[0, :, cols]
        a = eye * jnp.sum(q * k, axis=-1, keepdims=True)
        for l in range(N_LEVELS):
            d = jnp.exp2(-jnp.abs(b - level(l, cols)))
            a = a + mask_ref[l] * lax.dot_general((q * d).astype(BF16), (k * d).astype(BF16), dn_t,
                                                  preferred_element_type=F32)
        o_intra = jnp.dot(a.astype(BF16), v, preferred_element_type=F32)
        d_in = jnp.exp2(b)
        qd = (q * d_in).astype(BF16)
        st = state_ref[h]
        o_parts = []
        for c in range(n_c):
            rows = slice(c * CHUNK, (c + 1) * CHUNK)
            last = slice((c + 1) * CHUNK - 1, (c + 1) * CHUNK)
            o_parts.append(o_intra[rows] + lax.dot_general(qd[rows], st.astype(BF16), dn_t,
                                                           preferred_element_type=F32))
            kh = (k[rows] * jnp.exp2(b[last] - b[rows])).astype(BF16)
            upd = lax.dot_general(v[rows], kh, (((0,), (0,)), ((), ())), preferred_element_type=F32)
            st = st * d_in[last] + upd
        state_ref[h] = st
        o = jnp.concatenate(o_parts, axis=0)
        o = _rms_norm(o, og_ref[:, cols]) * gate_ref[0, :, cols].astype(F32)
        o_ref[0, :, cols] = o.astype(BF16)


def _hgrn(hq, hk, hg, hv, gate, out_g, B, S):
    zc, masks = _hgrn_constants()
    zc = jnp.asarray(zc, BF16)
    masks = jnp.asarray(masks, F32)
    blk = pl.BlockSpec((1, HG_ROWS, HG_W), lambda b, i: (b, i, 0))
    r3 = lambda a: a.reshape(B, S, HG_W)
    out = pl.pallas_call(
        _hgrn_body,
        grid=(B, S // HG_ROWS),
        in_specs=[blk, blk, blk, blk, blk,
                  pl.BlockSpec((1, HG_W), lambda b, i: (0, 0)),
                  pl.BlockSpec(zc.shape, lambda b, i: (0, 0)),
                  pl.BlockSpec(masks.shape, lambda b, i: (0, 0, 0))],
        out_specs=blk,
        out_shape=jax.ShapeDtypeStruct((B, S, HG_W), BF16),
        scratch_shapes=[pltpu.VMEM((HG_HEADS, HG_DV, HG_DK), F32)],
        compiler_params=_cparams(("parallel", "arbitrary")),
        name="hgrn2",
    )(r3(hq), r3(hk), r3(hg), r3(hv), r3(gate), out_g, zc, masks)
    return out.reshape(B * S, HG_W)


def _pack_bf16_pair(lo, hi):
    lo_b = pltpu.bitcast(lo.astype(BF16).astype(F32), I32)
    hi_b = pltpu.bitcast(hi.astype(BF16).astype(F32), I32)
    return hi_b | lax.shift_right_logical(lo_b, 16)


def _unpack_bf16_pair(w):
    lo = pltpu.bitcast(lax.shift_left(w, 16), F32)
    hi = pltpu.bitcast(w & jnp.int32(-65536), F32)
    return lo, hi


def _out_proj_body(x_ref, lng_ref, lnb_ref, attn_ref, hgo_ref, mg_ref, wo_ref, l1g_ref, l1b_ref,
                   wr_ref, br_ref,
                   h1_out, h1b_out, pos_out, gate_out, cnt_out):
    for sub_tile in range(TM_OUT // TM_DISP):
        rs = slice(sub_tile * TM_DISP, (sub_tile + 1) * TM_DISP)
        _out_proj_tile(x_ref[rs, :], lng_ref, lnb_ref, attn_ref[rs, :], hgo_ref[rs, :], mg_ref, wo_ref,
                       l1g_ref, l1b_ref, wr_ref, br_ref,
                       h1_out.at[rs, :], h1b_out.at[rs, :], pos_out.at[:, rs], gate_out.at[:, rs],
                       cnt_out.at[sub_tile])


def _out_proj_tile(x, lng_ref, lnb_ref, attn, hgo, mg_ref, wo_ref, l1g_ref, l1b_ref, wr_ref, br_ref,
                   h1_out, h1b_out, pos_out, gate_out, cnt_out):
    h = _layer_norm(x, lng_ref[...], lnb_ref[...])
    attn = _rms_norm(attn.astype(F32), mg_ref[...]).astype(BF16)
    mix_in = jnp.concatenate([attn, hgo], axis=1)
    mix = jnp.dot(mix_in, wo_ref[...], preferred_element_type=F32)
    h1 = _layer_norm(DEEPNORM_ALPHA * h + mix, l1g_ref[...], l1b_ref[...])
    h1_out[...] = h1
    h1_hi, h1_lo = _split_bf16(h1)
    h1b_out[...] = h1_hi

    w_hi, w_lo = _split_bf16(wr_ref[...])
    nt_dims = (((1,), (1,)), ((), ()))
    logits_t = (lax.dot_general(w_hi, h1_hi, nt_dims, preferred_element_type=F32)
                + lax.dot_general(w_hi, h1_lo, nt_dims, preferred_element_type=F32)
                + lax.dot_general(w_lo, h1_hi, nt_dims, preferred_element_type=F32)) + br_ref[...]
    tm = h1.shape[0]
    sub = lax.broadcasted_iota(I32, (N_EXPERTS, tm), 0)
    work = logits_t
    vals, idxs = [], []
    for _ in range(TOP_K):
        mx = jnp.max(work, axis=0, keepdims=True)
        ix = jnp.min(jnp.where(work == mx, sub, N_EXPERTS), axis=0, keepdims=True)
        vals.append(mx)
        idxs.append(ix)
        work = jnp.where(sub == ix, -jnp.inf, work)
    exps = [jnp.exp(vl - vals[0]) for vl in vals]
    den = exps[0] + exps[1] + exps[2] + exps[3]

    sel = jnp.zeros((N_EXPERTS, tm), F32)
    for ix in idxs:
        sel = sel + (sub == ix).astype(F32)
    selb = sel.astype(BF16)
    earlier = (lax.broadcasted_iota(I32, (tm, tm), 0)
               < lax.broadcasted_iota(I32, (tm, tm), 1)).astype(BF16)
    lower = (lax.broadcasted_iota(I32, (N_EXPERTS, N_EXPERTS), 1)
             < lax.broadcasted_iota(I32, (N_EXPERTS, N_EXPERTS), 0)).astype(BF16)
    cnt = jnp.sum(sel, axis=1, keepdims=True)
    run = jnp.floor((cnt + (SUBLANES - 1)) * (1.0 / SUBLANES)) * SUBLANES
    run_b = jnp.broadcast_to(run, (N_EXPERTS, LANES)).astype(BF16)
    seg_off = jnp.dot(lower, run_b, preferred_element_type=F32)[:, 0:1]
    posmat = jnp.dot(selb, earlier, preferred_element_type=F32) + seg_off
    pos = [jnp.sum(jnp.where(sub == ix, posmat, 0.0), axis=0, keepdims=True) for ix in idxs]
    pos_out[...] = jnp.concatenate(pos, axis=0).astype(I32)
    gate_out[...] = jnp.concatenate([e / den for e in exps], axis=0)
    cnt_out[...] = cnt


def _out_proj(xt, ln_g, ln_b, attn, hgo, mla_g, w_o, l1g, l1b, w_r, b_r):
    T = xt.shape[0]
    row = lambda w: pl.BlockSpec((TM_OUT, w), lambda i: (i, 0))
    full = lambda a: pl.BlockSpec(a.shape, lambda i: (0, 0))
    col = pl.BlockSpec((TOP_K, TM_OUT), lambda i: (0, i))
    sub_tiles = TM_OUT // TM_DISP
    return pl.pallas_call(
        _out_proj_body,
        grid=(T // TM_OUT,),
        in_specs=[row(D_MODEL), full(ln_g), full(ln_b), row(MLA_HEADS * MLA_V), row(HG_W), full(mla_g),
                  full(w_o), full(l1g), full(l1b), full(w_r), full(b_r)],
        out_specs=[row(D_MODEL), row(D_MODEL), col, col,
                   pl.BlockSpec((sub_tiles, N_EXPERTS, 1), lambda i: (i, 0, 0))],
        out_shape=[jax.ShapeDtypeStruct((T, D_MODEL), F32), jax.ShapeDtypeStruct((T, D_MODEL), BF16),
                   jax.ShapeDtypeStruct((TOP_K, T), I32), jax.ShapeDtypeStruct((TOP_K, T), F32),
                   jax.ShapeDtypeStruct((T // TM_DISP, N_EXPERTS, 1), F32)],
        compiler_params=_cparams(("parallel",)),
        name="out_proj_router",
    )(xt, ln_g, ln_b, attn, hgo, mla_g, w_o, l1g, l1b, w_r, b_r)


TILE_SLOTS = TOP_K * TM_DISP + N_EXPERTS * SUBLANES
RUN_UNROLL = 4
RARE_RUN = 2 * TM_DISP * TOP_K // N_EXPERTS


def _ceil_to(n, m):
    return (n + m - 1) // m * m


def _rows(start, size):
    return pl.ds(pl.multiple_of(start, SUBLANES), size)


def _for_each_piece(n, fn, max_rows=TM_DISP, rare_from=None):
    def pieces(kbits):
        for kbit in kbits:
            size = 1 << kbit

            @pl.when((n & size) != 0)
            def _():
                fn(pl.multiple_of((n >> (kbit + 1)) << (kbit + 1), SUBLANES), size)

    kbits = list(reversed(range(SUBLANES.bit_length() - 1, max_rows.bit_length())))
    if rare_from is None:
        pieces(kbits)
    else:
        pl.when(n >= rare_from)(lambda: pieces([k for k in kbits if (1 << k) >= rare_from]))
        pieces([k for k in kbits if (1 << k) < rare_from])


def _pack_exact_pair(lo, hi):
    return pltpu.bitcast(hi, I32) | lax.shift_right_logical(pltpu.bitcast(lo, I32), 16)


def _zero_tail_blocks(zeros_ref, out_hbm, n_used, zsem):
    def blk_copy(g):
        return pltpu.make_async_copy(zeros_ref, out_hbm.at[_rows(g * ROW_BLOCK, ROW_BLOCK)], zsem)

    n_total = out_hbm.shape[0] // ROW_BLOCK
    lax.fori_loop(n_used, n_total, lambda g, c: (blk_copy(g).start(), c)[1], 0)
    lax.fori_loop(n_used, n_total, lambda g, c: (blk_copy(g).wait(), c)[1], 0)


def _dispatch_body(segn_ref, segsrc_ref, segdst_ref, tot_ref, padn_ref, padd_ref, nused_ref, pos_ref, xb_ref, xs_out,
                   stage, zeros_ref, sem, zsem):
    i = pl.program_id(0)

    @pl.when(i == 0)
    def _():
        zeros_ref[...] = jnp.zeros_like(zeros_ref)

        def pad_copy(e, off, size):
            return pltpu.make_async_copy(zeros_ref.at[pl.ds(0, size)],
                                         xs_out.at[_rows(padd_ref[e] + off, size)], zsem)

        def fill(e, c):
            _for_each_piece(padn_ref[e], lambda off, size: pad_copy(e, off, size).start())
            return c

        def drain(e, c):
            _for_each_piece(padn_ref[e], lambda off, size: pad_copy(e, off, size).wait())
            return c

        lax.fori_loop(0, N_EXPERTS, fill, 0)
        lax.fori_loop(0, N_EXPERTS, drain, 0)
        _zero_tail_blocks(zeros_ref, xs_out, nused_ref[0], zsem)

    pos = pos_ref[...]
    slot = lax.broadcasted_iota(I32, (TILE_SLOTS, TM_DISP), 0)
    sel = jnp.zeros((TILE_SLOTS, TM_DISP), F32)
    for j in range(TOP_K):
        sel = sel + (slot == pos[j:j + 1, :]).astype(F32)
    rows = jnp.dot(sel.astype(BF16), xb_ref[...], preferred_element_type=F32)
    stage[...] = _pack_exact_pair(rows[:, :D_MODEL // 2], rows[:, D_MODEL // 2:])

    def send(eb, c):
        for u in range(RUN_UNROLL):
            k = i * N_EXPERTS + eb * RUN_UNROLL + u
            src, dst = segsrc_ref[k], segdst_ref[k]
            _for_each_piece(segn_ref[k], lambda off, size: pltpu.make_async_copy(
                stage.at[_rows(src + off, size)], xs_out.at[_rows(dst + off, size)], sem).start(),
                rare_from=RARE_RUN)
        return c

    lax.fori_loop(0, N_EXPERTS // RUN_UNROLL, send, 0)
    _for_each_piece(tot_ref[i], lambda off, size: pltpu.make_async_copy(
        stage.at[pl.ds(0, size)], xs_out.at[pl.ds(0, size)], sem).wait(), TILE_SLOTS)


def _dispatch(seg_n, seg_src, seg_dst, tile_tot, pad_n, pad_dst, n_used, pos_t, h1b, n_rows):
    T = h1b.shape[0]
    grid_spec = pltpu.PrefetchScalarGridSpec(
        num_scalar_prefetch=7,
        grid=(T // TM_DISP,),
        in_specs=[pl.BlockSpec((TOP_K, TM_DISP), lambda i, *_: (0, i)),
                  pl.BlockSpec((TM_DISP, D_MODEL), lambda i, *_: (i, 0))],
        out_specs=pl.BlockSpec(memory_space=pl.ANY),
        scratch_shapes=[pltpu.VMEM((TILE_SLOTS, D_MODEL // 2), I32),
                        pltpu.VMEM((ROW_BLOCK, D_MODEL // 2), I32),
                        pltpu.SemaphoreType.DMA(()), pltpu.SemaphoreType.DMA(())],
    )
    return pl.pallas_call(
        _dispatch_body,
        grid_spec=grid_spec,
        out_shape=jax.ShapeDtypeStruct((n_rows, D_MODEL // 2), I32),
        compiler_params=_cparams(("arbitrary",)),
        name="moe_dispatch",
    )(seg_n, seg_src, seg_dst, tile_tot, pad_n, pad_dst, n_used, pos_t, h1b)


def _experts_body(bstart_ref, bcount_ref, nused_ref, xs_hbm, wu_ref, bu_ref, wd_ref, bd_ref, ys_hbm,
                  wu_bf, wd_bf, xbuf, ybuf, sem_in, sem_out):
    e = pl.program_id(0)
    n_used = nused_ref[0]
    nb = bcount_ref[e]

    def fetch(g, slot):
        rows = pl.ds(pl.multiple_of(g * ROW_BLOCK, ROW_BLOCK), ROW_BLOCK)
        return pltpu.make_async_copy(xs_hbm.at[rows], xbuf.at[slot], sem_in.at[slot])

    def put(g, slot):
        rows = pl.ds(pl.multiple_of(g * ROW_BLOCK, ROW_BLOCK), ROW_BLOCK)
        return pltpu.make_async_copy(ybuf.at[slot], ys_hbm.at[rows], sem_out.at[slot])

    @pl.when(jnp.logical_and(e == 0, n_used > 0))
    def _():
        fetch(0, 0).start()

    @pl.when(nb > 0)
    def _():
        wu_bf[...] = wu_ref[0].astype(BF16)
        wd_bf[...] = wd_ref[0].astype(BF16)

    def block(b, c):
        g = bstart_ref[e] + b
        slot = lax.rem(g, 2)
        fetch(g, slot).wait()

        @pl.when(g + 1 < n_used)
        def _():
            fetch(g + 1, 1 - slot).start()

        lo, hi = _unpack_bf16_pair(xbuf[slot])
        x = jnp.concatenate([lo, hi], axis=1).astype(BF16)
        hb = jnp.dot(x, wu_bf[...], preferred_element_type=F32) + bu_ref[0]
        glu = jnp.minimum(hb[:, :D_EXPERT], SWIGLU_LIMIT)
        lin = jnp.clip(hb[:, D_EXPERT:], -SWIGLU_LIMIT, SWIGLU_LIMIT)
        act = glu * (1.0 / (1.0 + jnp.exp(-SWIGLU_ALPHA * glu))) * (lin + 1.0)
        y = jnp.dot(act.astype(BF16), wd_bf[...], preferred_element_type=F32) + bd_ref[0]

        @pl.when(g >= 2)
        def _():
            put(g - 2, slot).wait()

        ybuf[slot] = _pack_bf16_pair(y[:, :D_MODEL // 2], y[:, D_MODEL // 2:])
        put(g, slot).start()
        return c

    lax.fori_loop(0, nb, block, 0)

    @pl.when(e == N_EXPERTS - 1)
    def _():
        for back in (2, 1):
            @pl.when(n_used >= back)
            def _():
                put(n_used - back, lax.rem(n_used - back, 2)).wait()
        ybuf[0] = jnp.zeros_like(ybuf[0])
        _zero_tail_blocks(ybuf.at[0], ys_hbm, n_used, sem_out.at[0])


def _experts(block_start, block_count, n_used, xs, w_up, b_up, w_down, b_down):
    n_rows = xs.shape[0]
    grid_spec = pltpu.PrefetchScalarGridSpec(
        num_scalar_prefetch=3,
        grid=(N_EXPERTS,),
        in_specs=[pl.BlockSpec(memory_space=pl.ANY),
                  pl.BlockSpec((1, D_MODEL, 2 * D_EXPERT), lambda e, *_: (e, 0, 0)),
                  pl.BlockSpec((1, 1, 2 * D_EXPERT), lambda e, *_: (e, 0, 0)),
                  pl.BlockSpec((1, D_EXPERT, D_MODEL), lambda e, *_: (e, 0, 0)),
                  pl.BlockSpec((1, 1, D_MODEL), lambda e, *_: (e, 0, 0))],
        out_specs=pl.BlockSpec(memory_space=pl.ANY),
        scratch_shapes=[pltpu.VMEM((D_MODEL, 2 * D_EXPERT), BF16), pltpu.VMEM((D_EXPERT, D_MODEL), BF16),
                        pltpu.VMEM((2, ROW_BLOCK, D_MODEL // 2), I32),
                        pltpu.VMEM((2, ROW_BLOCK, D_MODEL // 2), I32),
                        pltpu.SemaphoreType.DMA((2,)), pltpu.SemaphoreType.DMA((2,))],
    )
    return pl.pallas_call(
        _experts_body,
        grid_spec=grid_spec,
        out_shape=jax.ShapeDtypeStruct((n_rows, D_MODEL // 2), I32),
        compiler_params=_cparams(("arbitrary",)),
        name="moe_experts",
    )(block_start, block_count, n_used, xs, w_up, b_up.reshape(N_EXPERTS, 1, 2 * D_EXPERT), w_down,
      b_down.reshape(N_EXPERTS, 1, D_MODEL))


def _combine_body(segn_ref, segsrc_ref, segdst_ref, tot_ref, pos_ref, gate_ref, h1_ref, l2g_ref, l2b_ref, ys_hbm,
                  o_ref, stage, sem):
    i = pl.program_id(0)

    @pl.when(i == 0)
    def _():
        stage[...] = jnp.zeros_like(stage)

    def fetch(eb, c):
        for u in range(RUN_UNROLL):
            k = i * N_EXPERTS + eb * RUN_UNROLL + u
            src, dst = segsrc_ref[k], segdst_ref[k]
            _for_each_piece(segn_ref[k], lambda off, size: pltpu.make_async_copy(
                ys_hbm.at[_rows(dst + off, size)], stage.at[_rows(src + off, size)], sem).start(),
                rare_from=RARE_RUN)
        return c

    lax.fori_loop(0, N_EXPERTS // RUN_UNROLL, fetch, 0)
    pos = pos_ref[...]
    gate = gate_ref[...]
    slot = lax.broadcasted_iota(I32, (TM_DISP, TILE_SLOTS), 1)
    w = jnp.zeros((TM_DISP, TILE_SLOTS), F32)
    for j in range(TOP_K):
        w = w + jnp.where(slot == pos[:, j:j + 1], gate[:, j:j + 1], 0.0)
    w_hi = w.astype(BF16)
    w_lo = (w - w_hi.astype(F32)).astype(BF16)
    _for_each_piece(tot_ref[i], lambda off, size: pltpu.make_async_copy(
        ys_hbm.at[pl.ds(0, size)], stage.at[pl.ds(0, size)], sem).wait(), TILE_SLOTS)
    lo, hi = _unpack_bf16_pair(stage[...])
    y = jnp.concatenate([lo, hi], axis=1).astype(BF16)
    ffn = (jnp.dot(w_hi, y, preferred_element_type=F32) + jnp.dot(w_lo, y, preferred_element_type=F32))
    o_ref[...] = _layer_norm(DEEPNORM_ALPHA * h1_ref[...] + ffn, l2g_ref[...], l2b_ref[...])


def _combine(seg_n, seg_src, seg_dst, tile_tot, pos, gate, h1, l2g, l2b, ys):
    T = h1.shape[0]
    grid_spec = pltpu.PrefetchScalarGridSpec(
        num_scalar_prefetch=4,
        grid=(T // TM_DISP,),
        in_specs=[pl.BlockSpec((TM_DISP, TOP_K), lambda i, *_: (i, 0)),
                  pl.BlockSpec((TM_DISP, TOP_K), lambda i, *_: (i, 0)),
                  pl.BlockSpec((TM_DISP, D_MODEL), lambda i, *_: (i, 0)),
                  pl.BlockSpec(l2g.shape, lambda i, *_: (0, 0)),
                  pl.BlockSpec(l2b.shape, lambda i, *_: (0, 0)),
                  pl.BlockSpec(memory_space=pl.ANY)],
        out_specs=pl.BlockSpec((TM_DISP, D_MODEL), lambda i, *_: (i, 0)),
        scratch_shapes=[pltpu.VMEM((TILE_SLOTS, D_MODEL // 2), I32), pltpu.SemaphoreType.DMA(())],
    )
    return pl.pallas_call(
        _combine_body,
        grid_spec=grid_spec,
        out_shape=jax.ShapeDtypeStruct((T, D_MODEL), F32),
        compiler_params=_cparams(("arbitrary",)),
        name="moe_combine",
    )(seg_n, seg_src, seg_dst, tile_tot, pos, gate, h1, l2g, l2b, ys)


def _prep_weights(w_in, w_q_b, w_kv_b):
    zc = lambda n: jnp.zeros((D_MODEL, n), F32)
    w_in_p = jnp.concatenate(
        [w_in[:, :KR_OFF], zc(MLA_NOPE), w_in[:, KR_OFF:KR_OFF + MLA_ROPE],
         zc(HEAD_PAD - MLA_NOPE - MLA_ROPE), w_in[:, KR_OFF + MLA_ROPE:]], axis=1).astype(BF16)
    qd = MLA_NOPE + MLA_ROPE
    wq = w_q_b.reshape(Q_LORA, MLA_HEADS, qd)
    wq = jnp.pad(wq, ((0, 0), (0, 0), (0, HEAD_PAD - qd))).reshape(Q_LORA, MLA_HEADS * HEAD_PAD)
    wkv = w_kv_b.reshape(KV_LORA, MLA_HEADS, MLA_NOPE + MLA_V)
    wk = jnp.pad(wkv[:, :, :MLA_NOPE], ((0, 0), (0, 0), (0, HEAD_PAD - MLA_NOPE)))
    wk = wk.reshape(KV_LORA, MLA_HEADS * HEAD_PAD)
    wv = jnp.pad(wkv[:, :, MLA_NOPE:], ((0, 0), (0, 0), (0, HEAD_PAD - MLA_V)))
    wv = wv.reshape(KV_LORA, MLA_HEADS * HEAD_PAD)
    return w_in_p, wq.astype(BF16), jnp.concatenate([wk, wv], axis=1).astype(BF16)


def kernel(x, positions, ln_in_g, ln_in_b, w_in, q_a_norm_g, w_q_b, kv_a_norm_g, w_kv_b, hgrn_lb_logits,
           mla_out_g, hgrn_out_g, w_o, ln1_g, ln1_b, w_router, b_router, w_up, b_up, w_down, b_down,
           ln2_g, ln2_b):
    B, S, D = x.shape
    assert D == D_MODEL and w_in.shape[0] == DEPTH and hgrn_lb_logits.shape[0] == DEPTH + 1
    T = B * S
    xt = x.reshape(T, D)
    r2 = lambda a: a.reshape(1, -1)

    cos, sin = _rope_cos_sin(positions)
    w_in_p, w_qb_p, w_kv_p = _prep_weights(w_in[0], w_q_b[0], w_kv_b[0])
    q, k, v, hq, hk, hg, hv, gate = _in_proj(
        xt, r2(ln_in_g), r2(ln_in_b), w_in_p, q_a_norm_g, w_qb_p, kv_a_norm_g, w_kv_p,
        hgrn_lb_logits, cos, sin)
    attn = _attention(q, k, v, B, S)
    hgo = _hgrn(hq, hk, hg, hv, gate, hgrn_out_g, B, S)
    h1, h1b, pos_t, gate_t, tile_cnt = _out_proj(
        xt, r2(ln_in_g), r2(ln_in_b), attn, hgo, mla_out_g, w_o[0].astype(BF16), ln1_g, ln1_b,
        w_router[0].T, b_router.reshape(N_EXPERTS, 1))

    n_tiles = T // TM_DISP
    n_rows = T * TOP_K + n_tiles * N_EXPERTS * (SUBLANES - 1) + N_EXPERTS * ROW_BLOCK
    n_rows = _ceil_to(n_rows, ROW_BLOCK)
    tc = _ceil_to(tile_cnt[:, :, 0].astype(I32), SUBLANES)
    cnt = jnp.sum(tc, axis=0)
    padded = _ceil_to(cnt, ROW_BLOCK)
    pad_start = jnp.cumsum(padded) - padded
    seg_n = tc.reshape(-1)
    seg_src = (jnp.cumsum(tc, axis=1) - tc).reshape(-1)
    seg_dst = (pad_start[None, :] + jnp.cumsum(tc, axis=0) - tc).reshape(-1)
    n_used = jnp.sum(padded, keepdims=True) // ROW_BLOCK

    tile_tot = jnp.sum(tc, axis=1)
    xs = _dispatch(seg_n, seg_src, seg_dst, tile_tot, padded - cnt, pad_start + cnt, n_used, pos_t, h1b,
                   n_rows)
    ys = _experts(pad_start // ROW_BLOCK, padded // ROW_BLOCK, n_used, xs,
                  w_up[0], b_up[0], w_down[0], b_down[0])
    out = _combine(seg_n, seg_src, seg_dst, tile_tot, pos_t.T, gate_t.T, h1, ln2_g, ln2_b, ys)
    return out.reshape(B, S, D)
```

```python
import numpy as np
import jax
import jax.numpy as jnp
from jax import lax
from jax.experimental import pallas as pl
from jax.experimental.pallas import tpu as pltpu

F32 = jnp.float32
BF16 = jnp.bfloat16
I32 = jnp.int32

D_MODEL = 1024
CHUNK = 64
MLA_HEADS = 8
MLA_NOPE = 64
MLA_ROPE = 32
MLA_V = 64
Q_LORA = 384
KV_LORA = 256
ROPE_THETA = 10000.0
HG_HEADS = 4
HG_DK = 128
HG_DV = 128
HG_W = HG_HEADS * HG_DK
N_EXPERTS = 32
TOP_K = 4
D_EXPERT = 1024
SWIGLU_LIMIT = 7.0
SWIGLU_ALPHA = 1.702
LN_EPS = 1e-5
RMS_EPS = 1e-6
DEPTH = 1
DEEPNORM_ALPHA = (2.0 * DEPTH) ** 0.25
LOG2E = 1.4426950408889634

LANES = 128
SUBLANES = 8
HEAD_PAD = 128
HALF_ROPE = MLA_ROPE // 2
KR_OFF = Q_LORA + KV_LORA
HQ_OFF = KR_OFF + HEAD_PAD
IN_COLS_PAD = HQ_OFF + 4 * HG_W

TM_IN = 512
TQ = 256
HG_ROWS = 256
TM_OUT = 512
TM_DISP = 256
ROW_BLOCK = 512
VMEM_LIMIT = 52 * 1024 * 1024


def _cparams(sem, vmem=VMEM_LIMIT):
    return pltpu.CompilerParams(dimension_semantics=sem, vmem_limit_bytes=vmem)


def _layer_norm(x, g, b):
    mu = jnp.mean(x, axis=-1, keepdims=True)
    xc = x - mu
    var = jnp.mean(xc * xc, axis=-1, keepdims=True)
    return xc * lax.rsqrt(var + LN_EPS) * g + b


def _rms_norm(x, g):
    return x * lax.rsqrt(jnp.mean(x * x, axis=-1, keepdims=True) + RMS_EPS) * g


def _split_bf16(a):
    hi = a.astype(BF16)
    return hi, (a - hi.astype(F32)).astype(BF16)


def _sigmoid_pair(x):
    e = jnp.exp(-jnp.abs(x))
    big = 1.0 / (1.0 + e)
    small = e * big
    pos = x >= 0
    return jnp.where(pos, big, small), jnp.where(pos, small, big)


def _cos_sin_body(ang_ref, cos_ref, sin_ref):
    a = ang_ref[...]
    cos_ref[...] = jnp.cos(a)
    sin_ref[...] = jnp.sin(a)


def _rope_cos_sin(positions):
    T = positions.size
    inv_freq = ROPE_THETA ** (-jnp.arange(0, MLA_ROPE, 2, dtype=F32) / MLA_ROPE)
    ang = (positions.astype(F32)[..., None] * inv_freq).reshape(T * HALF_ROPE // LANES, LANES)
    cos, sin = pl.pallas_call(
        _cos_sin_body,
        out_shape=(jax.ShapeDtypeStruct(ang.shape, F32),) * 2,
        name="rope_cos_sin",
    )(ang)
    return cos.reshape(T, HALF_ROPE), sin.reshape(T, HALF_ROPE)


def _rope_placement():
    pc = np.zeros((HALF_ROPE, HEAD_PAD), np.float32)
    ps = np.zeros((HALF_ROPE, 2 * HEAD_PAD), np.float32)
    for j in range(HALF_ROPE):
        pc[j, MLA_NOPE + j] = pc[j, MLA_NOPE + HALF_ROPE + j] = 1.0
        ps[j, MLA_NOPE + HALF_ROPE + j] = 1.0
        ps[j, HEAD_PAD + MLA_NOPE + j] = -1.0
    return jnp.asarray(pc, BF16), jnp.asarray(ps, BF16)


def _rope_tables(cos, sin, pc, ps):
    place = lambda a, p: sum(jnp.dot(part, p, preferred_element_type=F32) for part in _split_bf16(a))
    lane = lax.broadcasted_iota(I32, (1, HEAD_PAD), 1)
    c = place(cos, pc) + (lane < MLA_NOPE).astype(F32)
    s = place(sin, ps)
    return c, s[:, :HEAD_PAD], s[:, HEAD_PAD:]


def _rope(x, c, s1, s2, reps):
    width = reps * HEAD_PAD
    if reps > 1:
        c = jnp.concatenate([c] * reps, axis=1)
        s1 = jnp.concatenate([s1] * reps, axis=1)
        s2 = jnp.concatenate([s2] * reps, axis=1)
    return (x * c + pltpu.roll(x, HALF_ROPE, 1) * s1
            + pltpu.roll(x, width - HALF_ROPE, 1) * s2)


def _in_proj_body(x_ref, lng_ref, lnb_ref, win_ref, qag_ref, wqb_ref, kvg_ref, wkv_ref, lbl_ref,
                  cos_ref, sin_ref, pc_ref, ps_ref,
                  q_out, k_out, v_out, hq_out, hk_out, hg_out, hv_out, gate_out):
    h = _layer_norm(x_ref[...], lng_ref[...], lnb_ref[...])
    p = jnp.dot(h.astype(BF16), win_ref[...], preferred_element_type=F32)
    c, s1, s2 = _rope_tables(cos_ref[...], sin_ref[...], pc_ref[...], ps_ref[...])

    qn = _rms_norm(p[:, :Q_LORA], qag_ref[...]).astype(BF16)
    q = jnp.dot(qn, wqb_ref[...], preferred_element_type=F32)
    scale = (MLA_NOPE + MLA_ROPE) ** -0.5 * LOG2E
    q_out[...] = (_rope(q, c, s1, s2, MLA_HEADS) * scale).astype(BF16)
    kvn = _rms_norm(p[:, Q_LORA:KR_OFF], kvg_ref[...]).astype(BF16)
    kv = jnp.dot(kvn, wkv_ref[...], preferred_element_type=F32)
    kr = _rope(p[:, KR_OFF:HQ_OFF], c, s1, s2, 1)
    k_out[...] = (kv[:, :MLA_HEADS * HEAD_PAD] + jnp.concatenate([kr] * MLA_HEADS, axis=1)).astype(BF16)
    lane = lax.broadcasted_iota(I32, (1, MLA_HEADS * HEAD_PAD), 1)
    ones_lane = (lane % HEAD_PAD == MLA_V).astype(F32)
    v_out[...] = (kv[:, MLA_HEADS * HEAD_PAD:] + ones_lane).astype(BF16)

    lbl = lbl_ref[...]
    le = jnp.exp(lbl - jnp.max(lbl, axis=0, keepdims=True))
    lb = le[0:1] / jnp.sum(le, axis=0, keepdims=True)
    hq = p[:, HQ_OFF:HQ_OFF + HG_W]
    hf = p[:, HQ_OFF + HG_W:HQ_OFF + 2 * HG_W]
    hi = p[:, HQ_OFF + 2 * HG_W:HQ_OFF + 3 * HG_W]
    hg = p[:, HQ_OFF + 3 * HG_W:HQ_OFF + 4 * HG_W]
    hq_out[...] = (hq * _sigmoid_pair(hq)[0]).astype(BF16)
    sg, one_minus_sg = _sigmoid_pair(hf)
    hk_out[...] = ((1.0 - lb) * one_minus_sg).astype(BF16)
    hg_out[...] = jnp.log(lb + (1.0 - lb) * sg)
    hv_out[...] = hi.astype(BF16)
    gate_out[...] = (hg * _sigmoid_pair(hg)[0]).astype(BF16)


def _in_proj(xt, ln_g, ln_b, w_in_p, qa_g, w_qb_p, kv_g, w_kv_p, lb_logits, cos, sin):
    T = xt.shape[0]
    pc, ps = _rope_placement()
    row = lambda w: pl.BlockSpec((TM_IN, w), lambda i: (i, 0))
    full = lambda a: pl.BlockSpec(a.shape, lambda i: (0, 0))
    outs = [(MLA_HEADS * HEAD_PAD, BF16), (MLA_HEADS * HEAD_PAD, BF16), (MLA_HEADS * HEAD_PAD, BF16),
            (HG_W, BF16), (HG_W, BF16), (HG_W, F32), (HG_W, BF16), (HG_W, BF16)]
    return pl.pallas_call(
        _in_proj_body,
        grid=(T // TM_IN,),
        in_specs=[row(D_MODEL), full(ln_g), full(ln_b), full(w_in_p), full(qa_g), full(w_qb_p),
                  full(kv_g), full(w_kv_p), full(lb_logits), row(HALF_ROPE), row(HALF_ROPE),
                  full(pc), full(ps)],
        out_specs=[row(w) for w, _ in outs],
        out_shape=[jax.ShapeDtypeStruct((T, w), d) for w, d in outs],
        compiler_params=_cparams(("parallel",)),
        name="in_proj",
    )(xt, ln_g, ln_b, w_in_p, qa_g, w_qb_p, kv_g, w_kv_p, lb_logits, cos, sin, pc, ps)


def _attn_body(q_ref, k_ref, v_ref, o_ref):
    n_q = q_ref.shape[1] // TQ
    dn = (((1,), (1,)), ((), ()))
    row_c = lax.broadcasted_iota(I32, (TQ, TQ), 0) // CHUNK
    col_c = lax.broadcasted_iota(I32, (TQ, TQ), 1) // CHUNK
    diag_mask = col_c <= row_c
    heads = [slice(hh * HEAD_PAD, (hh + 1) * HEAD_PAD) for hh in range(2)]
    lane = lax.broadcasted_iota(I32, (TQ, 2 * MLA_V), 1)
    for qi in range(n_q):
        r0, kv = qi * TQ, (qi + 1) * TQ
        res = []
        for hs in heads:
            s = lax.dot_general(q_ref[0, r0:kv, hs], k_ref[0, :kv, hs], dn, preferred_element_type=F32)
            s_diag = jnp.where(diag_mask, s[:, r0:], -jnp.inf)
            m = jnp.max(s_diag, axis=-1, keepdims=True)
            if qi:
                m = jnp.maximum(m, jnp.max(s[:, :r0], axis=-1, keepdims=True))
                p = jnp.concatenate([jnp.exp2(s[:, :r0] - m), jnp.exp2(s_diag - m)], axis=1)
            else:
                p = jnp.exp2(s_diag - m)
            a = jnp.dot(p.astype(BF16), v_ref[0, :kv, hs], preferred_element_type=F32)
            res.append(a * (1.0 / a[:, MLA_V:MLA_V + 1]))
        o_ref[0, r0:kv, :] = jnp.where(lane < MLA_V, res[0], pltpu.roll(res[1], MLA_V, 1)).astype(BF16)


def _attention(q, k, v, B, S):
    w = MLA_HEADS * HEAD_PAD
    blk = pl.BlockSpec((1, S, 2 * HEAD_PAD), lambda b, h: (b, 0, h))
    out = pl.pallas_call(
        _attn_body,
        grid=(B, MLA_HEADS // 2),
        in_specs=[blk, blk, blk],
        out_specs=pl.BlockSpec((1, S, 2 * MLA_V), lambda b, h: (b, 0, h)),
        out_shape=jax.ShapeDtypeStruct((B, S, MLA_HEADS * MLA_V), BF16),
        compiler_params=_cparams(("parallel", "parallel")),
        name="mla_attention",
    )(q.reshape(B, S, w), k.reshape(B, S, w), v.reshape(B, S, w))
    return out.reshape(B * S, MLA_HEADS * MLA_V)


N_LEVELS = 6
Z_BLOCKS = N_LEVELS + 1


def _hgrn_constants():
    z = np.zeros((Z_BLOCKS, CHUNK, CHUNK), np.float32)
    masks = np.zeros((N_LEVELS, CHUNK, CHUNK), np.float32)
    idx = np.arange(CHUNK)
    for l in range(N_LEVELS):
        hs = CHUNK >> (l + 1)
        parent = idx // (2 * hs)
        right = (idx % (2 * hs)) >= hs
        mid = parent * 2 * hs + hs
        for t in range(CHUNK):
            z[l, t, :mid[t]] = 1.0
        masks[l] = (parent[:, None] == parent[None, :]) & right[:, None] & (~right)[None, :]
    z[N_LEVELS] = np.tril(np.ones((CHUNK, CHUNK), np.float32))
    z = z.reshape(Z_BLOCKS * CHUNK, CHUNK)
    n_c = HG_ROWS // CHUNK
    step_masks = np.zeros((N_LEVELS, HG_ROWS, HG_ROWS), np.float32)
    for c in range(n_c):
        step_masks[:, c * CHUNK:(c + 1) * CHUNK, c * CHUNK:(c + 1) * CHUNK] = masks
    return np.concatenate([z, z], axis=1), step_masks


def _hgrn_rows(r0, q_ref, k_ref, g_ref, v_ref, gate_ref, og_ref, z_ref, mask_ref, o_ref, state_ref):
    dn_t = (((1,), (1,)), ((), ()))
    n_c = HG_ROWS // CHUNK
    rr = slice(r0, r0 + HG_ROWS)
    eye = (lax.broadcasted_iota(I32, (HG_ROWS, HG_ROWS), 0)
           == lax.broadcasted_iota(I32, (HG_ROWS, HG_ROWS), 1)).astype(F32)
    sums = []
    for c in range(n_c):
        g2 = g_ref[0, r0 + c * CHUNK:r0 + (c + 1) * CHUNK, :] * LOG2E
        g_hi, g_lo = _split_bf16(g2)
        sums.append(jnp.dot(z_ref[...], jnp.concatenate([g_hi, g_lo], axis=0),
                            preferred_element_type=F32))
    level = lambda l, cols: jnp.concatenate([s[l * CHUNK:(l + 1) * CHUNK, cols] for s in sums], axis=0)
    for h in range(HG_HEADS):
        cols = slice(h * HG_DK, (h + 1) * HG_DK)
        b = level(N_LEVELS, cols)
        q = q_ref[0, rr, cols].astype(F32)
        k = k_ref[0, rr, cols].astype(F32)
        v = v_ref[0, rr, cols]
        a = eye * jnp.sum(q * k, axis=-1, keepdims=True)
        for l in range(N_LEVELS):
            d = jnp.exp2(-jnp.abs(b - level(l, cols)))
            a = a + mask_ref[l] * lax.dot_general((q * d).astype(BF16), (k * d).astype(BF16), dn_t,
                                                  preferred_element_type=F32)
        o_intra = jnp.dot(a.astype(BF16), v, preferred_element_type=F32)
        d_in = jnp.exp2(b)
        qd = (q * d_in).astype(BF16)
        st = state_ref[h]
        o_parts = []
        for c in range(n_c):
            rows = slice(c * CHUNK, (c + 1) * CHUNK)
            last = slice((c + 1) * CHUNK - 1, (c + 1) * CHUNK)
            o_parts.append(o_intra[rows] + lax.dot_general(qd[rows], st.astype(BF16), dn_t,
                                                           preferred_element_type=F32))
            kh = (k[rows] * jnp.exp2(b[last] - b[rows])).astype(BF16)
            upd = lax.dot_general(v[rows], kh, (((0,), (0,)), ((), ())), preferred_element_type=F32)
            st = st * d_in[last] + upd
        state_ref[h] = st
        o = jnp.concatenate(o_parts, axis=0)
        o = _rms_norm(o, og_ref[:, cols]) * gate_ref[0, rr, cols].astype(F32)
        o_ref[0, rr, cols] = o.astype(BF16)


def _hgrn_body(q_ref, k_ref, g_ref, v_ref, gate_ref, og_ref, z_ref, mask_ref, o_ref, state_ref):
    @pl.when(pl.program_id(1) == 0)
    def _():
        state_ref[...] = jnp.zeros_like(state_ref)

    _hgrn_rows(0, q_ref, k_ref, g_ref, v_ref, gate_ref, og_ref, z_ref, mask_ref, o_ref, state_ref)


def _hgrn(hq, hk, hg, hv, gate, out_g, B, S):
    zc, masks = _hgrn_constants()
    zc = jnp.asarray(zc, BF16)
    masks = jnp.asarray(masks, F32)
    blk = pl.BlockSpec((1, HG_ROWS, HG_W), lambda b, i: (b, i, 0))
    r3 = lambda a: a.reshape(B, S, HG_W)
    out = pl.pallas_call(
        _hgrn_body,
        grid=(B, S // HG_ROWS),
        in_specs=[blk, blk, blk, blk, blk,
                  pl.BlockSpec((1, HG_W), lambda b, i: (0, 0)),
                  pl.BlockSpec(zc.shape, lambda b, i: (0, 0)),
                  pl.BlockSpec(masks.shape, lambda b, i: (0, 0, 0))],
        out_specs=blk,
        out_shape=jax.ShapeDtypeStruct((B, S, HG_W), BF16),
        scratch_shapes=[pltpu.VMEM((HG_HEADS, HG_DV, HG_DK), F32)],
        compiler_params=_cparams(("parallel", "arbitrary")),
        name="hgrn2",
    )(r3(hq), r3(hk), r3(hg), r3(hv), r3(gate), out_g, zc, masks)
    return out.reshape(B * S, HG_W)


def _pack_bf16_pair(lo, hi):
    lo_b = pltpu.bitcast(lo.astype(BF16).astype(F32), I32)
    hi_b = pltpu.bitcast(hi.astype(BF16).astype(F32), I32)
    return hi_b | lax.shift_right_logical(lo_b, 16)


def _unpack_bf16_pair(w):
    lo = pltpu.bitcast(lax.shift_left(w, 16), F32)
    hi = pltpu.bitcast(w & jnp.int32(-65536), F32)
    return lo, hi


def _out_proj_body(x_ref, lng_ref, lnb_ref, attn_ref, hgo_ref, mg_ref, wo_ref, l1g_ref, l1b_ref,
                   wr_ref, br_ref,
                   h1_out, h1b_out, pos_out, gate_out, cnt_out):
    for sub_tile in range(TM_OUT // TM_DISP):
        rs = slice(sub_tile * TM_DISP, (sub_tile + 1) * TM_DISP)
        _out_proj_tile(x_ref[rs, :], lng_ref, lnb_ref, attn_ref[rs, :], hgo_ref[rs, :], mg_ref, wo_ref,
                       l1g_ref, l1b_ref, wr_ref, br_ref,
                       h1_out.at[rs, :], h1b_out.at[rs, :], pos_out.at[:, rs], gate_out.at[:, rs],
                       cnt_out.at[sub_tile])


def _out_proj_tile(x, lng_ref, lnb_ref, attn, hgo, mg_ref, wo_ref, l1g_ref, l1b_ref, wr_ref, br_ref,
                   h1_out, h1b_out, pos_out, gate_out, cnt_out):
    h = _layer_norm(x, lng_ref[...], lnb_ref[...])
    attn = _rms_norm(attn.astype(F32), mg_ref[...]).astype(BF16)
    mix_in = jnp.concatenate([attn, hgo], axis=1)
    mix = jnp.dot(mix_in, wo_ref[...], preferred_element_type=F32)
    h1 = _layer_norm(DEEPNORM_ALPHA * h + mix, l1g_ref[...], l1b_ref[...])
    h1_out[...] = h1
    h1_hi, h1_lo = _split_bf16(h1)
    h1b_out[...] = h1_hi

    w_hi, w_lo = _split_bf16(wr_ref[...])
    nt_dims = (((1,), (1,)), ((), ()))
    logits_t = (lax.dot_general(w_hi, h1_hi, nt_dims, preferred_element_type=F32)
                + lax.dot_general(w_hi, h1_lo, nt_dims, preferred_element_type=F32)
                + lax.dot_general(w_lo, h1_hi, nt_dims, preferred_element_type=F32)) + br_ref[...]
    tm = h1.shape[0]
    sub = lax.broadcasted_iota(I32, (N_EXPERTS, tm), 0)
    work = logits_t
    vals, idxs = [], []
    for _ in range(TOP_K):
        mx = jnp.max(work, axis=0, keepdims=True)
        ix = jnp.min(jnp.where(work == mx, sub, N_EXPERTS), axis=0, keepdims=True)
        vals.append(mx)
        idxs.append(ix)
        work = jnp.where(sub == ix, -jnp.inf, work)
    exps = [jnp.exp(vl - vals[0]) for vl in vals]
    den = exps[0] + exps[1] + exps[2] + exps[3]

    sel = jnp.zeros((N_EXPERTS, tm), F32)
    for ix in idxs:
        sel = sel + (sub == ix).astype(F32)
    selb = sel.astype(BF16)
    earlier = (lax.broadcasted_iota(I32, (tm, tm), 0)
               < lax.broadcasted_iota(I32, (tm, tm), 1)).astype(BF16)
    lower = (lax.broadcasted_iota(I32, (N_EXPERTS, N_EXPERTS), 1)
             < lax.broadcasted_iota(I32, (N_EXPERTS, N_EXPERTS), 0)).astype(BF16)
    cnt = jnp.sum(sel, axis=1, keepdims=True)
    run = jnp.floor((cnt + (SUBLANES - 1)) * (1.0 / SUBLANES)) * SUBLANES
    run_b = jnp.broadcast_to(run, (N_EXPERTS, LANES)).astype(BF16)
    seg_off = jnp.dot(lower, run_b, preferred_element_type=F32)[:, 0:1]
    posmat = jnp.dot(selb, earlier, preferred_element_type=F32) + seg_off
    pos = [jnp.sum(jnp.where(sub == ix, posmat, 0.0), axis=0, keepdims=True) for ix in idxs]
    pos_out[...] = jnp.concatenate(pos, axis=0).astype(I32)
    gate_out[...] = jnp.concatenate([e / den for e in exps], axis=0)
    cnt_out[...] = cnt


def _out_proj(xt, ln_g, ln_b, attn, hgo, mla_g, w_o, l1g, l1b, w_r, b_r):
    T = xt.shape[0]
    row = lambda w: pl.BlockSpec((TM_OUT, w), lambda i: (i, 0))
    full = lambda a: pl.BlockSpec(a.shape, lambda i: (0, 0))
    col = pl.BlockSpec((TOP_K, TM_OUT), lambda i: (0, i))
    sub_tiles = TM_OUT // TM_DISP
    return pl.pallas_call(
        _out_proj_body,
        grid=(T // TM_OUT,),
        in_specs=[row(D_MODEL), full(ln_g), full(ln_b), row(MLA_HEADS * MLA_V), row(HG_W), full(mla_g),
                  full(w_o), full(l1g), full(l1b), full(w_r), full(b_r)],
        out_specs=[row(D_MODEL), row(D_MODEL), col, col,
                   pl.BlockSpec((sub_tiles, N_EXPERTS, 1), lambda i: (i, 0, 0))],
        out_shape=[jax.ShapeDtypeStruct((T, D_MODEL), F32), jax.ShapeDtypeStruct((T, D_MODEL), BF16),
                   jax.ShapeDtypeStruct((TOP_K, T), I32), jax.ShapeDtypeStruct((TOP_K, T), F32),
                   jax.ShapeDtypeStruct((T // TM_DISP, N_EXPERTS, 1), F32)],
        compiler_params=_cparams(("parallel",)),
        name="out_proj_router",
    )(xt, ln_g, ln_b, attn, hgo, mla_g, w_o, l1g, l1b, w_r, b_r)


TILE_SLOTS = TOP_K * TM_DISP + N_EXPERTS * SUBLANES
TILES_PER_STEP = 2
TM_STEP = TILES_PER_STEP * TM_DISP
RUN_UNROLL = 4
RARE_RUN = 2 * TM_DISP * TOP_K // N_EXPERTS


def _ceil_to(n, m):
    return (n + m - 1) // m * m


def _rows(start, size):
    return pl.ds(pl.multiple_of(start, SUBLANES), size)


def _for_each_piece(n, fn, max_rows=TM_DISP, rare_from=None):
    def pieces(kbits):
        for kbit in kbits:
            size = 1 << kbit

            @pl.when((n & size) != 0)
            def _():
                fn(pl.multiple_of((n >> (kbit + 1)) << (kbit + 1), SUBLANES), size)

    kbits = list(reversed(range(SUBLANES.bit_length() - 1, max_rows.bit_length())))
    if rare_from is None:
        pieces(kbits)
    else:
        pl.when(n >= rare_from)(lambda: pieces([k for k in kbits if (1 << k) >= rare_from]))
        pieces([k for k in kbits if (1 << k) < rare_from])


def _pack_exact_pair(lo, hi):
    return pltpu.bitcast(hi, I32) | lax.shift_right_logical(pltpu.bitcast(lo, I32), 16)


def _zero_tail_blocks(zeros_ref, out_hbm, n_used, zsem):
    def blk_copy(g):
        return pltpu.make_async_copy(zeros_ref, out_hbm.at[_rows(g * ROW_BLOCK, ROW_BLOCK)], zsem)

    n_total = out_hbm.shape[0] // ROW_BLOCK
    lax.fori_loop(n_used, n_total, lambda g, c: (blk_copy(g).start(), c)[1], 0)
    lax.fori_loop(n_used, n_total, lambda g, c: (blk_copy(g).wait(), c)[1], 0)


def _dispatch_body(segn_ref, segsrc_ref, segdst_ref, tot_ref, padn_ref, padd_ref, nused_ref, pos_ref, xb_ref, xs_out,
                   stage, zeros_ref, sem, zsem):
    i = pl.program_id(0)

    @pl.when(i == 0)
    def _():
        zeros_ref[...] = jnp.zeros_like(zeros_ref)

        def pad_copy(e, off, size):
            return pltpu.make_async_copy(zeros_ref.at[pl.ds(0, size)],
                                         xs_out.at[_rows(padd_ref[e] + off, size)], zsem)

        def fill(e, c):
            _for_each_piece(padn_ref[e], lambda off, size: pad_copy(e, off, size).start())
            return c

        def drain(e, c):
            _for_each_piece(padn_ref[e], lambda off, size: pad_copy(e, off, size).wait())
            return c

        lax.fori_loop(0, N_EXPERTS, fill, 0)
        lax.fori_loop(0, N_EXPERTS, drain, 0)
        _zero_tail_blocks(zeros_ref, xs_out, nused_ref[0], zsem)

    def settle(tile, b):
        _for_each_piece(tot_ref[tile], lambda off, size: pltpu.make_async_copy(
            stage.at[b, pl.ds(0, size)], xs_out.at[pl.ds(0, size)], sem.at[b]).wait(), TILE_SLOTS)

    for b in range(TILES_PER_STEP):
        tile = i * TILES_PER_STEP + b
        ts = slice(b * TM_DISP, (b + 1) * TM_DISP)
        pos = pos_ref[:, ts]
        slot = lax.broadcasted_iota(I32, (TILE_SLOTS, TM_DISP), 0)
        sel = jnp.zeros((TILE_SLOTS, TM_DISP), F32)
        for j in range(TOP_K):
            sel = sel + (slot == pos[j:j + 1, :]).astype(F32)
        rows = jnp.dot(sel.astype(BF16), xb_ref[ts, :], preferred_element_type=F32)
        packed = _pack_exact_pair(rows[:, :D_MODEL // 2], rows[:, D_MODEL // 2:])

        @pl.when(i >= 1)
        def _():
            settle(tile - TILES_PER_STEP, b)

        stage[b] = packed

        def send(eb, c):
            for u in range(RUN_UNROLL):
                k = tile * N_EXPERTS + eb * RUN_UNROLL + u
                src, dst = segsrc_ref[k], segdst_ref[k]
                _for_each_piece(segn_ref[k], lambda off, size: pltpu.make_async_copy(
                    stage.at[b, _rows(src + off, size)], xs_out.at[_rows(dst + off, size)],
                    sem.at[b]).start(), rare_from=RARE_RUN)
            return c

        lax.fori_loop(0, N_EXPERTS // RUN_UNROLL, send, 0)

    @pl.when(i == pl.num_programs(0) - 1)
    def _():
        for b in range(TILES_PER_STEP):
            settle(i * TILES_PER_STEP + b, b)


def _dispatch(seg_n, seg_src, seg_dst, tile_tot, pad_n, pad_dst, n_used, pos_t, h1b, n_rows):
    T = h1b.shape[0]
    grid_spec = pltpu.PrefetchScalarGridSpec(
        num_scalar_prefetch=7,
        grid=(T // TM_STEP,),
        in_specs=[pl.BlockSpec((TOP_K, TM_STEP), lambda i, *_: (0, i)),
                  pl.BlockSpec((TM_STEP, D_MODEL), lambda i, *_: (i, 0))],
        out_specs=pl.BlockSpec(memory_space=pl.ANY),
        scratch_shapes=[pltpu.VMEM((TILES_PER_STEP, TILE_SLOTS, D_MODEL // 2), I32),
                        pltpu.VMEM((ROW_BLOCK, D_MODEL // 2), I32),
                        pltpu.SemaphoreType.DMA((TILES_PER_STEP,)), pltpu.SemaphoreType.DMA(())],
    )
    return pl.pallas_call(
        _dispatch_body,
        grid_spec=grid_spec,
        out_shape=jax.ShapeDtypeStruct((n_rows, D_MODEL // 2), I32),
        compiler_params=_cparams(("arbitrary",)),
        name="moe_dispatch",
    )(seg_n, seg_src, seg_dst, tile_tot, pad_n, pad_dst, n_used, pos_t, h1b)


def _experts_body(bstart_ref, bcount_ref, nused_ref, xs_hbm, wu_ref, bu_ref, wd_ref, bd_ref, ys_hbm,
                  wu_bf, wd_bf, xbuf, ybuf, sem_in, sem_out):
    e = pl.program_id(0)
    n_used = nused_ref[0]
    nb = bcount_ref[e]

    def fetch(g, slot):
        rows = pl.ds(pl.multiple_of(g * ROW_BLOCK, ROW_BLOCK), ROW_BLOCK)
        return pltpu.make_async_copy(xs_hbm.at[rows], xbuf.at[slot], sem_in.at[slot])

    def put(g, slot):
        rows = pl.ds(pl.multiple_of(g * ROW_BLOCK, ROW_BLOCK), ROW_BLOCK)
        return pltpu.make_async_copy(ybuf.at[slot], ys_hbm.at[rows], sem_out.at[slot])

    @pl.when(jnp.logical_and(e == 0, n_used > 0))
    def _():
        fetch(0, 0).start()

    @pl.when(nb > 0)
    def _():
        wu_bf[...] = wu_ref[0].astype(BF16)
        wd_bf[...] = wd_ref[0].astype(BF16)

    def block(b, c):
        g = bstart_ref[e] + b
        slot = lax.rem(g, 2)
        fetch(g, slot).wait()

        @pl.when(g + 1 < n_used)
        def _():
            fetch(g + 1, 1 - slot).start()

        lo, hi = _unpack_bf16_pair(xbuf[slot])
        x = jnp.concatenate([lo, hi], axis=1).astype(BF16)
        hb = jnp.dot(x, wu_bf[...], preferred_element_type=F32) + bu_ref[0]
        glu = jnp.minimum(hb[:, :D_EXPERT], SWIGLU_LIMIT)
        lin = jnp.clip(hb[:, D_EXPERT:], -SWIGLU_LIMIT, SWIGLU_LIMIT)
        act = glu * (1.0 / (1.0 + jnp.exp(-SWIGLU_ALPHA * glu))) * (lin + 1.0)
        y = jnp.dot(act.astype(BF16), wd_bf[...], preferred_element_type=F32) + bd_ref[0]

        @pl.when(g >= 2)
        def _():
            put(g - 2, slot).wait()

        ybuf[slot] = _pack_bf16_pair(y[:, :D_MODEL // 2], y[:, D_MODEL // 2:])
        put(g, slot).start()
        return c

    lax.fori_loop(0, nb, block, 0)

    @pl.when(e == N_EXPERTS - 1)
    def _():
        for back in (2, 1):
            @pl.when(n_used >= back)
            def _():
                put(n_used - back, lax.rem(n_used - back, 2)).wait()
        ybuf[0] = jnp.zeros_like(ybuf[0])
        _zero_tail_blocks(ybuf.at[0], ys_hbm, n_used, sem_out.at[0])


def _experts(block_start, block_count, n_used, xs, w_up, b_up, w_down, b_down):
    n_rows = xs.shape[0]
    grid_spec = pltpu.PrefetchScalarGridSpec(
        num_scalar_prefetch=3,
        grid=(N_EXPERTS,),
        in_specs=[pl.BlockSpec(memory_space=pl.ANY),
                  pl.BlockSpec((1, D_MODEL, 2 * D_EXPERT), lambda e, *_: (e, 0, 0)),
                  pl.BlockSpec((1, 1, 2 * D_EXPERT), lambda e, *_: (e, 0, 0)),
                  pl.BlockSpec((1, D_EXPERT, D_MODEL), lambda e, *_: (e, 0, 0)),
                  pl.BlockSpec((1, 1, D_MODEL), lambda e, *_: (e, 0, 0))],
        out_specs=pl.BlockSpec(memory_space=pl.ANY),
        scratch_shapes=[pltpu.VMEM((D_MODEL, 2 * D_EXPERT), BF16), pltpu.VMEM((D_EXPERT, D_MODEL), BF16),
                        pltpu.VMEM((2, ROW_BLOCK, D_MODEL // 2), I32),
                        pltpu.VMEM((2, ROW_BLOCK, D_MODEL // 2), I32),
                        pltpu.SemaphoreType.DMA((2,)), pltpu.SemaphoreType.DMA((2,))],
    )
    return pl.pallas_call(
        _experts_body,
        grid_spec=grid_spec,
        out_shape=jax.ShapeDtypeStruct((n_rows, D_MODEL // 2), I32),
        compiler_params=_cparams(("arbitrary",)),
        name="moe_experts",
    )(block_start, block_count, n_used, xs, w_up, b_up.reshape(N_EXPERTS, 1, 2 * D_EXPERT), w_down,
      b_down.reshape(N_EXPERTS, 1, D_MODEL))


def _combine_body(segn_ref, segsrc_ref, segdst_ref, tot_ref, pos_ref, gate_ref, h1_ref, l2g_ref, l2b_ref, ys_hbm,
                  o_ref, stage, sem):
    i = pl.program_id(0)
    n_tiles = pl.num_programs(0) * TILES_PER_STEP

    def fetch_tile(tile, b):
        def fetch(eb, c):
            for u in range(RUN_UNROLL):
                k = tile * N_EXPERTS + eb * RUN_UNROLL + u
                src, dst = segsrc_ref[k], segdst_ref[k]
                _for_each_piece(segn_ref[k], lambda off, size: pltpu.make_async_copy(
                    ys_hbm.at[_rows(dst + off, size)], stage.at[b, _rows(src + off, size)],
                    sem.at[b]).start(), rare_from=RARE_RUN)
            return c

        lax.fori_loop(0, N_EXPERTS // RUN_UNROLL, fetch, 0)

    @pl.when(i == 0)
    def _():
        stage[...] = jnp.zeros_like(stage)
        fetch_tile(0, 0)

    for b in range(TILES_PER_STEP):
        tile = i * TILES_PER_STEP + b
        nxt = (b + 1) % TILES_PER_STEP

        @pl.when(tile + 1 < n_tiles)
        def _():
            fetch_tile(tile + 1, nxt)

        ts = slice(b * TM_DISP, (b + 1) * TM_DISP)
        pos = pos_ref[ts, :]
        gate = gate_ref[ts, :]
        slot = lax.broadcasted_iota(I32, (TM_DISP, TILE_SLOTS), 1)
        w = jnp.zeros((TM_DISP, TILE_SLOTS), F32)
        for j in range(TOP_K):
            w = w + jnp.where(slot == pos[:, j:j + 1], gate[:, j:j + 1], 0.0)
        w_hi, w_lo = _split_bf16(w)
        _for_each_piece(tot_ref[tile], lambda off, size: pltpu.make_async_copy(
            ys_hbm.at[pl.ds(0, size)], stage.at[b, pl.ds(0, size)], sem.at[b]).wait(), TILE_SLOTS)
        lo, hi = _unpack_bf16_pair(stage[b])
        y = jnp.concatenate([lo, hi], axis=1).astype(BF16)
        ffn = (jnp.dot(w_hi, y, preferred_element_type=F32)
               + jnp.dot(w_lo, y, preferred_element_type=F32))
        o_ref[ts, :] = _layer_norm(DEEPNORM_ALPHA * h1_ref[ts, :] + ffn, l2g_ref[...], l2b_ref[...])


def _combine(seg_n, seg_src, seg_dst, tile_tot, pos, gate, h1, l2g, l2b, ys):
    T = h1.shape[0]
    grid_spec = pltpu.PrefetchScalarGridSpec(
        num_scalar_prefetch=4,
        grid=(T // TM_STEP,),
        in_specs=[pl.BlockSpec((TM_STEP, TOP_K), lambda i, *_: (i, 0)),
                  pl.BlockSpec((TM_STEP, TOP_K), lambda i, *_: (i, 0)),
                  pl.BlockSpec((TM_STEP, D_MODEL), lambda i, *_: (i, 0)),
                  pl.BlockSpec(l2g.shape, lambda i, *_: (0, 0)),
                  pl.BlockSpec(l2b.shape, lambda i, *_: (0, 0)),
                  pl.BlockSpec(memory_space=pl.ANY)],
        out_specs=pl.BlockSpec((TM_STEP, D_MODEL), lambda i, *_: (i, 0)),
        scratch_shapes=[pltpu.VMEM((TILES_PER_STEP, TILE_SLOTS, D_MODEL // 2), I32),
                        pltpu.SemaphoreType.DMA((TILES_PER_STEP,))],
    )
    return pl.pallas_call(
        _combine_body,
        grid_spec=grid_spec,
        out_shape=jax.ShapeDtypeStruct((T, D_MODEL), F32),
        compiler_params=_cparams(("arbitrary",)),
        name="moe_combine",
    )(seg_n, seg_src, seg_dst, tile_tot, pos, gate, h1, l2g, l2b, ys)


def _prep_weights(w_in, w_q_b, w_kv_b):
    zc = lambda n: jnp.zeros((D_MODEL, n), F32)
    w_in_p = jnp.concatenate(
        [w_in[:, :KR_OFF], zc(MLA_NOPE), w_in[:, KR_OFF:KR_OFF + MLA_ROPE],
         zc(HEAD_PAD - MLA_NOPE - MLA_ROPE), w_in[:, KR_OFF + MLA_ROPE:]], axis=1).astype(BF16)
    qd = MLA_NOPE + MLA_ROPE
    wq = w_q_b.reshape(Q_LORA, MLA_HEADS, qd)
    wq = jnp.pad(wq, ((0, 0), (0, 0), (0, HEAD_PAD - qd))).reshape(Q_LORA, MLA_HEADS * HEAD_PAD)
    wkv = w_kv_b.reshape(KV_LORA, MLA_HEADS, MLA_NOPE + MLA_V)
    wk = jnp.pad(wkv[:, :, :MLA_NOPE], ((0, 0), (0, 0), (0, HEAD_PAD - MLA_NOPE)))
    wk = wk.reshape(KV_LORA, MLA_HEADS * HEAD_PAD)
    wv = jnp.pad(wkv[:, :, MLA_NOPE:], ((0, 0), (0, 0), (0, HEAD_PAD - MLA_V)))
    wv = wv.reshape(KV_LORA, MLA_HEADS * HEAD_PAD)
    return w_in_p, wq.astype(BF16), jnp.concatenate([wk, wv], axis=1).astype(BF16)


def kernel(x, positions, ln_in_g, ln_in_b, w_in, q_a_norm_g, w_q_b, kv_a_norm_g, w_kv_b, hgrn_lb_logits,
           mla_out_g, hgrn_out_g, w_o, ln1_g, ln1_b, w_router, b_router, w_up, b_up, w_down, b_down,
           ln2_g, ln2_b):
    B, S, D = x.shape
    assert D == D_MODEL and w_in.shape[0] == DEPTH and hgrn_lb_logits.shape[0] == DEPTH + 1
    T = B * S
    xt = x.reshape(T, D)
    r2 = lambda a: a.reshape(1, -1)

    cos, sin = _rope_cos_sin(positions)
    w_in_p, w_qb_p, w_kv_p = _prep_weights(w_in[0], w_q_b[0], w_kv_b[0])
    q, k, v, hq, hk, hg, hv, gate = _in_proj(
        xt, r2(ln_in_g), r2(ln_in_b), w_in_p, q_a_norm_g, w_qb_p, kv_a_norm_g, w_kv_p,
        hgrn_lb_logits, cos, sin)
    attn = _attention(q, k, v, B, S)
    hgo = _hgrn(hq, hk, hg, hv, gate, hgrn_out_g, B, S)
    h1, h1b, pos_t, gate_t, tile_cnt = _out_proj(
        xt, r2(ln_in_g), r2(ln_in_b), attn, hgo, mla_out_g, w_o[0].astype(BF16), ln1_g, ln1_b,
        w_router[0].T, b_router.reshape(N_EXPERTS, 1))

    n_tiles = T // TM_DISP
    n_rows = T * TOP_K + n_tiles * N_EXPERTS * (SUBLANES - 1) + N_EXPERTS * ROW_BLOCK
    n_rows = _ceil_to(n_rows, ROW_BLOCK)
    tc = _ceil_to(tile_cnt[:, :, 0].astype(I32), SUBLANES)
    cnt = jnp.sum(tc, axis=0)
    padded = _ceil_to(cnt, ROW_BLOCK)
    pad_start = jnp.cumsum(padded) - padded
    seg_n = tc.reshape(-1)
    seg_src = (jnp.cumsum(tc, axis=1) - tc).reshape(-1)
    seg_dst = (pad_start[None, :] + jnp.cumsum(tc, axis=0) - tc).reshape(-1)
    n_used = jnp.sum(padded, keepdims=True) // ROW_BLOCK

    tile_tot = jnp.sum(tc, axis=1)
    xs = _dispatch(seg_n, seg_src, seg_dst, tile_tot, padded - cnt, pad_start + cnt, n_used, pos_t, h1b,
                   n_rows)
    ys = _experts(pad_start // ROW_BLOCK, padded // ROW_BLOCK, n_used, xs,
                  w_up[0], b_up[0], w_down[0], b_down[0])
    out = _combine(seg_n, seg_src, seg_dst, tile_tot, pos_t.T, gate_t.T, h1, ln2_g, ln2_b, ys)
    return out.reshape(B, S, D)
```

```python
import numpy as np
import jax
import jax.numpy as jnp
from jax import lax
from jax.experimental import pallas as pl
from jax.experimental.pallas import tpu as pltpu

F32 = jnp.float32
BF16 = jnp.bfloat16
I32 = jnp.int32

D_MODEL = 1024
CHUNK = 64
MLA_HEADS = 8
MLA_NOPE = 64
MLA_ROPE = 32
MLA_V = 64
Q_LORA = 384
KV_LORA = 256
ROPE_THETA = 10000.0
HG_HEADS = 4
HG_DK = 128
HG_DV = 128
HG_W = HG_HEADS * HG_DK
N_EXPERTS = 32
TOP_K = 4
D_EXPERT = 1024
SWIGLU_LIMIT = 7.0
SWIGLU_ALPHA = 1.702
LN_EPS = 1e-5
RMS_EPS = 1e-6
DEPTH = 1
DEEPNORM_ALPHA = (2.0 * DEPTH) ** 0.25
LOG2E = 1.4426950408889634

LANES = 128
SUBLANES = 8
HEAD_PAD = 128
HALF_ROPE = MLA_ROPE // 2
KR_OFF = Q_LORA + KV_LORA
HQ_OFF = KR_OFF + HEAD_PAD
IN_COLS_PAD = HQ_OFF + 4 * HG_W

TM_IN = 512
TQ = 512
HG_ROWS = 256
TM_OUT = 512
TM_DISP = 256
ROW_BLOCK = 512
VMEM_LIMIT = 52 * 1024 * 1024


def _cparams(sem, vmem=VMEM_LIMIT):
    return pltpu.CompilerParams(dimension_semantics=sem, vmem_limit_bytes=vmem)


def _layer_norm(x, g, b):
    mu = jnp.mean(x, axis=-1, keepdims=True)
    xc = x - mu
    var = jnp.mean(xc * xc, axis=-1, keepdims=True)
    return xc * lax.rsqrt(var + LN_EPS) * g + b


def _rms_norm(x, g):
    return x * lax.rsqrt(jnp.mean(x * x, axis=-1, keepdims=True) + RMS_EPS) * g


def _split_bf16(a):
    hi = a.astype(BF16)
    return hi, (a - hi.astype(F32)).astype(BF16)


def _sigmoid_pair(x):
    e = jnp.exp(-jnp.abs(x))
    big = 1.0 / (1.0 + e)
    small = e * big
    pos = x >= 0
    return jnp.where(pos, big, small), jnp.where(pos, small, big)


def _cos_sin_body(ang_ref, cos_ref, sin_ref):
    a = ang_ref[...]
    cos_ref[...] = jnp.cos(a)
    sin_ref[...] = jnp.sin(a)


def _rope_cos_sin(positions):
    T = positions.size
    inv_freq = ROPE_THETA ** (-jnp.arange(0, MLA_ROPE, 2, dtype=F32) / MLA_ROPE)
    ang = (positions.astype(F32)[..., None] * inv_freq).reshape(T * HALF_ROPE // LANES, LANES)
    cos, sin = pl.pallas_call(
        _cos_sin_body,
        out_shape=(jax.ShapeDtypeStruct(ang.shape, F32),) * 2,
        name="rope_cos_sin",
    )(ang)
    return cos.reshape(T, HALF_ROPE), sin.reshape(T, HALF_ROPE)


def _rope_placement():
    pc = np.zeros((HALF_ROPE, HEAD_PAD), np.float32)
    ps = np.zeros((HALF_ROPE, 2 * HEAD_PAD), np.float32)
    for j in range(HALF_ROPE):
        pc[j, MLA_NOPE + j] = pc[j, MLA_NOPE + HALF_ROPE + j] = 1.0
        ps[j, MLA_NOPE + HALF_ROPE + j] = 1.0
        ps[j, HEAD_PAD + MLA_NOPE + j] = -1.0
    return jnp.asarray(pc, BF16), jnp.asarray(ps, BF16)


def _rope_tables(cos, sin, pc, ps):
    place = lambda a, p: sum(jnp.dot(part, p, preferred_element_type=F32) for part in _split_bf16(a))
    lane = lax.broadcasted_iota(I32, (1, HEAD_PAD), 1)
    c = place(cos, pc) + (lane < MLA_NOPE).astype(F32)
    s = place(sin, ps)
    return c, s[:, :HEAD_PAD], s[:, HEAD_PAD:]


def _rope(x, c, s1, s2, reps):
    width = reps * HEAD_PAD
    if reps > 1:
        c = jnp.concatenate([c] * reps, axis=1)
        s1 = jnp.concatenate([s1] * reps, axis=1)
        s2 = jnp.concatenate([s2] * reps, axis=1)
    return (x * c + pltpu.roll(x, HALF_ROPE, 1) * s1
            + pltpu.roll(x, width - HALF_ROPE, 1) * s2)


def _in_proj_body(x_ref, lng_ref, lnb_ref, win_ref, qag_ref, wqb_ref, kvg_ref, wkv_ref, lbl_ref,
                  cos_ref, sin_ref, pc_ref, ps_ref,
                  q_out, k_out, v_out, hq_out, hk_out, hg_out, hv_out, gate_out):
    h = _layer_norm(x_ref[...], lng_ref[...], lnb_ref[...])
    p = jnp.dot(h.astype(BF16), win_ref[...], preferred_element_type=F32)
    c, s1, s2 = _rope_tables(cos_ref[...], sin_ref[...], pc_ref[...], ps_ref[...])

    qn = _rms_norm(p[:, :Q_LORA], qag_ref[...]).astype(BF16)
    q = jnp.dot(qn, wqb_ref[...], preferred_element_type=F32)
    scale = (MLA_NOPE + MLA_ROPE) ** -0.5 * LOG2E
    q_out[...] = (_rope(q, c, s1, s2, MLA_HEADS) * scale).astype(BF16)
    kvn = _rms_norm(p[:, Q_LORA:KR_OFF], kvg_ref[...]).astype(BF16)
    kv = jnp.dot(kvn, wkv_ref[...], preferred_element_type=F32)
    kr = _rope(p[:, KR_OFF:HQ_OFF], c, s1, s2, 1)
    k_out[...] = (kv[:, :MLA_HEADS * HEAD_PAD] + jnp.concatenate([kr] * MLA_HEADS, axis=1)).astype(BF16)
    lane = lax.broadcasted_iota(I32, (1, MLA_HEADS * HEAD_PAD), 1)
    ones_lane = (lane % HEAD_PAD == MLA_V).astype(F32)
    v_out[...] = (kv[:, MLA_HEADS * HEAD_PAD:] + ones_lane).astype(BF16)

    lbl = lbl_ref[...]
    le = jnp.exp(lbl - jnp.max(lbl, axis=0, keepdims=True))
    lb = le[0:1] / jnp.sum(le, axis=0, keepdims=True)
    hq = p[:, HQ_OFF:HQ_OFF + HG_W]
    hf = p[:, HQ_OFF + HG_W:HQ_OFF + 2 * HG_W]
    hi = p[:, HQ_OFF + 2 * HG_W:HQ_OFF + 3 * HG_W]
    hg = p[:, HQ_OFF + 3 * HG_W:HQ_OFF + 4 * HG_W]
    hq_out[...] = (hq * _sigmoid_pair(hq)[0]).astype(BF16)
    sg, one_minus_sg = _sigmoid_pair(hf)
    hk_out[...] = ((1.0 - lb) * one_minus_sg).astype(BF16)
    hg_out[...] = jnp.log(lb + (1.0 - lb) * sg)
    hv_out[...] = hi.astype(BF16)
    gate_out[...] = (hg * _sigmoid_pair(hg)[0]).astype(BF16)


def _in_proj(xt, ln_g, ln_b, w_in_p, qa_g, w_qb_p, kv_g, w_kv_p, lb_logits, cos, sin):
    T = xt.shape[0]
    pc, ps = _rope_placement()
    row = lambda w: pl.BlockSpec((TM_IN, w), lambda i: (i, 0))
    full = lambda a: pl.BlockSpec(a.shape, lambda i: (0, 0))
    outs = [(MLA_HEADS * HEAD_PAD, BF16), (MLA_HEADS * HEAD_PAD, BF16), (MLA_HEADS * HEAD_PAD, BF16),
            (HG_W, BF16), (HG_W, BF16), (HG_W, F32), (HG_W, BF16), (HG_W, BF16)]
    return pl.pallas_call(
        _in_proj_body,
        grid=(T // TM_IN,),
        in_specs=[row(D_MODEL), full(ln_g), full(ln_b), full(w_in_p), full(qa_g), full(w_qb_p),
                  full(kv_g), full(w_kv_p), full(lb_logits), row(HALF_ROPE), row(HALF_ROPE),
                  full(pc), full(ps)],
        out_specs=[row(w) for w, _ in outs],
        out_shape=[jax.ShapeDtypeStruct((T, w), d) for w, d in outs],
        compiler_params=_cparams(("parallel",)),
        name="in_proj",
    )(xt, ln_g, ln_b, w_in_p, qa_g, w_qb_p, kv_g, w_kv_p, lb_logits, cos, sin, pc, ps)


def _attn_body(q_ref, k_ref, v_ref, o_ref):
    n_q = q_ref.shape[1] // TQ
    dn = (((1,), (1,)), ((), ()))
    row_c = lax.broadcasted_iota(I32, (TQ, TQ), 0) // CHUNK
    col_c = lax.broadcasted_iota(I32, (TQ, TQ), 1) // CHUNK
    diag_mask = col_c <= row_c
    heads = [slice(hh * HEAD_PAD, (hh + 1) * HEAD_PAD) for hh in range(2)]
    lane = lax.broadcasted_iota(I32, (TQ, 2 * MLA_V), 1)
    for qi in range(n_q):
        r0, kv = qi * TQ, (qi + 1) * TQ
        res = []
        for hs in heads:
            s = lax.dot_general(q_ref[0, r0:kv, hs], k_ref[0, :kv, hs], dn, preferred_element_type=F32)
            s_diag = jnp.where(diag_mask, s[:, r0:], -jnp.inf)
            m = jnp.max(s_diag, axis=-1, keepdims=True)
            if qi:
                m = jnp.maximum(m, jnp.max(s[:, :r0], axis=-1, keepdims=True))
                p = jnp.concatenate([jnp.exp2(s[:, :r0] - m), jnp.exp2(s_diag - m)], axis=1)
            else:
                p = jnp.exp2(s_diag - m)
            a = jnp.dot(p.astype(BF16), v_ref[0, :kv, hs], preferred_element_type=F32)
            res.append(a * (1.0 / a[:, MLA_V:MLA_V + 1]))
        o_ref[0, r0:kv, :] = jnp.where(lane < MLA_V, res[0], pltpu.roll(res[1], MLA_V, 1)).astype(BF16)


def _attention(q, k, v, B, S):
    w = MLA_HEADS * HEAD_PAD
    blk = pl.BlockSpec((1, S, 2 * HEAD_PAD), lambda b, h: (b, 0, h))
    out = pl.pallas_call(
        _attn_body,
        grid=(B, MLA_HEADS // 2),
        in_specs=[blk, blk, blk],
        out_specs=pl.BlockSpec((1, S, 2 * MLA_V), lambda b, h: (b, 0, h)),
        out_shape=jax.ShapeDtypeStruct((B, S, MLA_HEADS * MLA_V), BF16),
        compiler_params=_cparams(("parallel", "parallel")),
        name="mla_attention",
    )(q.reshape(B, S, w), k.reshape(B, S, w), v.reshape(B, S, w))
    return out.reshape(B * S, MLA_HEADS * MLA_V)


N_LEVELS = 6
Z_BLOCKS = N_LEVELS + 1


def _hgrn_constants():
    z = np.zeros((Z_BLOCKS, CHUNK, CHUNK), np.float32)
    masks = np.zeros((N_LEVELS, CHUNK, CHUNK), np.float32)
    idx = np.arange(CHUNK)
    for l in range(N_LEVELS):
        hs = CHUNK >> (l + 1)
        parent = idx // (2 * hs)
        right = (idx % (2 * hs)) >= hs
        mid = parent * 2 * hs + hs
        for t in range(CHUNK):
            z[l, t, :mid[t]] = 1.0
        masks[l] = (parent[:, None] == parent[None, :]) & right[:, None] & (~right)[None, :]
    z[N_LEVELS] = np.tril(np.ones((CHUNK, CHUNK), np.float32))
    z = z.reshape(Z_BLOCKS * CHUNK, CHUNK)
    n_c = HG_ROWS // CHUNK
    step_masks = np.zeros((N_LEVELS, HG_ROWS, HG_ROWS), np.float32)
    for c in range(n_c):
        step_masks[:, c * CHUNK:(c + 1) * CHUNK, c * CHUNK:(c + 1) * CHUNK] = masks
    return np.concatenate([z, z], axis=1), step_masks


def _hgrn_rows(r0, q_ref, k_ref, g_ref, v_ref, gate_ref, og_ref, z_ref, mask_ref, o_ref, state_ref):
    dn_t = (((1,), (1,)), ((), ()))
    n_c = HG_ROWS // CHUNK
    rr = slice(r0, r0 + HG_ROWS)
    eye = (lax.broadcasted_iota(I32, (HG_ROWS, HG_ROWS), 0)
           == lax.broadcasted_iota(I32, (HG_ROWS, HG_ROWS), 1)).astype(F32)
    sums = []
    for c in range(n_c):
        g2 = g_ref[0, r0 + c * CHUNK:r0 + (c + 1) * CHUNK, :] * LOG2E
        g_hi, g_lo = _split_bf16(g2)
        sums.append(jnp.dot(z_ref[...], jnp.concatenate([g_hi, g_lo], axis=0),
                            preferred_element_type=F32))
    level = lambda l, cols: jnp.concatenate([s[l * CHUNK:(l + 1) * CHUNK, cols] for s in sums], axis=0)
    for h in range(HG_HEADS):
        cols = slice(h * HG_DK, (h + 1) * HG_DK)
        b = level(N_LEVELS, cols)
        q = q_ref[0, rr, cols].astype(F32)
        k = k_ref[0, rr, cols].astype(F32)
        v = v_ref[0, rr, cols]
        a = eye * jnp.sum(q * k, axis=-1, keepdims=True)
        for l in range(N_LEVELS):
            d = jnp.exp2(-jnp.abs(b - level(l, cols)))
            a = a + mask_ref[l] * lax.dot_general((q * d).astype(BF16), (k * d).astype(BF16), dn_t,
                                                  preferred_element_type=F32)
        o_intra = jnp.dot(a.astype(BF16), v, preferred_element_type=F32)
        d_in = jnp.exp2(b)
        qd = (q * d_in).astype(BF16)
        st = state_ref[h]
        o_parts = []
        for c in range(n_c):
            rows = slice(c * CHUNK, (c + 1) * CHUNK)
            last = slice((c + 1) * CHUNK - 1, (c + 1) * CHUNK)
            o_parts.append(o_intra[rows] + lax.dot_general(qd[rows], st.astype(BF16), dn_t,
                                                           preferred_element_type=F32))
            kh = (k[rows] * jnp.exp2(b[last] - b[rows])).astype(BF16)
            upd = lax.dot_general(v[rows], kh, (((0,), (0,)), ((), ())), preferred_element_type=F32)
            st = st * d_in[last] + upd
        state_ref[h] = st
        o = jnp.concatenate(o_parts, axis=0)
        o = _rms_norm(o, og_ref[:, cols]) * gate_ref[0, rr, cols].astype(F32)
        o_ref[0, rr, cols] = o.astype(BF16)


def _hgrn_body(q_ref, k_ref, g_ref, v_ref, gate_ref, og_ref, z_ref, mask_ref, o_ref, state_ref):
    @pl.when(pl.program_id(1) == 0)
    def _():
        state_ref[...] = jnp.zeros_like(state_ref)

    _hgrn_rows(0, q_ref, k_ref, g_ref, v_ref, gate_ref, og_ref, z_ref, mask_ref, o_ref, state_ref)


def _hgrn(hq, hk, hg, hv, gate, out_g, B, S):
    zc, masks = _hgrn_constants()
    zc = jnp.asarray(zc, BF16)
    masks = jnp.asarray(masks, F32)
    blk = pl.BlockSpec((1, HG_ROWS, HG_W), lambda b, i: (b, i, 0))
    r3 = lambda a: a.reshape(B, S, HG_W)
    out = pl.pallas_call(
        _hgrn_body,
        grid=(B, S // HG_ROWS),
        in_specs=[blk, blk, blk, blk, blk,
                  pl.BlockSpec((1, HG_W), lambda b, i: (0, 0)),
                  pl.BlockSpec(zc.shape, lambda b, i: (0, 0)),
                  pl.BlockSpec(masks.shape, lambda b, i: (0, 0, 0))],
        out_specs=blk,
        out_shape=jax.ShapeDtypeStruct((B, S, HG_W), BF16),
        scratch_shapes=[pltpu.VMEM((HG_HEADS, HG_DV, HG_DK), F32)],
        compiler_params=_cparams(("parallel", "arbitrary")),
        name="hgrn2",
    )(r3(hq), r3(hk), r3(hg), r3(hv), r3(gate), out_g, zc, masks)
    return out.reshape(B * S, HG_W)


def _pack_bf16_pair(lo, hi):
    lo_b = pltpu.bitcast(lo.astype(BF16).astype(F32), I32)
    hi_b = pltpu.bitcast(hi.astype(BF16).astype(F32), I32)
    return hi_b | lax.shift_right_logical(lo_b, 16)


def _unpack_bf16_pair(w):
    lo = pltpu.bitcast(lax.shift_left(w, 16), F32)
    hi = pltpu.bitcast(w & jnp.int32(-65536), F32)
    return lo, hi


def _out_proj_body(x_ref, lng_ref, lnb_ref, attn_ref, hgo_ref, mg_ref, wo_ref, l1g_ref, l1b_ref,
                   wr_ref, br_ref,
                   h1_out, h1b_out, pos_out, gate_out, cnt_out):
    for sub_tile in range(TM_OUT // TM_DISP):
        rs = slice(sub_tile * TM_DISP, (sub_tile + 1) * TM_DISP)
        _out_proj_tile(x_ref[rs, :], lng_ref, lnb_ref, attn_ref[rs, :], hgo_ref[rs, :], mg_ref, wo_ref,
                       l1g_ref, l1b_ref, wr_ref, br_ref,
                       h1_out.at[rs, :], h1b_out.at[rs, :], pos_out.at[:, rs], gate_out.at[:, rs],
                       cnt_out.at[sub_tile])


def _out_proj_tile(x, lng_ref, lnb_ref, attn, hgo, mg_ref, wo_ref, l1g_ref, l1b_ref, wr_ref, br_ref,
                   h1_out, h1b_out, pos_out, gate_out, cnt_out):
    h = _layer_norm(x, lng_ref[...], lnb_ref[...])
    attn = _rms_norm(attn.astype(F32), mg_ref[...]).astype(BF16)
    mix_in = jnp.concatenate([attn, hgo], axis=1)
    mix = jnp.dot(mix_in, wo_ref[...], preferred_element_type=F32)
    h1 = _layer_norm(DEEPNORM_ALPHA * h + mix, l1g_ref[...], l1b_ref[...])
    h1_out[...] = h1
    h1_hi, h1_lo = _split_bf16(h1)
    h1b_out[...] = h1_hi

    w_hi, w_lo = _split_bf16(wr_ref[...])
    nt_dims = (((1,), (1,)), ((), ()))
    logits_t = (lax.dot_general(w_hi, h1_hi, nt_dims, preferred_element_type=F32)
                + lax.dot_general(w_hi, h1_lo, nt_dims, preferred_element_type=F32)
                + lax.dot_general(w_lo, h1_hi, nt_dims, preferred_element_type=F32)) + br_ref[...]
    tm = h1.shape[0]
    sub = lax.broadcasted_iota(I32, (N_EXPERTS, tm), 0)
    work = logits_t
    vals, idxs = [], []
    for _ in range(TOP_K):
        mx = jnp.max(work, axis=0, keepdims=True)
        ix = jnp.min(jnp.where(work == mx, sub, N_EXPERTS), axis=0, keepdims=True)
        vals.append(mx)
        idxs.append(ix)
        work = jnp.where(sub == ix, -jnp.inf, work)
    exps = [jnp.exp(vl - vals[0]) for vl in vals]
    den = exps[0] + exps[1] + exps[2] + exps[3]

    sel = jnp.zeros((N_EXPERTS, tm), F32)
    for ix in idxs:
        sel = sel + (sub == ix).astype(F32)
    selb = sel.astype(BF16)
    earlier = (lax.broadcasted_iota(I32, (tm, tm), 0)
               < lax.broadcasted_iota(I32, (tm, tm), 1)).astype(BF16)
    lower = (lax.broadcasted_iota(I32, (N_EXPERTS, N_EXPERTS), 1)
             < lax.broadcasted_iota(I32, (N_EXPERTS, N_EXPERTS), 0)).astype(BF16)
    cnt = jnp.sum(sel, axis=1, keepdims=True)
    run = jnp.floor((cnt + (SUBLANES - 1)) * (1.0 / SUBLANES)) * SUBLANES
    run_b = jnp.broadcast_to(run, (N_EXPERTS, LANES)).astype(BF16)
    seg_off = jnp.dot(lower, run_b, preferred_element_type=F32)[:, 0:1]
    posmat = jnp.dot(selb, earlier, preferred_element_type=F32) + seg_off
    pos = [jnp.sum(jnp.where(sub == ix, posmat, 0.0), axis=0, keepdims=True) for ix in idxs]
    pos_out[...] = jnp.concatenate(pos, axis=0).astype(I32)
    gsel = jnp.zeros((N_EXPERTS, tm), F32)
    for ix, ex in zip(idxs, exps):
        gsel = gsel + jnp.where(sub == ix, ex / den, 0.0)
    gate_out[...] = gsel
    cnt_out[...] = cnt


def _out_proj(xt, ln_g, ln_b, attn, hgo, mla_g, w_o, l1g, l1b, w_r, b_r):
    T = xt.shape[0]
    row = lambda w: pl.BlockSpec((TM_OUT, w), lambda i: (i, 0))
    full = lambda a: pl.BlockSpec(a.shape, lambda i: (0, 0))
    col = pl.BlockSpec((TOP_K, TM_OUT), lambda i: (0, i))
    sub_tiles = TM_OUT // TM_DISP
    return pl.pallas_call(
        _out_proj_body,
        grid=(T // TM_OUT,),
        in_specs=[row(D_MODEL), full(ln_g), full(ln_b), row(MLA_HEADS * MLA_V), row(HG_W), full(mla_g),
                  full(w_o), full(l1g), full(l1b), full(w_r), full(b_r)],
        out_specs=[row(D_MODEL), row(D_MODEL), col, pl.BlockSpec((N_EXPERTS, TM_OUT), lambda i: (0, i)),
                   pl.BlockSpec((sub_tiles, N_EXPERTS, 1), lambda i: (i, 0, 0))],
        out_shape=[jax.ShapeDtypeStruct((T, D_MODEL), F32), jax.ShapeDtypeStruct((T, D_MODEL), BF16),
                   jax.ShapeDtypeStruct((TOP_K, T), I32), jax.ShapeDtypeStruct((N_EXPERTS, T), F32),
                   jax.ShapeDtypeStruct((T // TM_DISP, N_EXPERTS, 1), F32)],
        compiler_params=_cparams(("parallel",)),
        name="out_proj_router",
    )(xt, ln_g, ln_b, attn, hgo, mla_g, w_o, l1g, l1b, w_r, b_r)


TILE_SLOTS = TOP_K * TM_DISP + N_EXPERTS * SUBLANES
ROW_WORDS = D_MODEL // 2 + LANES
TILES_PER_STEP = 2
TM_STEP = TILES_PER_STEP * TM_DISP
RUN_UNROLL = 4
RARE_RUN = 2 * TM_DISP * TOP_K // N_EXPERTS


def _ceil_to(n, m):
    return (n + m - 1) // m * m


def _rows(start, size):
    return pl.ds(pl.multiple_of(start, SUBLANES), size)


def _for_each_piece(n, fn, max_rows=TM_DISP, rare_from=None):
    def pieces(kbits):
        for kbit in kbits:
            size = 1 << kbit

            @pl.when((n & size) != 0)
            def _():
                fn(pl.multiple_of((n >> (kbit + 1)) << (kbit + 1), SUBLANES), size)

    kbits = list(reversed(range(SUBLANES.bit_length() - 1, max_rows.bit_length())))
    if rare_from is None:
        pieces(kbits)
    else:
        pl.when(n >= rare_from)(lambda: pieces([k for k in kbits if (1 << k) >= rare_from]))
        pieces([k for k in kbits if (1 << k) < rare_from])


def _pack_exact_pair(lo, hi):
    return pltpu.bitcast(hi, I32) | lax.shift_right_logical(pltpu.bitcast(lo, I32), 16)


def _zero_tail_blocks(zeros_ref, out_hbm, n_used, zsem):
    def blk_copy(g):
        return pltpu.make_async_copy(zeros_ref, out_hbm.at[_rows(g * ROW_BLOCK, ROW_BLOCK)], zsem)

    n_total = out_hbm.shape[0] // ROW_BLOCK
    lax.fori_loop(n_used, n_total, lambda g, c: (blk_copy(g).start(), c)[1], 0)
    lax.fori_loop(n_used, n_total, lambda g, c: (blk_copy(g).wait(), c)[1], 0)


def _dispatch_body(segn_ref, segsrc_ref, segdst_ref, tot_ref, padn_ref, padd_ref, nused_ref, pos_ref, xb_ref, gx_ref,
                   xs_out, stage, zeros_ref, sem, zsem):
    i = pl.program_id(0)

    @pl.when(i == 0)
    def _():
        zeros_ref[...] = jnp.zeros_like(zeros_ref)

        def pad_copy(e, off, size):
            return pltpu.make_async_copy(zeros_ref.at[pl.ds(0, size)],
                                         xs_out.at[_rows(padd_ref[e] + off, size)], zsem)

        def fill(e, c):
            _for_each_piece(padn_ref[e], lambda off, size: pad_copy(e, off, size).start())
            return c

        def drain(e, c):
            _for_each_piece(padn_ref[e], lambda off, size: pad_copy(e, off, size).wait())
            return c

        lax.fori_loop(0, N_EXPERTS, fill, 0)
        lax.fori_loop(0, N_EXPERTS, drain, 0)
        _zero_tail_blocks(zeros_ref, xs_out, nused_ref[0], zsem)

    def settle(tile, b):
        _for_each_piece(tot_ref[tile], lambda off, size: pltpu.make_async_copy(
            stage.at[b, pl.ds(0, size)], xs_out.at[pl.ds(0, size)], sem.at[b]).wait(), TILE_SLOTS)

    for b in range(TILES_PER_STEP):
        tile = i * TILES_PER_STEP + b
        ts = slice(b * TM_DISP, (b + 1) * TM_DISP)
        pos = pos_ref[:, ts]
        slot = lax.broadcasted_iota(I32, (TILE_SLOTS, TM_DISP), 0)
        sel = jnp.zeros((TILE_SLOTS, TM_DISP), F32)
        for j in range(TOP_K):
            sel = sel + (slot == pos[j:j + 1, :]).astype(F32)
        src_rows = jnp.concatenate([xb_ref[ts, :], gx_ref[ts, :]], axis=1)
        rows = jnp.dot(sel.astype(BF16), src_rows, preferred_element_type=F32)
        packed = jnp.concatenate(
            [_pack_exact_pair(rows[:, :D_MODEL // 2], rows[:, D_MODEL // 2:D_MODEL]),
             pltpu.bitcast(rows[:, D_MODEL:], I32)], axis=1)

        @pl.when(i >= 1)
        def _():
            settle(tile - TILES_PER_STEP, b)

        stage[b] = packed

        def send(eb, c):
            for u in range(RUN_UNROLL):
                k = tile * N_EXPERTS + eb * RUN_UNROLL + u
                src, dst = segsrc_ref[k], segdst_ref[k]
                _for_each_piece(segn_ref[k], lambda off, size: pltpu.make_async_copy(
                    stage.at[b, _rows(src + off, size)], xs_out.at[_rows(dst + off, size)],
                    sem.at[b]).start(), rare_from=RARE_RUN)
            return c

        lax.fori_loop(0, N_EXPERTS // RUN_UNROLL, send, 0)

    @pl.when(i == pl.num_programs(0) - 1)
    def _():
        for b in range(TILES_PER_STEP):
            settle(i * TILES_PER_STEP + b, b)


def _dispatch(seg_n, seg_src, seg_dst, tile_tot, pad_n, pad_dst, n_used, pos_t, h1b, gates_x, n_rows):
    T = h1b.shape[0]
    grid_spec = pltpu.PrefetchScalarGridSpec(
        num_scalar_prefetch=7,
        grid=(T // TM_STEP,),
        in_specs=[pl.BlockSpec((TOP_K, TM_STEP), lambda i, *_: (0, i)),
                  pl.BlockSpec((TM_STEP, D_MODEL), lambda i, *_: (i, 0)),
                  pl.BlockSpec((TM_STEP, LANES), lambda i, *_: (i, 0))],
        out_specs=pl.BlockSpec(memory_space=pl.ANY),
        scratch_shapes=[pltpu.VMEM((TILES_PER_STEP, TILE_SLOTS, ROW_WORDS), I32),
                        pltpu.VMEM((ROW_BLOCK, ROW_WORDS), I32),
                        pltpu.SemaphoreType.DMA((TILES_PER_STEP,)), pltpu.SemaphoreType.DMA(())],
    )
    return pl.pallas_call(
        _dispatch_body,
        grid_spec=grid_spec,
        out_shape=jax.ShapeDtypeStruct((n_rows, ROW_WORDS), I32),
        compiler_params=_cparams(("arbitrary",)),
        name="moe_dispatch",
    )(seg_n, seg_src, seg_dst, tile_tot, pad_n, pad_dst, n_used, pos_t, h1b, gates_x)


def _experts_body(bstart_ref, bcount_ref, nused_ref, xs_hbm, wu_ref, bu_ref, wd_ref, bd_ref, ys_hbm,
                  wu_bf, wd_bf, xbuf, ybuf, sem_in, sem_out):
    e = pl.program_id(0)
    n_used = nused_ref[0]
    nb = bcount_ref[e]

    def fetch(g, slot):
        rows = pl.ds(pl.multiple_of(g * ROW_BLOCK, ROW_BLOCK), ROW_BLOCK)
        return pltpu.make_async_copy(xs_hbm.at[rows], xbuf.at[slot], sem_in.at[slot])

    def put(g, slot):
        rows = pl.ds(pl.multiple_of(g * ROW_BLOCK, ROW_BLOCK), ROW_BLOCK)
        return pltpu.make_async_copy(ybuf.at[slot], ys_hbm.at[rows], sem_out.at[slot])

    @pl.when(jnp.logical_and(e == 0, n_used > 0))
    def _():
        fetch(0, 0).start()

    @pl.when(nb > 0)
    def _():
        wu_bf[...] = wu_ref[0].astype(BF16)
        wd_bf[...] = wd_ref[0].astype(BF16)

    def block(b, c):
        g = bstart_ref[e] + b
        slot = lax.rem(g, 2)
        fetch(g, slot).wait()

        @pl.when(g + 1 < n_used)
        def _():
            fetch(g + 1, 1 - slot).start()

        row = xbuf[slot]
        lo, hi = _unpack_bf16_pair(row[:, :D_MODEL // 2])
        x = jnp.concatenate([lo, hi], axis=1).astype(BF16)
        lane = lax.broadcasted_iota(I32, (ROW_BLOCK, LANES), 1)
        mine = jnp.logical_or(lane == e, lane == e + N_EXPERTS)
        gate = jnp.sum(jnp.where(mine, pltpu.bitcast(row[:, D_MODEL // 2:], F32), 0.0),
                       axis=1, keepdims=True)
        hb = jnp.dot(x, wu_bf[...], preferred_element_type=F32) + bu_ref[0]
        glu = jnp.minimum(hb[:, :D_EXPERT], SWIGLU_LIMIT)
        lin = jnp.clip(hb[:, D_EXPERT:], -SWIGLU_LIMIT, SWIGLU_LIMIT)
        act = glu * (1.0 / (1.0 + jnp.exp(-SWIGLU_ALPHA * glu))) * (lin + 1.0)
        y = (jnp.dot(act.astype(BF16), wd_bf[...], preferred_element_type=F32) + bd_ref[0]) * gate

        @pl.when(g >= 2)
        def _():
            put(g - 2, slot).wait()

        ybuf[slot] = _pack_bf16_pair(y[:, :D_MODEL // 2], y[:, D_MODEL // 2:])
        put(g, slot).start()
        return c

    lax.fori_loop(0, nb, block, 0)

    @pl.when(e == N_EXPERTS - 1)
    def _():
        for back in (2, 1):
            @pl.when(n_used >= back)
            def _():
                put(n_used - back, lax.rem(n_used - back, 2)).wait()
        ybuf[0] = jnp.zeros_like(ybuf[0])
        _zero_tail_blocks(ybuf.at[0], ys_hbm, n_used, sem_out.at[0])


def _experts(block_start, block_count, n_used, xs, w_up, b_up, w_down, b_down):
    n_rows = xs.shape[0]
    grid_spec = pltpu.PrefetchScalarGridSpec(
        num_scalar_prefetch=3,
        grid=(N_EXPERTS,),
        in_specs=[pl.BlockSpec(memory_space=pl.ANY),
                  pl.BlockSpec((1, D_MODEL, 2 * D_EXPERT), lambda e, *_: (e, 0, 0)),
                  pl.BlockSpec((1, 1, 2 * D_EXPERT), lambda e, *_: (e, 0, 0)),
                  pl.BlockSpec((1, D_EXPERT, D_MODEL), lambda e, *_: (e, 0, 0)),
                  pl.BlockSpec((1, 1, D_MODEL), lambda e, *_: (e, 0, 0))],
        out_specs=pl.BlockSpec(memory_space=pl.ANY),
        scratch_shapes=[pltpu.VMEM((D_MODEL, 2 * D_EXPERT), BF16), pltpu.VMEM((D_EXPERT, D_MODEL), BF16),
                        pltpu.VMEM((2, ROW_BLOCK, ROW_WORDS), I32),
                        pltpu.VMEM((2, ROW_BLOCK, D_MODEL // 2), I32),
                        pltpu.SemaphoreType.DMA((2,)), pltpu.SemaphoreType.DMA((2,))],
    )
    return pl.pallas_call(
        _experts_body,
        grid_spec=grid_spec,
        out_shape=jax.ShapeDtypeStruct((n_rows, D_MODEL // 2), I32),
        compiler_params=_cparams(("arbitrary",)),
        name="moe_experts",
    )(block_start, block_count, n_used, xs, w_up, b_up.reshape(N_EXPERTS, 1, 2 * D_EXPERT), w_down,
      b_down.reshape(N_EXPERTS, 1, D_MODEL))


def _combine_body(segn_ref, segsrc_ref, segdst_ref, tot_ref, pos_ref, h1_ref, l2g_ref, l2b_ref, ys_hbm,
                  o_ref, stage, sem):
    i = pl.program_id(0)
    n_tiles = pl.num_programs(0) * TILES_PER_STEP

    def fetch_tile(tile, b):
        def fetch(eb, c):
            for u in range(RUN_UNROLL):
                k = tile * N_EXPERTS + eb * RUN_UNROLL + u
                src, dst = segsrc_ref[k], segdst_ref[k]
                _for_each_piece(segn_ref[k], lambda off, size: pltpu.make_async_copy(
                    ys_hbm.at[_rows(dst + off, size)], stage.at[b, _rows(src + off, size)],
                    sem.at[b]).start(), rare_from=RARE_RUN)
            return c

        lax.fori_loop(0, N_EXPERTS // RUN_UNROLL, fetch, 0)

    @pl.when(i == 0)
    def _():
        stage[...] = jnp.zeros_like(stage)
        fetch_tile(0, 0)

    for b in range(TILES_PER_STEP):
        tile = i * TILES_PER_STEP + b
        nxt = (b + 1) % TILES_PER_STEP

        @pl.when(tile + 1 < n_tiles)
        def _():
            fetch_tile(tile + 1, nxt)

        ts = slice(b * TM_DISP, (b + 1) * TM_DISP)
        pos = pos_ref[ts, :]
        slot = lax.broadcasted_iota(I32, (TM_DISP, TILE_SLOTS), 1)
        w = jnp.zeros((TM_DISP, TILE_SLOTS), F32)
        for j in range(TOP_K):
            w = w + (slot == pos[:, j:j + 1]).astype(F32)
        w = w.astype(BF16)
        _for_each_piece(tot_ref[tile], lambda off, size: pltpu.make_async_copy(
            ys_hbm.at[pl.ds(0, size)], stage.at[b, pl.ds(0, size)], sem.at[b]).wait(), TILE_SLOTS)
        lo, hi = _unpack_bf16_pair(stage[b])
        y = jnp.concatenate([lo, hi], axis=1).astype(BF16)
        ffn = jnp.dot(w, y, preferred_element_type=F32)
        o_ref[ts, :] = _layer_norm(DEEPNORM_ALPHA * h1_ref[ts, :] + ffn, l2g_ref[...], l2b_ref[...])


def _combine(seg_n, seg_src, seg_dst, tile_tot, pos, h1, l2g, l2b, ys):
    T = h1.shape[0]
    grid_spec = pltpu.PrefetchScalarGridSpec(
        num_scalar_prefetch=4,
        grid=(T // TM_STEP,),
        in_specs=[pl.BlockSpec((TM_STEP, TOP_K), lambda i, *_: (i, 0)),
                  pl.BlockSpec((TM_STEP, D_MODEL), lambda i, *_: (i, 0)),
                  pl.BlockSpec(l2g.shape, lambda i, *_: (0, 0)),
                  pl.BlockSpec(l2b.shape, lambda i, *_: (0, 0)),
                  pl.BlockSpec(memory_space=pl.ANY)],
        out_specs=pl.BlockSpec((TM_STEP, D_MODEL), lambda i, *_: (i, 0)),
        scratch_shapes=[pltpu.VMEM((TILES_PER_STEP, TILE_SLOTS, D_MODEL // 2), I32),
                        pltpu.SemaphoreType.DMA((TILES_PER_STEP,))],
    )
    return pl.pallas_call(
        _combine_body,
        grid_spec=grid_spec,
        out_shape=jax.ShapeDtypeStruct((T, D_MODEL), F32),
        compiler_params=_cparams(("arbitrary",)),
        name="moe_combine",
    )(seg_n, seg_src, seg_dst, tile_tot, pos, h1, l2g, l2b, ys)


def _prep_weights(w_in, w_q_b, w_kv_b):
    zc = lambda n: jnp.zeros((D_MODEL, n), F32)
    w_in_p = jnp.concatenate(
        [w_in[:, :KR_OFF], zc(MLA_NOPE), w_in[:, KR_OFF:KR_OFF + MLA_ROPE],
         zc(HEAD_PAD - MLA_NOPE - MLA_ROPE), w_in[:, KR_OFF + MLA_ROPE:]], axis=1).astype(BF16)
    qd = MLA_NOPE + MLA_ROPE
    wq = w_q_b.reshape(Q_LORA, MLA_HEADS, qd)
    wq = jnp.pad(wq, ((0, 0), (0, 0), (0, HEAD_PAD - qd))).reshape(Q_LORA, MLA_HEADS * HEAD_PAD)
    wkv = w_kv_b.reshape(KV_LORA, MLA_HEADS, MLA_NOPE + MLA_V)
    wk = jnp.pad(wkv[:, :, :MLA_NOPE], ((0, 0), (0, 0), (0, HEAD_PAD - MLA_NOPE)))
    wk = wk.reshape(KV_LORA, MLA_HEADS * HEAD_PAD)
    wv = jnp.pad(wkv[:, :, MLA_NOPE:], ((0, 0), (0, 0), (0, HEAD_PAD - MLA_V)))
    wv = wv.reshape(KV_LORA, MLA_HEADS * HEAD_PAD)
    return w_in_p, wq.astype(BF16), jnp.concatenate([wk, wv], axis=1).astype(BF16)


def kernel(x, positions, ln_in_g, ln_in_b, w_in, q_a_norm_g, w_q_b, kv_a_norm_g, w_kv_b, hgrn_lb_logits,
           mla_out_g, hgrn_out_g, w_o, ln1_g, ln1_b, w_router, b_router, w_up, b_up, w_down, b_down,
           ln2_g, ln2_b):
    B, S, D = x.shape
    assert D == D_MODEL and w_in.shape[0] == DEPTH and hgrn_lb_logits.shape[0] == DEPTH + 1
    T = B * S
    xt = x.reshape(T, D)
    r2 = lambda a: a.reshape(1, -1)

    cos, sin = _rope_cos_sin(positions)
    w_in_p, w_qb_p, w_kv_p = _prep_weights(w_in[0], w_q_b[0], w_kv_b[0])
    q, k, v, hq, hk, hg, hv, gate = _in_proj(
        xt, r2(ln_in_g), r2(ln_in_b), w_in_p, q_a_norm_g, w_qb_p, kv_a_norm_g, w_kv_p,
        hgrn_lb_logits, cos, sin)
    attn = _attention(q, k, v, B, S)
    hgo = _hgrn(hq, hk, hg, hv, gate, hgrn_out_g, B, S)
    h1, h1b, pos_t, gate_t, tile_cnt = _out_proj(
        xt, r2(ln_in_g), r2(ln_in_b), attn, hgo, mla_out_g, w_o[0].astype(BF16), ln1_g, ln1_b,
        w_router[0].T, b_router.reshape(N_EXPERTS, 1))

    n_tiles = T // TM_DISP
    n_rows = T * TOP_K + n_tiles * N_EXPERTS * (SUBLANES - 1) + N_EXPERTS * ROW_BLOCK
    n_rows = _ceil_to(n_rows, ROW_BLOCK)
    tc = _ceil_to(tile_cnt[:, :, 0].astype(I32), SUBLANES)
    cnt = jnp.sum(tc, axis=0)
    padded = _ceil_to(cnt, ROW_BLOCK)
    pad_start = jnp.cumsum(padded) - padded
    seg_n = tc.reshape(-1)
    seg_src = (jnp.cumsum(tc, axis=1) - tc).reshape(-1)
    seg_dst = (pad_start[None, :] + jnp.cumsum(tc, axis=0) - tc).reshape(-1)
    n_used = jnp.sum(padded, keepdims=True) // ROW_BLOCK

    tile_tot = jnp.sum(tc, axis=1)
    g_hi, g_lo = _split_bf16(gate_t.T)
    gates_x = jnp.concatenate([g_hi, g_lo, jnp.zeros((T, LANES - 2 * N_EXPERTS), BF16)], axis=1)
    xs = _dispatch(seg_n, seg_src, seg_dst, tile_tot, padded - cnt, pad_start + cnt, n_used, pos_t, h1b,
                   gates_x, n_rows)
    ys = _experts(pad_start // ROW_BLOCK, padded // ROW_BLOCK, n_used, xs,
                  w_up[0], b_up[0], w_down[0], b_down[0])
    out = _combine(seg_n, seg_src, seg_dst, tile_tot, pos_t.T, h1, ln2_g, ln2_b, ys)
    return out.reshape(B, S, D)
```

```python
import numpy as np
import jax
import jax.numpy as jnp
from jax import lax
from jax.experimental import pallas as pl
from jax.experimental.pallas import tpu as pltpu

F32 = jnp.float32
BF16 = jnp.bfloat16
I32 = jnp.int32

D_MODEL = 1024
CHUNK = 64
MLA_HEADS = 8
MLA_NOPE = 64
MLA_ROPE = 32
MLA_V = 64
Q_LORA = 384
KV_LORA = 256
ROPE_THETA = 10000.0
HG_HEADS = 4
HG_DK = 128
HG_DV = 128
HG_W = HG_HEADS * HG_DK
N_EXPERTS = 32
TOP_K = 4
D_EXPERT = 1024
SWIGLU_LIMIT = 7.0
SWIGLU_ALPHA = 1.702
LN_EPS = 1e-5
RMS_EPS = 1e-6
DEPTH = 1
DEEPNORM_ALPHA = (2.0 * DEPTH) ** 0.25
LOG2E = 1.4426950408889634

LANES = 128
SUBLANES = 8
HEAD_PAD = 128
HALF_ROPE = MLA_ROPE // 2
KR_OFF = Q_LORA + KV_LORA
HQ_OFF = KR_OFF + HEAD_PAD
IN_COLS_PAD = HQ_OFF + 4 * HG_W

TM_IN = 512
TQ = 512
HG_ROWS = 256
TM_OUT = 512
TM_DISP = 256
ROW_BLOCK = 512
ROW_UNIT = ROW_BLOCK // 2
VMEM_LIMIT = 52 * 1024 * 1024


def _cparams(sem, vmem=VMEM_LIMIT):
    return pltpu.CompilerParams(dimension_semantics=sem, vmem_limit_bytes=vmem)


def _layer_norm(x, g, b):
    mu = jnp.mean(x, axis=-1, keepdims=True)
    xc = x - mu
    var = jnp.mean(xc * xc, axis=-1, keepdims=True)
    return xc * lax.rsqrt(var + LN_EPS) * g + b


def _rms_norm(x, g):
    return x * lax.rsqrt(jnp.mean(x * x, axis=-1, keepdims=True) + RMS_EPS) * g


def _split_bf16(a):
    hi = a.astype(BF16)
    return hi, (a - hi.astype(F32)).astype(BF16)


def _sigmoid_pair(x):
    e = jnp.exp(-jnp.abs(x))
    big = 1.0 / (1.0 + e)
    small = e * big
    pos = x >= 0
    return jnp.where(pos, big, small), jnp.where(pos, small, big)


def _cos_sin_body(ang_ref, cos_ref, sin_ref):
    a = ang_ref[...]
    cos_ref[...] = jnp.cos(a)
    sin_ref[...] = jnp.sin(a)


def _rope_cos_sin(positions):
    T = positions.size
    inv_freq = ROPE_THETA ** (-jnp.arange(0, MLA_ROPE, 2, dtype=F32) / MLA_ROPE)
    ang = (positions.astype(F32)[..., None] * inv_freq).reshape(T * HALF_ROPE // LANES, LANES)
    cos, sin = pl.pallas_call(
        _cos_sin_body,
        out_shape=(jax.ShapeDtypeStruct(ang.shape, F32),) * 2,
        name="rope_cos_sin",
    )(ang)
    return cos.reshape(T, HALF_ROPE), sin.reshape(T, HALF_ROPE)


def _rope_placement():
    pc = np.zeros((HALF_ROPE, HEAD_PAD), np.float32)
    ps = np.zeros((HALF_ROPE, 2 * HEAD_PAD), np.float32)
    for j in range(HALF_ROPE):
        pc[j, MLA_NOPE + j] = pc[j, MLA_NOPE + HALF_ROPE + j] = 1.0
        ps[j, MLA_NOPE + HALF_ROPE + j] = 1.0
        ps[j, HEAD_PAD + MLA_NOPE + j] = -1.0
    return jnp.asarray(pc, BF16), jnp.asarray(ps, BF16)


def _rope_tables(cos, sin, pc, ps):
    place = lambda a, p: sum(jnp.dot(part, p, preferred_element_type=F32) for part in _split_bf16(a))
    lane = lax.broadcasted_iota(I32, (1, HEAD_PAD), 1)
    c = place(cos, pc) + (lane < MLA_NOPE).astype(F32)
    s = place(sin, ps)
    return c, s[:, :HEAD_PAD], s[:, HEAD_PAD:]


def _rope(x, c, s1, s2, reps):
    width = reps * HEAD_PAD
    if reps > 1:
        c = jnp.concatenate([c] * reps, axis=1)
        s1 = jnp.concatenate([s1] * reps, axis=1)
        s2 = jnp.concatenate([s2] * reps, axis=1)
    return (x * c + pltpu.roll(x, HALF_ROPE, 1) * s1
            + pltpu.roll(x, width - HALF_ROPE, 1) * s2)


def _in_proj_body(x_ref, lng_ref, lnb_ref, win_ref, qag_ref, wqb_ref, kvg_ref, wkv_ref, lbl_ref,
                  cos_ref, sin_ref, pc_ref, ps_ref,
                  q_out, k_out, v_out, hq_out, hk_out, hg_out, hv_out, gate_out):
    h = _layer_norm(x_ref[...], lng_ref[...], lnb_ref[...])
    p = jnp.dot(h.astype(BF16), win_ref[...], preferred_element_type=F32)
    c, s1, s2 = _rope_tables(cos_ref[...], sin_ref[...], pc_ref[...], ps_ref[...])

    qn = _rms_norm(p[:, :Q_LORA], qag_ref[...]).astype(BF16)
    q = jnp.dot(qn, wqb_ref[...], preferred_element_type=F32)
    scale = (MLA_NOPE + MLA_ROPE) ** -0.5 * LOG2E
    q_out[...] = (_rope(q, c, s1, s2, MLA_HEADS) * scale).astype(BF16)
    kvn = _rms_norm(p[:, Q_LORA:KR_OFF], kvg_ref[...]).astype(BF16)
    kv = jnp.dot(kvn, wkv_ref[...], preferred_element_type=F32)
    kr = _rope(p[:, KR_OFF:HQ_OFF], c, s1, s2, 1)
    k_out[...] = (kv[:, :MLA_HEADS * HEAD_PAD] + jnp.concatenate([kr] * MLA_HEADS, axis=1)).astype(BF16)
    lane = lax.broadcasted_iota(I32, (1, MLA_HEADS * HEAD_PAD), 1)
    ones_lane = (lane % HEAD_PAD == MLA_V).astype(F32)
    v_out[...] = (kv[:, MLA_HEADS * HEAD_PAD:] + ones_lane).astype(BF16)

    lbl = lbl_ref[...]
    le = jnp.exp(lbl - jnp.max(lbl, axis=0, keepdims=True))
    lb = le[0:1] / jnp.sum(le, axis=0, keepdims=True)
    hq = p[:, HQ_OFF:HQ_OFF + HG_W]
    hf = p[:, HQ_OFF + HG_W:HQ_OFF + 2 * HG_W]
    hi = p[:, HQ_OFF + 2 * HG_W:HQ_OFF + 3 * HG_W]
    hg = p[:, HQ_OFF + 3 * HG_W:HQ_OFF + 4 * HG_W]
    hq_out[...] = (hq * _sigmoid_pair(hq)[0]).astype(BF16)
    sg, one_minus_sg = _sigmoid_pair(hf)
    hk_out[...] = ((1.0 - lb) * one_minus_sg).astype(BF16)
    hg_out[...] = jnp.log(lb + (1.0 - lb) * sg)
    hv_out[...] = hi.astype(BF16)
    gate_out[...] = (hg * _sigmoid_pair(hg)[0]).astype(BF16)


def _in_proj(xt, ln_g, ln_b, w_in_p, qa_g, w_qb_p, kv_g, w_kv_p, lb_logits, cos, sin):
    T = xt.shape[0]
    pc, ps = _rope_placement()
    row = lambda w: pl.BlockSpec((TM_IN, w), lambda i: (i, 0))
    full = lambda a: pl.BlockSpec(a.shape, lambda i: (0, 0))
    outs = [(MLA_HEADS * HEAD_PAD, BF16), (MLA_HEADS * HEAD_PAD, BF16), (MLA_HEADS * HEAD_PAD, BF16),
            (HG_W, BF16), (HG_W, BF16), (HG_W, F32), (HG_W, BF16), (HG_W, BF16)]
    return pl.pallas_call(
        _in_proj_body,
        grid=(T // TM_IN,),
        in_specs=[row(D_MODEL), full(ln_g), full(ln_b), full(w_in_p), full(qa_g), full(w_qb_p),
                  full(kv_g), full(w_kv_p), full(lb_logits), row(HALF_ROPE), row(HALF_ROPE),
                  full(pc), full(ps)],
        out_specs=[row(w) for w, _ in outs],
        out_shape=[jax.ShapeDtypeStruct((T, w), d) for w, d in outs],
        compiler_params=_cparams(("parallel",)),
        name="in_proj",
    )(xt, ln_g, ln_b, w_in_p, qa_g, w_qb_p, kv_g, w_kv_p, lb_logits, cos, sin, pc, ps)


def _attn_body(q_ref, k_ref, v_ref, o_ref):
    n_q = q_ref.shape[1] // TQ
    dn = (((1,), (1,)), ((), ()))
    row_c = lax.broadcasted_iota(I32, (TQ, TQ), 0) // CHUNK
    col_c = lax.broadcasted_iota(I32, (TQ, TQ), 1) // CHUNK
    diag_mask = col_c <= row_c
    heads = [slice(hh * HEAD_PAD, (hh + 1) * HEAD_PAD) for hh in range(2)]
    lane = lax.broadcasted_iota(I32, (TQ, 2 * MLA_V), 1)
    for qi in range(n_q):
        r0, kv = qi * TQ, (qi + 1) * TQ
        res = []
        for hs in heads:
            s = lax.dot_general(q_ref[0, r0:kv, hs], k_ref[0, :kv, hs], dn, preferred_element_type=F32)
            s_diag = jnp.where(diag_mask, s[:, r0:], -jnp.inf)
            m = jnp.max(s_diag, axis=-1, keepdims=True)
            if qi:
                m = jnp.maximum(m, jnp.max(s[:, :r0], axis=-1, keepdims=True))
                p = jnp.concatenate([jnp.exp2(s[:, :r0] - m), jnp.exp2(s_diag - m)], axis=1)
            else:
                p = jnp.exp2(s_diag - m)
            a = jnp.dot(p.astype(BF16), v_ref[0, :kv, hs], preferred_element_type=F32)
            res.append(a * (1.0 / a[:, MLA_V:MLA_V + 1]))
        o_ref[0, r0:kv, :] = jnp.where(lane < MLA_V, res[0], pltpu.roll(res[1], MLA_V, 1)).astype(BF16)


def _attention(q, k, v, B, S):
    w = MLA_HEADS * HEAD_PAD
    blk = pl.BlockSpec((1, S, 2 * HEAD_PAD), lambda b, h: (b, 0, h))
    out = pl.pallas_call(
        _attn_body,
        grid=(B, MLA_HEADS // 2),
        in_specs=[blk, blk, blk],
        out_specs=pl.BlockSpec((1, S, 2 * MLA_V), lambda b, h: (b, 0, h)),
        out_shape=jax.ShapeDtypeStruct((B, S, MLA_HEADS * MLA_V), BF16),
        compiler_params=_cparams(("parallel", "parallel")),
        name="mla_attention",
    )(q.reshape(B, S, w), k.reshape(B, S, w), v.reshape(B, S, w))
    return out.reshape(B * S, MLA_HEADS * MLA_V)


N_LEVELS = 6
Z_BLOCKS = N_LEVELS + 1


def _hgrn_constants():
    z = np.zeros((Z_BLOCKS, CHUNK, CHUNK), np.float32)
    masks = np.zeros((N_LEVELS, CHUNK, CHUNK), np.float32)
    idx = np.arange(CHUNK)
    for l in range(N_LEVELS):
        hs = CHUNK >> (l + 1)
        parent = idx // (2 * hs)
        right = (idx % (2 * hs)) >= hs
        mid = parent * 2 * hs + hs
        for t in range(CHUNK):
            z[l, t, :mid[t]] = 1.0
        masks[l] = (parent[:, None] == parent[None, :]) & right[:, None] & (~right)[None, :]
    z[N_LEVELS] = np.tril(np.ones((CHUNK, CHUNK), np.float32))
    z = z.reshape(Z_BLOCKS * CHUNK, CHUNK)
    n_c = HG_ROWS // CHUNK
    step_masks = np.zeros((N_LEVELS, HG_ROWS, HG_ROWS), np.float32)
    for c in range(n_c):
        step_masks[:, c * CHUNK:(c + 1) * CHUNK, c * CHUNK:(c + 1) * CHUNK] = masks
    return np.concatenate([z, z], axis=1), step_masks


def _hgrn_rows(r0, q_ref, k_ref, g_ref, v_ref, gate_ref, og_ref, z_ref, mask_ref, o_ref, state_ref):
    dn_t = (((1,), (1,)), ((), ()))
    n_c = HG_ROWS // CHUNK
    rr = slice(r0, r0 + HG_ROWS)
    eye = (lax.broadcasted_iota(I32, (HG_ROWS, HG_ROWS), 0)
           == lax.broadcasted_iota(I32, (HG_ROWS, HG_ROWS), 1)).astype(F32)
    sums = []
    for c in range(n_c):
        g2 = g_ref[0, r0 + c * CHUNK:r0 + (c + 1) * CHUNK, :] * LOG2E
        g_hi, g_lo = _split_bf16(g2)
        sums.append(jnp.dot(z_ref[...], jnp.concatenate([g_hi, g_lo], axis=0),
                            preferred_element_type=F32))
    level = lambda l, cols: jnp.concatenate([s[l * CHUNK:(l + 1) * CHUNK, cols] for s in sums], axis=0)
    for h in range(HG_HEADS):
        cols = slice(h * HG_DK, (h + 1) * HG_DK)
        b = level(N_LEVELS, cols)
        q = q_ref[0, rr, cols].astype(F32)
        k = k_ref[0, rr, cols].astype(F32)
        v = v_ref[0, rr, cols]
        a = eye * jnp.sum(q * k, axis=-1, keepdims=True)
        for l in range(N_LEVELS):
            d = jnp.exp2(-jnp.abs(b - level(l, cols)))
            a = a + mask_ref[l] * lax.dot_general((q * d).astype(BF16), (k * d).astype(BF16), dn_t,
                                                  preferred_element_type=F32)
        o_intra = jnp.dot(a.astype(BF16), v, preferred_element_type=F32)
        d_in = jnp.exp2(b)
        qd = (q * d_in).astype(BF16)
        st = state_ref[h]
        o_parts = []
        for c in range(n_c):
            rows = slice(c * CHUNK, (c + 1) * CHUNK)
            last = slice((c + 1) * CHUNK - 1, (c + 1) * CHUNK)
            o_parts.append(o_intra[rows] + lax.dot_general(qd[rows], st.astype(BF16), dn_t,
                                                           preferred_element_type=F32))
            kh = (k[rows] * jnp.exp2(b[last] - b[rows])).astype(BF16)
            upd = lax.dot_general(v[rows], kh, (((0,), (0,)), ((), ())), preferred_element_type=F32)
            st = st * d_in[last] + upd
        state_ref[h] = st
        o = jnp.concatenate(o_parts, axis=0)
        o = _rms_norm(o, og_ref[:, cols]) * gate_ref[0, rr, cols].astype(F32)
        o_ref[0, rr, cols] = o.astype(BF16)


def _hgrn_body(q_ref, k_ref, g_ref, v_ref, gate_ref, og_ref, z_ref, mask_ref, o_ref, state_ref):
    @pl.when(pl.program_id(1) == 0)
    def _():
        state_ref[...] = jnp.zeros_like(state_ref)

    _hgrn_rows(0, q_ref, k_ref, g_ref, v_ref, gate_ref, og_ref, z_ref, mask_ref, o_ref, state_ref)


def _hgrn(hq, hk, hg, hv, gate, out_g, B, S):
    zc, masks = _hgrn_constants()
    zc = jnp.asarray(zc, BF16)
    masks = jnp.asarray(masks, F32)
    blk = pl.BlockSpec((1, HG_ROWS, HG_W), lambda b, i: (b, i, 0))
    r3 = lambda a: a.reshape(B, S, HG_W)
    out = pl.pallas_call(
        _hgrn_body,
        grid=(B, S // HG_ROWS),
        in_specs=[blk, blk, blk, blk, blk,
                  pl.BlockSpec((1, HG_W), lambda b, i: (0, 0)),
                  pl.BlockSpec(zc.shape, lambda b, i: (0, 0)),
                  pl.BlockSpec(masks.shape, lambda b, i: (0, 0, 0))],
        out_specs=blk,
        out_shape=jax.ShapeDtypeStruct((B, S, HG_W), BF16),
        scratch_shapes=[pltpu.VMEM((HG_HEADS, HG_DV, HG_DK), F32)],
        compiler_params=_cparams(("parallel", "arbitrary")),
        name="hgrn2",
    )(r3(hq), r3(hk), r3(hg), r3(hv), r3(gate), out_g, zc, masks)
    return out.reshape(B * S, HG_W)


def _pack_bf16_pair(lo, hi):
    lo_b = pltpu.bitcast(lo.astype(BF16).astype(F32), I32)
    hi_b = pltpu.bitcast(hi.astype(BF16).astype(F32), I32)
    return hi_b | lax.shift_right_logical(lo_b, 16)


def _unpack_bf16_pair(w):
    lo = pltpu.bitcast(lax.shift_left(w, 16), F32)
    hi = pltpu.bitcast(w & jnp.int32(-65536), F32)
    return lo, hi


def _out_proj_body(x_ref, lng_ref, lnb_ref, attn_ref, hgo_ref, mg_ref, wo_ref, l1g_ref, l1b_ref,
                   wr_ref, br_ref,
                   h1_out, h1b_out, pos_out, gate_out, cnt_out):
    for sub_tile in range(TM_OUT // TM_DISP):
        rs = slice(sub_tile * TM_DISP, (sub_tile + 1) * TM_DISP)
        _out_proj_tile(x_ref[rs, :], lng_ref, lnb_ref, attn_ref[rs, :], hgo_ref[rs, :], mg_ref, wo_ref,
                       l1g_ref, l1b_ref, wr_ref, br_ref,
                       h1_out.at[rs, :], h1b_out.at[rs, :], pos_out.at[:, rs], gate_out.at[:, rs],
                       cnt_out.at[sub_tile])


def _out_proj_tile(x, lng_ref, lnb_ref, attn, hgo, mg_ref, wo_ref, l1g_ref, l1b_ref, wr_ref, br_ref,
                   h1_out, h1b_out, pos_out, gate_out, cnt_out):
    h = _layer_norm(x, lng_ref[...], lnb_ref[...])
    attn = _rms_norm(attn.astype(F32), mg_ref[...]).astype(BF16)
    mix_in = jnp.concatenate([attn, hgo], axis=1)
    mix = jnp.dot(mix_in, wo_ref[...], preferred_element_type=F32)
    h1 = _layer_norm(DEEPNORM_ALPHA * h + mix, l1g_ref[...], l1b_ref[...])
    h1_out[...] = h1
    h1_hi, h1_lo = _split_bf16(h1)
    h1b_out[...] = h1_hi

    w_hi, w_lo = _split_bf16(wr_ref[...])
    nt_dims = (((1,), (1,)), ((), ()))
    logits_t = (lax.dot_general(w_hi, h1_hi, nt_dims, preferred_element_type=F32)
                + lax.dot_general(w_hi, h1_lo, nt_dims, preferred_element_type=F32)
                + lax.dot_general(w_lo, h1_hi, nt_dims, preferred_element_type=F32)) + br_ref[...]
    tm = h1.shape[0]
    sub = lax.broadcasted_iota(I32, (N_EXPERTS, tm), 0)
    work = logits_t
    vals, idxs = [], []
    for _ in range(TOP_K):
        mx = jnp.max(work, axis=0, keepdims=True)
        ix = jnp.min(jnp.where(work == mx, sub, N_EXPERTS), axis=0, keepdims=True)
        vals.append(mx)
        idxs.append(ix)
        work = jnp.where(sub == ix, -jnp.inf, work)
    exps = [jnp.exp(vl - vals[0]) for vl in vals]
    den = exps[0] + exps[1] + exps[2] + exps[3]

    sel = jnp.zeros((N_EXPERTS, tm), F32)
    for ix in idxs:
        sel = sel + (sub == ix).astype(F32)
    selb = sel.astype(BF16)
    earlier = (lax.broadcasted_iota(I32, (tm, tm), 0)
               < lax.broadcasted_iota(I32, (tm, tm), 1)).astype(BF16)
    lower = (lax.broadcasted_iota(I32, (N_EXPERTS, N_EXPERTS), 1)
             < lax.broadcasted_iota(I32, (N_EXPERTS, N_EXPERTS), 0)).astype(BF16)
    cnt = jnp.sum(sel, axis=1, keepdims=True)
    run = jnp.floor((cnt + (SUBLANES - 1)) * (1.0 / SUBLANES)) * SUBLANES
    run_b = jnp.broadcast_to(run, (N_EXPERTS, LANES)).astype(BF16)
    seg_off = jnp.dot(lower, run_b, preferred_element_type=F32)[:, 0:1]
    posmat = jnp.dot(selb, earlier, preferred_element_type=F32) + seg_off
    pos = [jnp.sum(jnp.where(sub == ix, posmat, 0.0), axis=0, keepdims=True) for ix in idxs]
    pos_out[...] = jnp.concatenate(pos, axis=0).astype(I32)
    gsel = jnp.zeros((N_EXPERTS, tm), F32)
    for ix, ex in zip(idxs, exps):
        gsel = gsel + jnp.where(sub == ix, ex / den, 0.0)
    gate_out[...] = gsel
    cnt_out[...] = cnt


def _out_proj(xt, ln_g, ln_b, attn, hgo, mla_g, w_o, l1g, l1b, w_r, b_r):
    T = xt.shape[0]
    row = lambda w: pl.BlockSpec((TM_OUT, w), lambda i: (i, 0))
    full = lambda a: pl.BlockSpec(a.shape, lambda i: (0, 0))
    col = pl.BlockSpec((TOP_K, TM_OUT), lambda i: (0, i))
    sub_tiles = TM_OUT // TM_DISP
    return pl.pallas_call(
        _out_proj_body,
        grid=(T // TM_OUT,),
        in_specs=[row(D_MODEL), full(ln_g), full(ln_b), row(MLA_HEADS * MLA_V), row(HG_W), full(mla_g),
                  full(w_o), full(l1g), full(l1b), full(w_r), full(b_r)],
        out_specs=[row(D_MODEL), row(D_MODEL), col, pl.BlockSpec((N_EXPERTS, TM_OUT), lambda i: (0, i)),
                   pl.BlockSpec((sub_tiles, N_EXPERTS, 1), lambda i: (i, 0, 0))],
        out_shape=[jax.ShapeDtypeStruct((T, D_MODEL), F32), jax.ShapeDtypeStruct((T, D_MODEL), BF16),
                   jax.ShapeDtypeStruct((TOP_K, T), I32), jax.ShapeDtypeStruct((N_EXPERTS, T), F32),
                   jax.ShapeDtypeStruct((T // TM_DISP, N_EXPERTS, 1), F32)],
        compiler_params=_cparams(("parallel",)),
        name="out_proj_router",
    )(xt, ln_g, ln_b, attn, hgo, mla_g, w_o, l1g, l1b, w_r, b_r)


TILE_SLOTS = TOP_K * TM_DISP + N_EXPERTS * SUBLANES
ROW_WORDS = D_MODEL // 2 + LANES
TILES_PER_STEP = 2
TM_STEP = TILES_PER_STEP * TM_DISP
RUN_UNROLL = 4
RARE_RUN = 2 * TM_DISP * TOP_K // N_EXPERTS


def _ceil_to(n, m):
    return (n + m - 1) // m * m


def _rows(start, size):
    return pl.ds(pl.multiple_of(start, SUBLANES), size)


def _for_each_piece(n, fn, max_rows=TM_DISP, rare_from=None):
    def pieces(kbits):
        for kbit in kbits:
            size = 1 << kbit

            @pl.when((n & size) != 0)
            def _():
                fn(pl.multiple_of((n >> (kbit + 1)) << (kbit + 1), SUBLANES), size)

    kbits = list(reversed(range(SUBLANES.bit_length() - 1, max_rows.bit_length())))
    if rare_from is None:
        pieces(kbits)
    else:
        pl.when(n >= rare_from)(lambda: pieces([k for k in kbits if (1 << k) >= rare_from]))
        pieces([k for k in kbits if (1 << k) < rare_from])


def _pack_exact_pair(lo, hi):
    return pltpu.bitcast(hi, I32) | lax.shift_right_logical(pltpu.bitcast(lo, I32), 16)


def _zero_tail_blocks(zeros_ref, out_hbm, n_used, zsem):
    def blk_copy(g):
        return pltpu.make_async_copy(zeros_ref.at[pl.ds(0, ROW_UNIT)],
                                     out_hbm.at[_rows(g * ROW_UNIT, ROW_UNIT)], zsem)

    n_total = out_hbm.shape[0] // ROW_UNIT
    lax.fori_loop(n_used, n_total, lambda g, c: (blk_copy(g).start(), c)[1], 0)
    lax.fori_loop(n_used, n_total, lambda g, c: (blk_copy(g).wait(), c)[1], 0)


def _dispatch_body(segn_ref, segsrc_ref, segdst_ref, tot_ref, padn_ref, padd_ref, nused_ref, pos_ref, xb_ref, gx_ref,
                   xs_out, stage, zeros_ref, sem, zsem):
    i = pl.program_id(0)

    @pl.when(i == 0)
    def _():
        zeros_ref[...] = jnp.zeros_like(zeros_ref)

        def pad_copy(e, off, size):
            return pltpu.make_async_copy(zeros_ref.at[pl.ds(0, size)],
                                         xs_out.at[_rows(padd_ref[e] + off, size)], zsem)

        def fill(e, c):
            _for_each_piece(padn_ref[e], lambda off, size: pad_copy(e, off, size).start())
            return c

        def drain(e, c):
            _for_each_piece(padn_ref[e], lambda off, size: pad_copy(e, off, size).wait())
            return c

        lax.fori_loop(0, N_EXPERTS, fill, 0)
        lax.fori_loop(0, N_EXPERTS, drain, 0)
        _zero_tail_blocks(zeros_ref, xs_out, nused_ref[0], zsem)

    def settle(tile, b):
        _for_each_piece(tot_ref[tile], lambda off, size: pltpu.make_async_copy(
            stage.at[b, pl.ds(0, size)], xs_out.at[pl.ds(0, size)], sem.at[b]).wait(), TILE_SLOTS)

    for b in range(TILES_PER_STEP):
        tile = i * TILES_PER_STEP + b
        ts = slice(b * TM_DISP, (b + 1) * TM_DISP)
        pos = pos_ref[:, ts]
        slot = lax.broadcasted_iota(I32, (TILE_SLOTS, TM_DISP), 0)
        sel = jnp.zeros((TILE_SLOTS, TM_DISP), F32)
        for j in range(TOP_K):
            sel = sel + (slot == pos[j:j + 1, :]).astype(F32)
        src_rows = jnp.concatenate([xb_ref[ts, :], gx_ref[ts, :]], axis=1)
        rows = jnp.dot(sel.astype(BF16), src_rows, preferred_element_type=F32)
        packed = jnp.concatenate(
            [_pack_exact_pair(rows[:, :D_MODEL // 2], rows[:, D_MODEL // 2:D_MODEL]),
             pltpu.bitcast(rows[:, D_MODEL:], I32)], axis=1)

        @pl.when(i >= 1)
        def _():
            settle(tile - TILES_PER_STEP, b)

        stage[b] = packed

        def send(eb, c):
            for u in range(RUN_UNROLL):
                k = tile * N_EXPERTS + eb * RUN_UNROLL + u
                src, dst = segsrc_ref[k], segdst_ref[k]
                _for_each_piece(segn_ref[k], lambda off, size: pltpu.make_async_copy(
                    stage.at[b, _rows(src + off, size)], xs_out.at[_rows(dst + off, size)],
                    sem.at[b]).start(), rare_from=RARE_RUN)
            return c

        lax.fori_loop(0, N_EXPERTS // RUN_UNROLL, send, 0)

    @pl.when(i == pl.num_programs(0) - 1)
    def _():
        for b in range(TILES_PER_STEP):
            settle(i * TILES_PER_STEP + b, b)


def _dispatch(seg_n, seg_src, seg_dst, tile_tot, pad_n, pad_dst, n_used, pos_t, h1b, gates_x, n_rows):
    T = h1b.shape[0]
    grid_spec = pltpu.PrefetchScalarGridSpec(
        num_scalar_prefetch=7,
        grid=(T // TM_STEP,),
        in_specs=[pl.BlockSpec((TOP_K, TM_STEP), lambda i, *_: (0, i)),
                  pl.BlockSpec((TM_STEP, D_MODEL), lambda i, *_: (i, 0)),
                  pl.BlockSpec((TM_STEP, LANES), lambda i, *_: (i, 0))],
        out_specs=pl.BlockSpec(memory_space=pl.ANY),
        scratch_shapes=[pltpu.VMEM((TILES_PER_STEP, TILE_SLOTS, ROW_WORDS), I32),
                        pltpu.VMEM((ROW_BLOCK, ROW_WORDS), I32),
                        pltpu.SemaphoreType.DMA((TILES_PER_STEP,)), pltpu.SemaphoreType.DMA(())],
    )
    return pl.pallas_call(
        _dispatch_body,
        grid_spec=grid_spec,
        out_shape=jax.ShapeDtypeStruct((n_rows, ROW_WORDS), I32),
        compiler_params=_cparams(("arbitrary",)),
        name="moe_dispatch",
    )(seg_n, seg_src, seg_dst, tile_tot, pad_n, pad_dst, n_used, pos_t, h1b, gates_x)


def _experts_body(istart_ref, icount_ref, iunit_ref, ibig_ref, nitems_ref, uused_ref,
                  xs_hbm, wu_ref, bu_ref, wd_ref, bd_ref, ys_hbm,
                  wu_bf, wd_bf, xbuf, ybuf, sem_in, sem_out):
    e = pl.program_id(0)
    n_items = nitems_ref[0]
    nb = icount_ref[e]

    def by_size(k, fn):
        full = ibig_ref[k] == 1
        pl.when(full)(lambda: fn(ROW_BLOCK))
        pl.when(jnp.logical_not(full))(lambda: fn(ROW_UNIT))

    def fetch(k, slot, size):
        rows = _rows(iunit_ref[k] * ROW_UNIT, size)
        return pltpu.make_async_copy(xs_hbm.at[rows], xbuf.at[slot, pl.ds(0, size)], sem_in.at[slot])

    def put(k, slot, size):
        rows = _rows(iunit_ref[k] * ROW_UNIT, size)
        return pltpu.make_async_copy(ybuf.at[slot, pl.ds(0, size)], ys_hbm.at[rows], sem_out.at[slot])

    @pl.when(jnp.logical_and(e == 0, n_items > 0))
    def _():
        by_size(0, lambda size: fetch(0, 0, size).start())

    @pl.when(nb > 0)
    def _():
        wu_bf[...] = wu_ref[0].astype(BF16)
        wd_bf[...] = wd_ref[0].astype(BF16)

    def compute(k, slot, size):
        row = xbuf[slot, :size]
        lo, hi = _unpack_bf16_pair(row[:, :D_MODEL // 2])
        x = jnp.concatenate([lo, hi], axis=1).astype(BF16)
        lane = lax.broadcasted_iota(I32, (size, LANES), 1)
        mine = jnp.logical_or(lane == e, lane == e + N_EXPERTS)
        gate = jnp.sum(jnp.where(mine, pltpu.bitcast(row[:, D_MODEL // 2:], F32), 0.0),
                       axis=1, keepdims=True)
        hb = jnp.dot(x, wu_bf[...], preferred_element_type=F32) + bu_ref[0]
        glu = jnp.minimum(hb[:, :D_EXPERT], SWIGLU_LIMIT)
        lin = jnp.clip(hb[:, D_EXPERT:], -SWIGLU_LIMIT, SWIGLU_LIMIT)
        act = glu * (1.0 / (1.0 + jnp.exp(-SWIGLU_ALPHA * glu))) * (lin + 1.0)
        y = (jnp.dot(act.astype(BF16), wd_bf[...], preferred_element_type=F32) + bd_ref[0]) * gate
        ybuf[slot, :size] = _pack_bf16_pair(y[:, :D_MODEL // 2], y[:, D_MODEL // 2:])
        put(k, slot, size).start()

    def block(b, c):
        k = istart_ref[e] + b
        slot = lax.rem(k, 2)
        by_size(k, lambda size: fetch(k, slot, size).wait())

        @pl.when(k + 1 < n_items)
        def _():
            by_size(k + 1, lambda size: fetch(k + 1, 1 - slot, size).start())

        @pl.when(k >= 2)
        def _():
            by_size(k - 2, lambda size: put(k - 2, slot, size).wait())

        by_size(k, lambda size: compute(k, slot, size))
        return c

    lax.fori_loop(0, nb, block, 0)

    @pl.when(e == N_EXPERTS - 1)
    def _():
        for back in (2, 1):
            @pl.when(n_items >= back)
            def _():
                k = n_items - back
                by_size(k, lambda size: put(k, lax.rem(k, 2), size).wait())
        ybuf[0] = jnp.zeros_like(ybuf[0])
        _zero_tail_blocks(ybuf.at[0], ys_hbm, uused_ref[0], sem_out.at[0])


def _experts(item_start, item_count, item_unit, item_big, n_items, units_used, xs,
             w_up, b_up, w_down, b_down):
    n_rows = xs.shape[0]
    grid_spec = pltpu.PrefetchScalarGridSpec(
        num_scalar_prefetch=6,
        grid=(N_EXPERTS,),
        in_specs=[pl.BlockSpec(memory_space=pl.ANY),
                  pl.BlockSpec((1, D_MODEL, 2 * D_EXPERT), lambda e, *_: (e, 0, 0)),
                  pl.BlockSpec((1, 1, 2 * D_EXPERT), lambda e, *_: (e, 0, 0)),
                  pl.BlockSpec((1, D_EXPERT, D_MODEL), lambda e, *_: (e, 0, 0)),
                  pl.BlockSpec((1, 1, D_MODEL), lambda e, *_: (e, 0, 0))],
        out_specs=pl.BlockSpec(memory_space=pl.ANY),
        scratch_shapes=[pltpu.VMEM((D_MODEL, 2 * D_EXPERT), BF16), pltpu.VMEM((D_EXPERT, D_MODEL), BF16),
                        pltpu.VMEM((2, ROW_BLOCK, ROW_WORDS), I32),
                        pltpu.VMEM((2, ROW_BLOCK, D_MODEL // 2), I32),
                        pltpu.SemaphoreType.DMA((2,)), pltpu.SemaphoreType.DMA((2,))],
    )
    return pl.pallas_call(
        _experts_body,
        grid_spec=grid_spec,
        out_shape=jax.ShapeDtypeStruct((n_rows, D_MODEL // 2), I32),
        compiler_params=_cparams(("arbitrary",)),
        name="moe_experts",
    )(item_start, item_count, item_unit, item_big, n_items, units_used, xs, w_up,
      b_up.reshape(N_EXPERTS, 1, 2 * D_EXPERT), w_down, b_down.reshape(N_EXPERTS, 1, D_MODEL))


def _combine_body(segn_ref, segsrc_ref, segdst_ref, tot_ref, pos_ref, h1_ref, l2g_ref, l2b_ref, ys_hbm,
                  o_ref, stage, sem):
    i = pl.program_id(0)
    n_tiles = pl.num_programs(0) * TILES_PER_STEP

    def fetch_tile(tile, b):
        def fetch(eb, c):
            for u in range(RUN_UNROLL):
                k = tile * N_EXPERTS + eb * RUN_UNROLL + u
                src, dst = segsrc_ref[k], segdst_ref[k]
                _for_each_piece(segn_ref[k], lambda off, size: pltpu.make_async_copy(
                    ys_hbm.at[_rows(dst + off, size)], stage.at[b, _rows(src + off, size)],
                    sem.at[b]).start(), rare_from=RARE_RUN)
            return c

        lax.fori_loop(0, N_EXPERTS // RUN_UNROLL, fetch, 0)

    @pl.when(i == 0)
    def _():
        stage[...] = jnp.zeros_like(stage)
        fetch_tile(0, 0)

    for b in range(TILES_PER_STEP):
        tile = i * TILES_PER_STEP + b
        nxt = (b + 1) % TILES_PER_STEP

        @pl.when(tile + 1 < n_tiles)
        def _():
            fetch_tile(tile + 1, nxt)

        ts = slice(b * TM_DISP, (b + 1) * TM_DISP)
        pos = pos_ref[ts, :]
        slot = lax.broadcasted_iota(I32, (TM_DISP, TILE_SLOTS), 1)
        w = jnp.zeros((TM_DISP, TILE_SLOTS), F32)
        for j in range(TOP_K):
            w = w + (slot == pos[:, j:j + 1]).astype(F32)
        w = w.astype(BF16)
        _for_each_piece(tot_ref[tile], lambda off, size: pltpu.make_async_copy(
            ys_hbm.at[pl.ds(0, size)], stage.at[b, pl.ds(0, size)], sem.at[b]).wait(), TILE_SLOTS)
        lo, hi = _unpack_bf16_pair(stage[b])
        y = jnp.concatenate([lo, hi], axis=1).astype(BF16)
        ffn = jnp.dot(w, y, preferred_element_type=F32)
        o_ref[ts, :] = _layer_norm(DEEPNORM_ALPHA * h1_ref[ts, :] + ffn, l2g_ref[...], l2b_ref[...])


def _combine(seg_n, seg_src, seg_dst, tile_tot, pos, h1, l2g, l2b, ys):
    T = h1.shape[0]
    grid_spec = pltpu.PrefetchScalarGridSpec(
        num_scalar_prefetch=4,
        grid=(T // TM_STEP,),
        in_specs=[pl.BlockSpec((TM_STEP, TOP_K), lambda i, *_: (i, 0)),
                  pl.BlockSpec((TM_STEP, D_MODEL), lambda i, *_: (i, 0)),
                  pl.BlockSpec(l2g.shape, lambda i, *_: (0, 0)),
                  pl.BlockSpec(l2b.shape, lambda i, *_: (0, 0)),
                  pl.BlockSpec(memory_space=pl.ANY)],
        out_specs=pl.BlockSpec((TM_STEP, D_MODEL), lambda i, *_: (i, 0)),
        scratch_shapes=[pltpu.VMEM((TILES_PER_STEP, TILE_SLOTS, D_MODEL // 2), I32),
                        pltpu.SemaphoreType.DMA((TILES_PER_STEP,))],
    )
    return pl.pallas_call(
        _combine_body,
        grid_spec=grid_spec,
        out_shape=jax.ShapeDtypeStruct((T, D_MODEL), F32),
        compiler_params=_cparams(("arbitrary",)),
        name="moe_combine",
    )(seg_n, seg_src, seg_dst, tile_tot, pos, h1, l2g, l2b, ys)


def _prep_weights(w_in, w_q_b, w_kv_b):
    zc = lambda n: jnp.zeros((D_MODEL, n), F32)
    w_in_p = jnp.concatenate(
        [w_in[:, :KR_OFF], zc(MLA_NOPE), w_in[:, KR_OFF:KR_OFF + MLA_ROPE],
         zc(HEAD_PAD - MLA_NOPE - MLA_ROPE), w_in[:, KR_OFF + MLA_ROPE:]], axis=1).astype(BF16)
    qd = MLA_NOPE + MLA_ROPE
    wq = w_q_b.reshape(Q_LORA, MLA_HEADS, qd)
    wq = jnp.pad(wq, ((0, 0), (0, 0), (0, HEAD_PAD - qd))).reshape(Q_LORA, MLA_HEADS * HEAD_PAD)
    wkv = w_kv_b.reshape(KV_LORA, MLA_HEADS, MLA_NOPE + MLA_V)
    wk = jnp.pad(wkv[:, :, :MLA_NOPE], ((0, 0), (0, 0), (0, HEAD_PAD - MLA_NOPE)))
    wk = wk.reshape(KV_LORA, MLA_HEADS * HEAD_PAD)
    wv = jnp.pad(wkv[:, :, MLA_NOPE:], ((0, 0), (0, 0), (0, HEAD_PAD - MLA_V)))
    wv = wv.reshape(KV_LORA, MLA_HEADS * HEAD_PAD)
    return w_in_p, wq.astype(BF16), jnp.concatenate([wk, wv], axis=1).astype(BF16)


def kernel(x, positions, ln_in_g, ln_in_b, w_in, q_a_norm_g, w_q_b, kv_a_norm_g, w_kv_b, hgrn_lb_logits,
           mla_out_g, hgrn_out_g, w_o, ln1_g, ln1_b, w_router, b_router, w_up, b_up, w_down, b_down,
           ln2_g, ln2_b):
    B, S, D = x.shape
    assert D == D_MODEL and w_in.shape[0] == DEPTH and hgrn_lb_logits.shape[0] == DEPTH + 1
    T = B * S
    xt = x.reshape(T, D)
    r2 = lambda a: a.reshape(1, -1)

    cos, sin = _rope_cos_sin(positions)
    w_in_p, w_qb_p, w_kv_p = _prep_weights(w_in[0], w_q_b[0], w_kv_b[0])
    q, k, v, hq, hk, hg, hv, gate = _in_proj(
        xt, r2(ln_in_g), r2(ln_in_b), w_in_p, q_a_norm_g, w_qb_p, kv_a_norm_g, w_kv_p,
        hgrn_lb_logits, cos, sin)
    attn = _attention(q, k, v, B, S)
    hgo = _hgrn(hq, hk, hg, hv, gate, hgrn_out_g, B, S)
    h1, h1b, pos_t, gate_t, tile_cnt = _out_proj(
        xt, r2(ln_in_g), r2(ln_in_b), attn, hgo, mla_out_g, w_o[0].astype(BF16), ln1_g, ln1_b,
        w_router[0].T, b_router.reshape(N_EXPERTS, 1))

    n_tiles = T // TM_DISP
    n_rows = T * TOP_K + n_tiles * N_EXPERTS * (SUBLANES - 1) + N_EXPERTS * ROW_UNIT
    n_rows = _ceil_to(n_rows, ROW_BLOCK)
    tc = _ceil_to(tile_cnt[:, :, 0].astype(I32), SUBLANES)
    cnt = jnp.sum(tc, axis=0)
    padded = _ceil_to(cnt, ROW_UNIT)
    pad_start = jnp.cumsum(padded) - padded
    seg_n = tc.reshape(-1)
    seg_src = (jnp.cumsum(tc, axis=1) - tc).reshape(-1)
    seg_dst = (pad_start[None, :] + jnp.cumsum(tc, axis=0) - tc).reshape(-1)
    units_used = jnp.sum(padded, keepdims=True) // ROW_UNIT

    units = padded // ROW_UNIT
    item_count = (units + 1) // 2
    item_end = jnp.cumsum(item_count)
    item_start = item_end - item_count
    k = jnp.arange(n_rows // ROW_BLOCK + N_EXPERTS, dtype=I32)
    of_expert = (jnp.minimum(jnp.sum((item_end[None, :] <= k[:, None]).astype(I32), axis=1), N_EXPERTS - 1)
                 [:, None] == jnp.arange(N_EXPERTS, dtype=I32)[None, :])
    pick = lambda v: jnp.sum(jnp.where(of_expert, v[None, :], 0), axis=1)
    first_unit = 2 * (k - pick(item_start))
    item_unit = pick(pad_start // ROW_UNIT) + first_unit
    item_big = (first_unit + 2 <= pick(units)).astype(I32)

    tile_tot = jnp.sum(tc, axis=1)
    g_hi, g_lo = _split_bf16(gate_t.T)
    gates_x = jnp.concatenate([g_hi, g_lo, jnp.zeros((T, LANES - 2 * N_EXPERTS), BF16)], axis=1)
    xs = _dispatch(seg_n, seg_src, seg_dst, tile_tot, padded - cnt, pad_start + cnt, units_used, pos_t, h1b,
                   gates_x, n_rows)
    ys = _experts(item_start, item_count, item_unit, item_big, item_end[-1:], units_used, xs,
                  w_up[0], b_up[0], w_down[0], b_down[0])
    out = _combine(seg_n, seg_src, seg_dst, tile_tot, pos_t.T, h1, ln2_g, ln2_b, ys)
    return out.reshape(B, S, D)
```

```python
import numpy as np
import jax
import jax.numpy as jnp
from jax import lax
from jax.experimental import pallas as pl
from jax.experimental.pallas import tpu as pltpu

F32 = jnp.float32
BF16 = jnp.bfloat16
I32 = jnp.int32

D_MODEL = 1024
CHUNK = 64
MLA_HEADS = 8
MLA_NOPE = 64
MLA_ROPE = 32
MLA_V = 64
Q_LORA = 384
KV_LORA = 256
ROPE_THETA = 10000.0
HG_HEADS = 4
HG_DK = 128
HG_DV = 128
HG_W = HG_HEADS * HG_DK
N_EXPERTS = 32
TOP_K = 4
D_EXPERT = 1024
SWIGLU_LIMIT = 7.0
SWIGLU_ALPHA = 1.702
LN_EPS = 1e-5
RMS_EPS = 1e-6
DEPTH = 1
DEEPNORM_ALPHA = (2.0 * DEPTH) ** 0.25
LOG2E = 1.4426950408889634

LANES = 128
SUBLANES = 8
HEAD_PAD = 128
HALF_ROPE = MLA_ROPE // 2
KR_OFF = Q_LORA + KV_LORA
HQ_OFF = KR_OFF + HEAD_PAD
IN_COLS_PAD = HQ_OFF + 4 * HG_W

TM_IN = 512
TQ = 512
HG_ROWS = 256
TM_OUT = 512
TM_DISP = 256
ROW_BLOCK = 512
ROW_UNIT = ROW_BLOCK // 2
VMEM_LIMIT = 52 * 1024 * 1024


def _cparams(sem, vmem=VMEM_LIMIT):
    return pltpu.CompilerParams(dimension_semantics=sem, vmem_limit_bytes=vmem)


def _layer_norm(x, g, b):
    mu = jnp.mean(x, axis=-1, keepdims=True)
    xc = x - mu
    var = jnp.mean(xc * xc, axis=-1, keepdims=True)
    return xc * lax.rsqrt(var + LN_EPS) * g + b


def _rms_norm(x, g):
    return x * lax.rsqrt(jnp.mean(x * x, axis=-1, keepdims=True) + RMS_EPS) * g


def _split_bf16(a):
    hi = a.astype(BF16)
    return hi, (a - hi.astype(F32)).astype(BF16)


def _sigmoid_pair(x):
    e = jnp.exp(-jnp.abs(x))
    big = 1.0 / (1.0 + e)
    small = e * big
    pos = x >= 0
    return jnp.where(pos, big, small), jnp.where(pos, small, big)


def _cos_sin_body(ang_ref, cos_ref, sin_ref):
    a = ang_ref[...]
    cos_ref[...] = jnp.cos(a)
    sin_ref[...] = jnp.sin(a)


def _rope_cos_sin(positions):
    T = positions.size
    inv_freq = ROPE_THETA ** (-jnp.arange(0, MLA_ROPE, 2, dtype=F32) / MLA_ROPE)
    ang = (positions.astype(F32)[..., None] * inv_freq).reshape(T * HALF_ROPE // LANES, LANES)
    cos, sin = pl.pallas_call(
        _cos_sin_body,
        out_shape=(jax.ShapeDtypeStruct(ang.shape, F32),) * 2,
        name="rope_cos_sin",
    )(ang)
    return cos.reshape(T, HALF_ROPE), sin.reshape(T, HALF_ROPE)


def _rope_placement():
    pc = np.zeros((HALF_ROPE, HEAD_PAD), np.float32)
    ps = np.zeros((HALF_ROPE, 2 * HEAD_PAD), np.float32)
    for j in range(HALF_ROPE):
        pc[j, MLA_NOPE + j] = pc[j, MLA_NOPE + HALF_ROPE + j] = 1.0
        ps[j, MLA_NOPE + HALF_ROPE + j] = 1.0
        ps[j, HEAD_PAD + MLA_NOPE + j] = -1.0
    return jnp.asarray(pc, BF16), jnp.asarray(ps, BF16)


def _rope_tables(cos, sin, pc, ps):
    place = lambda a, p: sum(jnp.dot(part, p, preferred_element_type=F32) for part in _split_bf16(a))
    lane = lax.broadcasted_iota(I32, (1, HEAD_PAD), 1)
    c = place(cos, pc) + (lane < MLA_NOPE).astype(F32)
    s = place(sin, ps)
    return c, s[:, :HEAD_PAD], s[:, HEAD_PAD:]


def _rope(x, c, s1, s2, reps):
    width = reps * HEAD_PAD
    if reps > 1:
        c = jnp.concatenate([c] * reps, axis=1)
        s1 = jnp.concatenate([s1] * reps, axis=1)
        s2 = jnp.concatenate([s2] * reps, axis=1)
    return (x * c + pltpu.roll(x, HALF_ROPE, 1) * s1
            + pltpu.roll(x, width - HALF_ROPE, 1) * s2)


def _in_proj_body(x_ref, lng_ref, lnb_ref, win_ref, qag_ref, wqb_ref, kvg_ref, wkv_ref, lbl_ref,
                  cos_ref, sin_ref, pc_ref, ps_ref,
                  q_out, k_out, v_out, hq_out, hk_out, hg_out, hv_out, gate_out, h_out):
    h = _layer_norm(x_ref[...], lng_ref[...], lnb_ref[...])
    h_out[...] = h
    p = jnp.dot(h.astype(BF16), win_ref[...], preferred_element_type=F32)
    c, s1, s2 = _rope_tables(cos_ref[...], sin_ref[...], pc_ref[...], ps_ref[...])

    qn = _rms_norm(p[:, :Q_LORA], qag_ref[...]).astype(BF16)
    q = jnp.dot(qn, wqb_ref[...], preferred_element_type=F32)
    scale = (MLA_NOPE + MLA_ROPE) ** -0.5 * LOG2E
    q_out[...] = (_rope(q, c, s1, s2, MLA_HEADS) * scale).astype(BF16)
    kvn = _rms_norm(p[:, Q_LORA:KR_OFF], kvg_ref[...]).astype(BF16)
    kv = jnp.dot(kvn, wkv_ref[...], preferred_element_type=F32)
    kr = _rope(p[:, KR_OFF:HQ_OFF], c, s1, s2, 1)
    k_out[...] = (kv[:, :MLA_HEADS * HEAD_PAD] + jnp.concatenate([kr] * MLA_HEADS, axis=1)).astype(BF16)
    lane = lax.broadcasted_iota(I32, (1, MLA_HEADS * HEAD_PAD), 1)
    ones_lane = (lane % HEAD_PAD == MLA_V).astype(F32)
    v_out[...] = (kv[:, MLA_HEADS * HEAD_PAD:] + ones_lane).astype(BF16)

    lbl = lbl_ref[...]
    le = jnp.exp(lbl - jnp.max(lbl, axis=0, keepdims=True))
    lb = le[0:1] / jnp.sum(le, axis=0, keepdims=True)
    hq = p[:, HQ_OFF:HQ_OFF + HG_W]
    hf = p[:, HQ_OFF + HG_W:HQ_OFF + 2 * HG_W]
    hi = p[:, HQ_OFF + 2 * HG_W:HQ_OFF + 3 * HG_W]
    hg = p[:, HQ_OFF + 3 * HG_W:HQ_OFF + 4 * HG_W]
    hq_out[...] = (hq * _sigmoid_pair(hq)[0]).astype(BF16)
    sg, one_minus_sg = _sigmoid_pair(hf)
    hk_out[...] = ((1.0 - lb) * one_minus_sg).astype(BF16)
    hg_out[...] = jnp.log(lb + (1.0 - lb) * sg)
    hv_out[...] = hi.astype(BF16)
    gate_out[...] = (hg * _sigmoid_pair(hg)[0]).astype(BF16)


def _in_proj(xt, ln_g, ln_b, w_in_p, qa_g, w_qb_p, kv_g, w_kv_p, lb_logits, cos, sin):
    T = xt.shape[0]
    pc, ps = _rope_placement()
    row = lambda w: pl.BlockSpec((TM_IN, w), lambda i: (i, 0))
    full = lambda a: pl.BlockSpec(a.shape, lambda i: (0, 0))
    outs = [(MLA_HEADS * HEAD_PAD, BF16), (MLA_HEADS * HEAD_PAD, BF16), (MLA_HEADS * HEAD_PAD, BF16),
            (HG_W, BF16), (HG_W, BF16), (HG_W, F32), (HG_W, BF16), (HG_W, BF16), (D_MODEL, F32)]
    return pl.pallas_call(
        _in_proj_body,
        grid=(T // TM_IN,),
        in_specs=[row(D_MODEL), full(ln_g), full(ln_b), full(w_in_p), full(qa_g), full(w_qb_p),
                  full(kv_g), full(w_kv_p), full(lb_logits), row(HALF_ROPE), row(HALF_ROPE),
                  full(pc), full(ps)],
        out_specs=[row(w) for w, _ in outs],
        out_shape=[jax.ShapeDtypeStruct((T, w), d) for w, d in outs],
        compiler_params=_cparams(("parallel",)),
        name="in_proj",
    )(xt, ln_g, ln_b, w_in_p, qa_g, w_qb_p, kv_g, w_kv_p, lb_logits, cos, sin, pc, ps)


def _attn_body(q_ref, k_ref, v_ref, o_ref):
    n_q = q_ref.shape[1] // TQ
    dn = (((1,), (1,)), ((), ()))
    row_c = lax.broadcasted_iota(I32, (TQ, TQ), 0) // CHUNK
    col_c = lax.broadcasted_iota(I32, (TQ, TQ), 1) // CHUNK
    diag_mask = col_c <= row_c
    heads = [slice(hh * HEAD_PAD, (hh + 1) * HEAD_PAD) for hh in range(2)]
    lane = lax.broadcasted_iota(I32, (TQ, 2 * MLA_V), 1)
    for qi in range(n_q):
        r0, kv = qi * TQ, (qi + 1) * TQ
        res = []
        for hs in heads:
            s = lax.dot_general(q_ref[0, r0:kv, hs], k_ref[0, :kv, hs], dn, preferred_element_type=F32)
            s_diag = jnp.where(diag_mask, s[:, r0:], -jnp.inf)
            m = jnp.max(s_diag, axis=-1, keepdims=True)
            if qi:
                m = jnp.maximum(m, jnp.max(s[:, :r0], axis=-1, keepdims=True))
                p = jnp.concatenate([jnp.exp2(s[:, :r0] - m), jnp.exp2(s_diag - m)], axis=1)
            else:
                p = jnp.exp2(s_diag - m)
            a = jnp.dot(p.astype(BF16), v_ref[0, :kv, hs], preferred_element_type=F32)
            res.append(a * (1.0 / a[:, MLA_V:MLA_V + 1]))
        o_ref[0, r0:kv, :] = jnp.where(lane < MLA_V, res[0], pltpu.roll(res[1], MLA_V, 1)).astype(BF16)


def _attention(q, k, v, B, S):
    w = MLA_HEADS * HEAD_PAD
    blk = pl.BlockSpec((1, S, 2 * HEAD_PAD), lambda b, h: (b, 0, h))
    out = pl.pallas_call(
        _attn_body,
        grid=(B, MLA_HEADS // 2),
        in_specs=[blk, blk, blk],
        out_specs=pl.BlockSpec((1, S, 2 * MLA_V), lambda b, h: (b, 0, h)),
        out_shape=jax.ShapeDtypeStruct((B, S, MLA_HEADS * MLA_V), BF16),
        compiler_params=_cparams(("parallel", "parallel")),
        name="mla_attention",
    )(q.reshape(B, S, w), k.reshape(B, S, w), v.reshape(B, S, w))
    return out.reshape(B * S, MLA_HEADS * MLA_V)


N_LEVELS = 6
Z_BLOCKS = N_LEVELS + 1


def _hgrn_constants():
    z = np.zeros((Z_BLOCKS, CHUNK, CHUNK), np.float32)
    masks = np.zeros((N_LEVELS, CHUNK, CHUNK), np.float32)
    idx = np.arange(CHUNK)
    for l in range(N_LEVELS):
        hs = CHUNK >> (l + 1)
        parent = idx // (2 * hs)
        right = (idx % (2 * hs)) >= hs
        mid = parent * 2 * hs + hs
        for t in range(CHUNK):
            z[l, t, :mid[t]] = 1.0
        masks[l] = (parent[:, None] == parent[None, :]) & right[:, None] & (~right)[None, :]
    z[N_LEVELS] = np.tril(np.ones((CHUNK, CHUNK), np.float32))
    z = z.reshape(Z_BLOCKS * CHUNK, CHUNK)
    tile_masks = np.zeros((N_LEVELS, LANES, LANES), np.float32)
    for c in range(LANES // CHUNK):
        tile_masks[:, c * CHUNK:(c + 1) * CHUNK, c * CHUNK:(c + 1) * CHUNK] = masks
    return np.concatenate([z, z], axis=1), tile_masks


def _hgrn_rows(r0, q_ref, k_ref, g_ref, v_ref, gate_ref, og_ref, z_ref, mask_ref, o_ref, state_ref):
    dn_t = (((1,), (1,)), ((), ()))
    n_c = HG_ROWS // CHUNK
    n_t = HG_ROWS // LANES
    rr = slice(r0, r0 + HG_ROWS)
    eye = (lax.broadcasted_iota(I32, (LANES, LANES), 0)
           == lax.broadcasted_iota(I32, (LANES, LANES), 1)).astype(F32)
    sums = []
    for c in range(n_c):
        g2 = g_ref[0, r0 + c * CHUNK:r0 + (c + 1) * CHUNK, :] * LOG2E
        g_hi, g_lo = _split_bf16(g2)
        sums.append(jnp.dot(z_ref[...], jnp.concatenate([g_hi, g_lo], axis=0),
                            preferred_element_type=F32))
    level = lambda l, cols: jnp.concatenate([s[l * CHUNK:(l + 1) * CHUNK, cols] for s in sums], axis=0)
    for h in range(HG_HEADS):
        cols = slice(h * HG_DK, (h + 1) * HG_DK)
        b = level(N_LEVELS, cols)
        q = q_ref[0, rr, cols].astype(F32)
        k = k_ref[0, rr, cols].astype(F32)
        v = v_ref[0, rr, cols]
        diag = jnp.sum(q * k, axis=-1, keepdims=True)
        a = [eye * diag[t * LANES:(t + 1) * LANES] for t in range(n_t)]
        for l in range(N_LEVELS):
            d = jnp.exp2(-jnp.abs(b - level(l, cols)))
            s = lax.dot_general((q * d).astype(BF16), (k * d).astype(BF16), dn_t,
                                preferred_element_type=F32)
            for t in range(n_t):
                ts = slice(t * LANES, (t + 1) * LANES)
                a[t] = a[t] + mask_ref[l] * s[ts, ts]
        o_intra = jnp.concatenate(
            [jnp.dot(a[t].astype(BF16), v[t * LANES:(t + 1) * LANES], preferred_element_type=F32)
             for t in range(n_t)], axis=0)
        d_in = jnp.exp2(b)
        qd = (q * d_in).astype(BF16)
        st = state_ref[h]
        o_parts = []
        for c in range(n_c):
            rows = slice(c * CHUNK, (c + 1) * CHUNK)
            last = slice((c + 1) * CHUNK - 1, (c + 1) * CHUNK)
            o_parts.append(o_intra[rows] + lax.dot_general(qd[rows], st.astype(BF16), dn_t,
                                                           preferred_element_type=F32))
            kh = (k[rows] * jnp.exp2(b[last] - b[rows])).astype(BF16)
            upd = lax.dot_general(v[rows], kh, (((0,), (0,)), ((), ())), preferred_element_type=F32)
            st = st * d_in[last] + upd
        state_ref[h] = st
        o = jnp.concatenate(o_parts, axis=0)
        o = _rms_norm(o, og_ref[:, cols]) * gate_ref[0, rr, cols].astype(F32)
        o_ref[0, rr, cols] = o.astype(BF16)


def _hgrn_body(q_ref, k_ref, g_ref, v_ref, gate_ref, og_ref, z_ref, mask_ref, o_ref, state_ref):
    @pl.when(pl.program_id(1) == 0)
    def _():
        state_ref[...] = jnp.zeros_like(state_ref)

    _hgrn_rows(0, q_ref, k_ref, g_ref, v_ref, gate_ref, og_ref, z_ref, mask_ref, o_ref, state_ref)


def _hgrn(hq, hk, hg, hv, gate, out_g, B, S):
    zc, masks = _hgrn_constants()
    zc = jnp.asarray(zc, BF16)
    masks = jnp.asarray(masks, F32)
    blk = pl.BlockSpec((1, HG_ROWS, HG_W), lambda b, i: (b, i, 0))
    r3 = lambda a: a.reshape(B, S, HG_W)
    out = pl.pallas_call(
        _hgrn_body,
        grid=(B, S // HG_ROWS),
        in_specs=[blk, blk, blk, blk, blk,
                  pl.BlockSpec((1, HG_W), lambda b, i: (0, 0)),
                  pl.BlockSpec(zc.shape, lambda b, i: (0, 0)),
                  pl.BlockSpec(masks.shape, lambda b, i: (0, 0, 0))],
        out_specs=blk,
        out_shape=jax.ShapeDtypeStruct((B, S, HG_W), BF16),
        scratch_shapes=[pltpu.VMEM((HG_HEADS, HG_DV, HG_DK), F32)],
        compiler_params=_cparams(("parallel", "arbitrary")),
        name="hgrn2",
    )(r3(hq), r3(hk), r3(hg), r3(hv), r3(gate), out_g, zc, masks)
    return out.reshape(B * S, HG_W)


def _pack_bf16_pair(lo, hi):
    lo_b = pltpu.bitcast(lo.astype(BF16).astype(F32), I32)
    hi_b = pltpu.bitcast(hi.astype(BF16).astype(F32), I32)
    return hi_b | lax.shift_right_logical(lo_b, 16)


def _unpack_bf16_pair(w):
    lo = pltpu.bitcast(lax.shift_left(w, 16), F32)
    hi = pltpu.bitcast(w & jnp.int32(-65536), F32)
    return lo, hi


def _out_proj_body(h_ref, attn_ref, hgo_ref, mg_ref, wo_ref, l1g_ref, l1b_ref, wr_ref, br_ref,
                   h1_out, h1b_out, pos_out, gate_out, cnt_out):
    for sub_tile in range(TM_OUT // TM_DISP):
        rs = slice(sub_tile * TM_DISP, (sub_tile + 1) * TM_DISP)
        _out_proj_tile(h_ref[rs, :], attn_ref[rs, :], hgo_ref[rs, :], mg_ref, wo_ref,
                       l1g_ref, l1b_ref, wr_ref, br_ref,
                       h1_out.at[rs, :], h1b_out.at[rs, :], pos_out.at[:, rs], gate_out.at[:, rs],
                       cnt_out.at[sub_tile])


def _out_proj_tile(h, attn, hgo, mg_ref, wo_ref, l1g_ref, l1b_ref, wr_ref, br_ref,
                   h1_out, h1b_out, pos_out, gate_out, cnt_out):
    attn = _rms_norm(attn.astype(F32), mg_ref[...]).astype(BF16)
    mix_in = jnp.concatenate([attn, hgo], axis=1)
    mix = jnp.dot(mix_in, wo_ref[...], preferred_element_type=F32)
    h1 = _layer_norm(DEEPNORM_ALPHA * h + mix, l1g_ref[...], l1b_ref[...])
    h1_out[...] = h1
    h1_hi, h1_lo = _split_bf16(h1)
    h1b_out[...] = h1_hi

    w_hi, w_lo = _split_bf16(wr_ref[...])
    nt_dims = (((1,), (1,)), ((), ()))
    logits_t = (lax.dot_general(w_hi, h1_hi, nt_dims, preferred_element_type=F32)
                + lax.dot_general(w_hi, h1_lo, nt_dims, preferred_element_type=F32)
                + lax.dot_general(w_lo, h1_hi, nt_dims, preferred_element_type=F32)) + br_ref[...]
    tm = h1.shape[0]
    sub = lax.broadcasted_iota(I32, (N_EXPERTS, tm), 0)
    work = logits_t
    vals, idxs = [], []
    for _ in range(TOP_K):
        mx = jnp.max(work, axis=0, keepdims=True)
        ix = jnp.min(jnp.where(work == mx, sub, N_EXPERTS), axis=0, keepdims=True)
        vals.append(mx)
        idxs.append(ix)
        work = jnp.where(sub == ix, -jnp.inf, work)
    exps = [jnp.exp(vl - vals[0]) for vl in vals]
    den = exps[0] + exps[1] + exps[2] + exps[3]

    sel = jnp.zeros((N_EXPERTS, tm), F32)
    for ix in idxs:
        sel = sel + (sub == ix).astype(F32)
    selb = sel.astype(BF16)
    earlier = (lax.broadcasted_iota(I32, (tm, tm), 0)
               < lax.broadcasted_iota(I32, (tm, tm), 1)).astype(BF16)
    lower = (lax.broadcasted_iota(I32, (N_EXPERTS, N_EXPERTS), 1)
             < lax.broadcasted_iota(I32, (N_EXPERTS, N_EXPERTS), 0)).astype(BF16)
    cnt = jnp.sum(sel, axis=1, keepdims=True)
    run = jnp.floor((cnt + (SUBLANES - 1)) * (1.0 / SUBLANES)) * SUBLANES
    run_b = jnp.broadcast_to(run, (N_EXPERTS, LANES)).astype(BF16)
    seg_off = jnp.dot(lower, run_b, preferred_element_type=F32)[:, 0:1]
    posmat = jnp.dot(selb, earlier, preferred_element_type=F32) + seg_off
    pos = [jnp.sum(jnp.where(sub == ix, posmat, 0.0), axis=0, keepdims=True) for ix in idxs]
    pos_out[...] = jnp.concatenate(pos, axis=0).astype(I32)
    gsel = jnp.zeros((N_EXPERTS, tm), F32)
    for ix, ex in zip(idxs, exps):
        gsel = gsel + jnp.where(sub == ix, ex / den, 0.0)
    gate_out[...] = gsel
    cnt_out[...] = cnt


def _out_proj(h, attn, hgo, mla_g, w_o, l1g, l1b, w_r, b_r):
    T = h.shape[0]
    row = lambda w: pl.BlockSpec((TM_OUT, w), lambda i: (i, 0))
    full = lambda a: pl.BlockSpec(a.shape, lambda i: (0, 0))
    col = pl.BlockSpec((TOP_K, TM_OUT), lambda i: (0, i))
    sub_tiles = TM_OUT // TM_DISP
    return pl.pallas_call(
        _out_proj_body,
        grid=(T // TM_OUT,),
        in_specs=[row(D_MODEL), row(MLA_HEADS * MLA_V), row(HG_W), full(mla_g),
                  full(w_o), full(l1g), full(l1b), full(w_r), full(b_r)],
        out_specs=[row(D_MODEL), row(D_MODEL), col, pl.BlockSpec((N_EXPERTS, TM_OUT), lambda i: (0, i)),
                   pl.BlockSpec((sub_tiles, N_EXPERTS, 1), lambda i: (i, 0, 0))],
        out_shape=[jax.ShapeDtypeStruct((T, D_MODEL), F32), jax.ShapeDtypeStruct((T, D_MODEL), BF16),
                   jax.ShapeDtypeStruct((TOP_K, T), I32), jax.ShapeDtypeStruct((N_EXPERTS, T), F32),
                   jax.ShapeDtypeStruct((T // TM_DISP, N_EXPERTS, 1), F32)],
        compiler_params=_cparams(("parallel",)),
        name="out_proj_router",
    )(h, attn, hgo, mla_g, w_o, l1g, l1b, w_r, b_r)


TILE_SLOTS = TOP_K * TM_DISP + N_EXPERTS * SUBLANES
ROW_WORDS = D_MODEL // 2 + LANES
TILES_PER_STEP = 2
TM_STEP = TILES_PER_STEP * TM_DISP
RUN_UNROLL = 4
RARE_RUN = 2 * TM_DISP * TOP_K // N_EXPERTS


def _ceil_to(n, m):
    return (n + m - 1) // m * m


def _rows(start, size):
    return pl.ds(pl.multiple_of(start, SUBLANES), size)


def _for_each_piece(n, fn, max_rows=TM_DISP, rare_from=None):
    def pieces(kbits):
        for kbit in kbits:
            size = 1 << kbit

            @pl.when((n & size) != 0)
            def _():
                fn(pl.multiple_of((n >> (kbit + 1)) << (kbit + 1), SUBLANES), size)

    kbits = list(reversed(range(SUBLANES.bit_length() - 1, max_rows.bit_length())))
    if rare_from is None:
        pieces(kbits)
    else:
        pl.when(n >= rare_from)(lambda: pieces([k for k in kbits if (1 << k) >= rare_from]))
        pieces([k for k in kbits if (1 << k) < rare_from])


def _pack_exact_pair(lo, hi):
    return pltpu.bitcast(hi, I32) | lax.shift_right_logical(pltpu.bitcast(lo, I32), 16)


def _zero_tail_blocks(zeros_ref, out_hbm, n_used, zsem):
    def blk_copy(g):
        return pltpu.make_async_copy(zeros_ref.at[pl.ds(0, ROW_UNIT)],
                                     out_hbm.at[_rows(g * ROW_UNIT, ROW_UNIT)], zsem)

    n_total = out_hbm.shape[0] // ROW_UNIT
    lax.fori_loop(n_used, n_total, lambda g, c: (blk_copy(g).start(), c)[1], 0)
    lax.fori_loop(n_used, n_total, lambda g, c: (blk_copy(g).wait(), c)[1], 0)


def _dispatch_body(segn_ref, segsrc_ref, segdst_ref, tot_ref, padn_ref, padd_ref, nused_ref, pos_ref, xb_ref, g_ref,
                   xs_out, stage, zeros_ref, sem, zsem):
    i = pl.program_id(0)

    @pl.when(i == 0)
    def _():
        zeros_ref[...] = jnp.zeros_like(zeros_ref)

        def pad_copy(e, off, size):
            return pltpu.make_async_copy(zeros_ref.at[pl.ds(0, size)],
                                         xs_out.at[_rows(padd_ref[e] + off, size)], zsem)

        def fill(e, c):
            _for_each_piece(padn_ref[e], lambda off, size: pad_copy(e, off, size).start())
            return c

        def drain(e, c):
            _for_each_piece(padn_ref[e], lambda off, size: pad_copy(e, off, size).wait())
            return c

        lax.fori_loop(0, N_EXPERTS, fill, 0)
        lax.fori_loop(0, N_EXPERTS, drain, 0)
        _zero_tail_blocks(zeros_ref, xs_out, nused_ref[0], zsem)

    def settle(tile, b):
        _for_each_piece(tot_ref[tile], lambda off, size: pltpu.make_async_copy(
            stage.at[b, pl.ds(0, size)], xs_out.at[pl.ds(0, size)], sem.at[b]).wait(), TILE_SLOTS)

    for b in range(TILES_PER_STEP):
        tile = i * TILES_PER_STEP + b
        ts = slice(b * TM_DISP, (b + 1) * TM_DISP)
        pos = pos_ref[:, ts]
        slot = lax.broadcasted_iota(I32, (TILE_SLOTS, TM_DISP), 0)
        sel = jnp.zeros((TILE_SLOTS, TM_DISP), F32)
        for j in range(TOP_K):
            sel = sel + (slot == pos[j:j + 1, :]).astype(F32)
        g_hi, g_lo = _split_bf16(g_ref[:, ts])
        e_sub = lax.broadcasted_iota(I32, (N_EXPERTS, LANES), 0)
        e_lane = lax.broadcasted_iota(I32, (N_EXPERTS, LANES), 1)
        tn_dims = (((0,), (0,)), ((), ()))
        gx = (lax.dot_general(g_hi, (e_lane == e_sub).astype(BF16), tn_dims, preferred_element_type=F32)
              + lax.dot_general(g_lo, (e_lane == e_sub + N_EXPERTS).astype(BF16), tn_dims,
                                preferred_element_type=F32))
        src_rows = jnp.concatenate([xb_ref[ts, :], gx.astype(BF16)], axis=1)
        rows = jnp.dot(sel.astype(BF16), src_rows, preferred_element_type=F32)
        packed = jnp.concatenate(
            [_pack_exact_pair(rows[:, :D_MODEL // 2], rows[:, D_MODEL // 2:D_MODEL]),
             pltpu.bitcast(rows[:, D_MODEL:], I32)], axis=1)

        @pl.when(i >= 1)
        def _():
            settle(tile - TILES_PER_STEP, b)

        stage[b] = packed

        def send(eb, c):
            for u in range(RUN_UNROLL):
                k = tile * N_EXPERTS + eb * RUN_UNROLL + u
                src, dst = segsrc_ref[k], segdst_ref[k]
                _for_each_piece(segn_ref[k], lambda off, size: pltpu.make_async_copy(
                    stage.at[b, _rows(src + off, size)], xs_out.at[_rows(dst + off, size)],
                    sem.at[b]).start(), rare_from=RARE_RUN)
            return c

        lax.fori_loop(0, N_EXPERTS // RUN_UNROLL, send, 0)

    @pl.when(i == pl.num_programs(0) - 1)
    def _():
        for b in range(TILES_PER_STEP):
            settle(i * TILES_PER_STEP + b, b)


def _dispatch(seg_n, seg_src, seg_dst, tile_tot, pad_n, pad_dst, n_used, pos_t, h1b, gates_x, n_rows):
    T = h1b.shape[0]
    grid_spec = pltpu.PrefetchScalarGridSpec(
        num_scalar_prefetch=7,
        grid=(T // TM_STEP,),
        in_specs=[pl.BlockSpec((TOP_K, TM_STEP), lambda i, *_: (0, i)),
                  pl.BlockSpec((TM_STEP, D_MODEL), lambda i, *_: (i, 0)),
                  pl.BlockSpec((N_EXPERTS, TM_STEP), lambda i, *_: (0, i))],
        out_specs=pl.BlockSpec(memory_space=pl.ANY),
        scratch_shapes=[pltpu.VMEM((TILES_PER_STEP, TILE_SLOTS, ROW_WORDS), I32),
                        pltpu.VMEM((ROW_BLOCK, ROW_WORDS), I32),
                        pltpu.SemaphoreType.DMA((TILES_PER_STEP,)), pltpu.SemaphoreType.DMA(())],
    )
    return pl.pallas_call(
        _dispatch_body,
        grid_spec=grid_spec,
        out_shape=jax.ShapeDtypeStruct((n_rows, ROW_WORDS), I32),
        compiler_params=_cparams(("arbitrary",)),
        name="moe_dispatch",
    )(seg_n, seg_src, seg_dst, tile_tot, pad_n, pad_dst, n_used, pos_t, h1b, gates_x)


def _experts_body(istart_ref, icount_ref, iunit_ref, ibig_ref, nitems_ref, uused_ref,
                  xs_hbm, wu_ref, bu_ref, wd_ref, bd_ref, ys_hbm,
                  wu_bf, wd_bf, xbuf, ybuf, sem_in, sem_out):
    e = pl.program_id(0)
    n_items = nitems_ref[0]
    nb = icount_ref[e]

    def by_size(k, fn):
        full = ibig_ref[k] == 1
        pl.when(full)(lambda: fn(ROW_BLOCK))
        pl.when(jnp.logical_not(full))(lambda: fn(ROW_UNIT))

    def fetch(k, slot, size):
        rows = _rows(iunit_ref[k] * ROW_UNIT, size)
        return pltpu.make_async_copy(xs_hbm.at[rows], xbuf.at[slot, pl.ds(0, size)], sem_in.at[slot])

    def put(k, slot, size):
        rows = _rows(iunit_ref[k] * ROW_UNIT, size)
        return pltpu.make_async_copy(ybuf.at[slot, pl.ds(0, size)], ys_hbm.at[rows], sem_out.at[slot])

    @pl.when(jnp.logical_and(e == 0, n_items > 0))
    def _():
        by_size(0, lambda size: fetch(0, 0, size).start())

    @pl.when(nb > 0)
    def _():
        wu_bf[...] = wu_ref[0].astype(BF16)
        wd_bf[...] = wd_ref[0].astype(BF16)

    def compute(k, slot, size):
        row = xbuf[slot, :size]
        lo, hi = _unpack_bf16_pair(row[:, :D_MODEL // 2])
        x = jnp.concatenate([lo, hi], axis=1).astype(BF16)
        lane = lax.broadcasted_iota(I32, (size, LANES), 1)
        mine = jnp.logical_or(lane == e, lane == e + N_EXPERTS)
        gate = jnp.sum(jnp.where(mine, pltpu.bitcast(row[:, D_MODEL // 2:], F32), 0.0),
                       axis=1, keepdims=True)
        hb = jnp.dot(x, wu_bf[...], preferred_element_type=F32) + bu_ref[0]
        glu = jnp.minimum(hb[:, :D_EXPERT], SWIGLU_LIMIT)
        lin = jnp.clip(hb[:, D_EXPERT:], -SWIGLU_LIMIT, SWIGLU_LIMIT)
        act = glu * (1.0 / (1.0 + jnp.exp(-SWIGLU_ALPHA * glu))) * (lin + 1.0)
        y = (jnp.dot(act.astype(BF16), wd_bf[...], preferred_element_type=F32) + bd_ref[0]) * gate
        ybuf[slot, :size] = _pack_bf16_pair(y[:, :D_MODEL // 2], y[:, D_MODEL // 2:])
        put(k, slot, size).start()

    def block(b, c):
        k = istart_ref[e] + b
        slot = lax.rem(k, 2)
        by_size(k, lambda size: fetch(k, slot, size).wait())

        @pl.when(k + 1 < n_items)
        def _():
            by_size(k + 1, lambda size: fetch(k + 1, 1 - slot, size).start())

        @pl.when(k >= 2)
        def _():
            by_size(k - 2, lambda size: put(k - 2, slot, size).wait())

        by_size(k, lambda size: compute(k, slot, size))
        return c

    lax.fori_loop(0, nb, block, 0)

    @pl.when(e == N_EXPERTS - 1)
    def _():
        for back in (2, 1):
            @pl.when(n_items >= back)
            def _():
                k = n_items - back
                by_size(k, lambda size: put(k, lax.rem(k, 2), size).wait())
        ybuf[0] = jnp.zeros_like(ybuf[0])
        _zero_tail_blocks(ybuf.at[0], ys_hbm, uused_ref[0], sem_out.at[0])


def _experts(item_start, item_count, item_unit, item_big, n_items, units_used, xs,
             w_up, b_up, w_down, b_down):
    n_rows = xs.shape[0]
    grid_spec = pltpu.PrefetchScalarGridSpec(
        num_scalar_prefetch=6,
        grid=(N_EXPERTS,),
        in_specs=[pl.BlockSpec(memory_space=pl.ANY),
                  pl.BlockSpec((1, D_MODEL, 2 * D_EXPERT), lambda e, *_: (e, 0, 0)),
                  pl.BlockSpec((1, 1, 2 * D_EXPERT), lambda e, *_: (e, 0, 0)),
                  pl.BlockSpec((1, D_EXPERT, D_MODEL), lambda e, *_: (e, 0, 0)),
                  pl.BlockSpec((1, 1, D_MODEL), lambda e, *_: (e, 0, 0))],
        out_specs=pl.BlockSpec(memory_space=pl.ANY),
        scratch_shapes=[pltpu.VMEM((D_MODEL, 2 * D_EXPERT), BF16), pltpu.VMEM((D_EXPERT, D_MODEL), BF16),
                        pltpu.VMEM((2, ROW_BLOCK, ROW_WORDS), I32),
                        pltpu.VMEM((2, ROW_BLOCK, D_MODEL // 2), I32),
                        pltpu.SemaphoreType.DMA((2,)), pltpu.SemaphoreType.DMA((2,))],
    )
    return pl.pallas_call(
        _experts_body,
        grid_spec=grid_spec,
        out_shape=jax.ShapeDtypeStruct((n_rows, D_MODEL // 2), I32),
        compiler_params=_cparams(("arbitrary",)),
        name="moe_experts",
    )(item_start, item_count, item_unit, item_big, n_items, units_used, xs, w_up,
      b_up.reshape(N_EXPERTS, 1, 2 * D_EXPERT), w_down, b_down.reshape(N_EXPERTS, 1, D_MODEL))


def _combine_body(segn_ref, segsrc_ref, segdst_ref, tot_ref, pos_ref, h1_ref, l2g_ref, l2b_ref, ys_hbm,
                  o_ref, stage, sem):
    i = pl.program_id(0)
    n_tiles = pl.num_programs(0) * TILES_PER_STEP

    def fetch_tile(tile, b):
        def fetch(eb, c):
            for u in range(RUN_UNROLL):
                k = tile * N_EXPERTS + eb * RUN_UNROLL + u
                src, dst = segsrc_ref[k], segdst_ref[k]
                _for_each_piece(segn_ref[k], lambda off, size: pltpu.make_async_copy(
                    ys_hbm.at[_rows(dst + off, size)], stage.at[b, _rows(src + off, size)],
                    sem.at[b]).start(), rare_from=RARE_RUN)
            return c

        lax.fori_loop(0, N_EXPERTS // RUN_UNROLL, fetch, 0)

    @pl.when(i == 0)
    def _():
        stage[...] = jnp.zeros_like(stage)
        fetch_tile(0, 0)

    for b in range(TILES_PER_STEP):
        tile = i * TILES_PER_STEP + b
        nxt = (b + 1) % TILES_PER_STEP

        @pl.when(tile + 1 < n_tiles)
        def _():
            fetch_tile(tile + 1, nxt)

        ts = slice(b * TM_DISP, (b + 1) * TM_DISP)
        pos = pos_ref[ts, :]
        slot = lax.broadcasted_iota(I32, (TM_DISP, TILE_SLOTS), 1)
        w = jnp.zeros((TM_DISP, TILE_SLOTS), F32)
        for j in range(TOP_K):
            w = w + (slot == pos[:, j:j + 1]).astype(F32)
        w = w.astype(BF16)
        _for_each_piece(tot_ref[tile], lambda off, size: pltpu.make_async_copy(
            ys_hbm.at[pl.ds(0, size)], stage.at[b, pl.ds(0, size)], sem.at[b]).wait(), TILE_SLOTS)
        lo, hi = _unpack_bf16_pair(stage[b])
        y = jnp.concatenate([lo, hi], axis=1).astype(BF16)
        ffn = jnp.dot(w, y, preferred_element_type=F32)
        o_ref[ts, :] = _layer_norm(DEEPNORM_ALPHA * h1_ref[ts, :] + ffn, l2g_ref[...], l2b_ref[...])


def _combine(seg_n, seg_src, seg_dst, tile_tot, pos, h1, l2g, l2b, ys):
    T = h1.shape[0]
    grid_spec = pltpu.PrefetchScalarGridSpec(
        num_scalar_prefetch=4,
        grid=(T // TM_STEP,),
        in_specs=[pl.BlockSpec((TM_STEP, TOP_K), lambda i, *_: (i, 0)),
                  pl.BlockSpec((TM_STEP, D_MODEL), lambda i, *_: (i, 0)),
                  pl.BlockSpec(l2g.shape, lambda i, *_: (0, 0)),
                  pl.BlockSpec(l2b.shape, lambda i, *_: (0, 0)),
                  pl.BlockSpec(memory_space=pl.ANY)],
        out_specs=pl.BlockSpec((TM_STEP, D_MODEL), lambda i, *_: (i, 0)),
        scratch_shapes=[pltpu.VMEM((TILES_PER_STEP, TILE_SLOTS, D_MODEL // 2), I32),
                        pltpu.SemaphoreType.DMA((TILES_PER_STEP,))],
    )
    return pl.pallas_call(
        _combine_body,
        grid_spec=grid_spec,
        out_shape=jax.ShapeDtypeStruct((T, D_MODEL), F32),
        compiler_params=_cparams(("arbitrary",)),
        name="moe_combine",
    )(seg_n, seg_src, seg_dst, tile_tot, pos, h1, l2g, l2b, ys)


def _prep_weights(w_in, w_q_b, w_kv_b):
    zc = lambda n: jnp.zeros((D_MODEL, n), F32)
    w_in_p = jnp.concatenate(
        [w_in[:, :KR_OFF], zc(MLA_NOPE), w_in[:, KR_OFF:KR_OFF + MLA_ROPE],
         zc(HEAD_PAD - MLA_NOPE - MLA_ROPE), w_in[:, KR_OFF + MLA_ROPE:]], axis=1).astype(BF16)
    qd = MLA_NOPE + MLA_ROPE
    wq = w_q_b.reshape(Q_LORA, MLA_HEADS, qd)
    wq = jnp.pad(wq, ((0, 0), (0, 0), (0, HEAD_PAD - qd))).reshape(Q_LORA, MLA_HEADS * HEAD_PAD)
    wkv = w_kv_b.reshape(KV_LORA, MLA_HEADS, MLA_NOPE + MLA_V)
    wk = jnp.pad(wkv[:, :, :MLA_NOPE], ((0, 0), (0, 0), (0, HEAD_PAD - MLA_NOPE)))
    wk = wk.reshape(KV_LORA, MLA_HEADS * HEAD_PAD)
    wv = jnp.pad(wkv[:, :, MLA_NOPE:], ((0, 0), (0, 0), (0, HEAD_PAD - MLA_V)))
    wv = wv.reshape(KV_LORA, MLA_HEADS * HEAD_PAD)
    return w_in_p, wq.astype(BF16), jnp.concatenate([wk, wv], axis=1).astype(BF16)


def kernel(x, positions, ln_in_g, ln_in_b, w_in, q_a_norm_g, w_q_b, kv_a_norm_g, w_kv_b, hgrn_lb_logits,
           mla_out_g, hgrn_out_g, w_o, ln1_g, ln1_b, w_router, b_router, w_up, b_up, w_down, b_down,
           ln2_g, ln2_b):
    B, S, D = x.shape
    assert D == D_MODEL and w_in.shape[0] == DEPTH and hgrn_lb_logits.shape[0] == DEPTH + 1
    T = B * S
    xt = x.reshape(T, D)
    r2 = lambda a: a.reshape(1, -1)

    cos, sin = _rope_cos_sin(positions)
    w_in_p, w_qb_p, w_kv_p = _prep_weights(w_in[0], w_q_b[0], w_kv_b[0])
    q, k, v, hq, hk, hg, hv, gate, h0 = _in_proj(
        xt, r2(ln_in_g), r2(ln_in_b), w_in_p, q_a_norm_g, w_qb_p, kv_a_norm_g, w_kv_p,
        hgrn_lb_logits, cos, sin)
    attn = _attention(q, k, v, B, S)
    hgo = _hgrn(hq, hk, hg, hv, gate, hgrn_out_g, B, S)
    h1, h1b, pos_t, gate_t, tile_cnt = _out_proj(
        h0, attn, hgo, mla_out_g, w_o[0].astype(BF16), ln1_g, ln1_b,
        w_router[0].T, b_router.reshape(N_EXPERTS, 1))

    n_tiles = T // TM_DISP
    n_rows = T * TOP_K + n_tiles * N_EXPERTS * (SUBLANES - 1) + N_EXPERTS * ROW_UNIT
    n_rows = _ceil_to(n_rows, ROW_BLOCK)
    tc = _ceil_to(tile_cnt[:, :, 0].astype(I32), SUBLANES)
    cnt = jnp.sum(tc, axis=0)
    padded = _ceil_to(cnt, ROW_UNIT)
    pad_start = jnp.cumsum(padded) - padded
    seg_n = tc.reshape(-1)
    seg_src = (jnp.cumsum(tc, axis=1) - tc).reshape(-1)
    seg_dst = (pad_start[None, :] + jnp.cumsum(tc, axis=0) - tc).reshape(-1)
    units_used = jnp.sum(padded, keepdims=True) // ROW_UNIT

    units = padded // ROW_UNIT
    item_count = (units + 1) // 2
    item_end = jnp.cumsum(item_count)
    item_start = item_end - item_count
    k = jnp.arange(n_rows // ROW_BLOCK + N_EXPERTS, dtype=I32)
    of_expert = (jnp.minimum(jnp.sum((item_end[None, :] <= k[:, None]).astype(I32), axis=1), N_EXPERTS - 1)
                 [:, None] == jnp.arange(N_EXPERTS, dtype=I32)[None, :])
    pick = lambda v: jnp.sum(jnp.where(of_expert, v[None, :], 0), axis=1)
    first_unit = 2 * (k - pick(item_start))
    item_unit = pick(pad_start // ROW_UNIT) + first_unit
    item_big = (first_unit + 2 <= pick(units)).astype(I32)

    tile_tot = jnp.sum(tc, axis=1)
    xs = _dispatch(seg_n, seg_src, seg_dst, tile_tot, padded - cnt, pad_start + cnt, units_used, pos_t, h1b,
                   gate_t, n_rows)
    ys = _experts(item_start, item_count, item_unit, item_big, item_end[-1:], units_used, xs,
                  w_up[0], b_up[0], w_down[0], b_down[0])
    out = _combine(seg_n, seg_src, seg_dst, tile_tot, pos_t.T, h1, ln2_g, ln2_b, ys)
    return out.reshape(B, S, D)
```

```python
import numpy as np
import jax
import jax.numpy as jnp
from jax import lax
from jax.experimental import pallas as pl
from jax.experimental.pallas import tpu as pltpu

F32 = jnp.float32
BF16 = jnp.bfloat16
I32 = jnp.int32

D_MODEL = 1024
CHUNK = 64
MLA_HEADS = 8
MLA_NOPE = 64
MLA_ROPE = 32
MLA_V = 64
Q_LORA = 384
KV_LORA = 256
ROPE_THETA = 10000.0
HG_HEADS = 4
HG_DK = 128
HG_DV = 128
HG_W = HG_HEADS * HG_DK
N_EXPERTS = 32
TOP_K = 4
D_EXPERT = 1024
SWIGLU_LIMIT = 7.0
SWIGLU_ALPHA = 1.702
LN_EPS = 1e-5
RMS_EPS = 1e-6
DEPTH = 1
DEEPNORM_ALPHA = (2.0 * DEPTH) ** 0.25
LOG2E = 1.4426950408889634

LANES = 128
SUBLANES = 8
HEAD_PAD = 128
HALF_ROPE = MLA_ROPE // 2
KR_OFF = Q_LORA + KV_LORA
HQ_OFF = KR_OFF + HEAD_PAD
IN_COLS_PAD = HQ_OFF + 4 * HG_W

TM_IN = 512
TQ = 512
HG_ROWS = 256
TM_OUT = 1024
TM_DISP = 256
ROW_BLOCK = 512
ROW_UNIT = ROW_BLOCK // 2
VMEM_LIMIT = 52 * 1024 * 1024


def _cparams(sem, vmem=VMEM_LIMIT):
    return pltpu.CompilerParams(dimension_semantics=sem, vmem_limit_bytes=vmem)


def _layer_norm(x, g, b):
    mu = jnp.mean(x, axis=-1, keepdims=True)
    xc = x - mu
    var = jnp.mean(xc * xc, axis=-1, keepdims=True)
    return xc * lax.rsqrt(var + LN_EPS) * g + b


def _rms_norm(x, g):
    return x * lax.rsqrt(jnp.mean(x * x, axis=-1, keepdims=True) + RMS_EPS) * g


def _split_bf16(a):
    hi = a.astype(BF16)
    return hi, (a - hi.astype(F32)).astype(BF16)


def _sigmoid_pair(x):
    e = jnp.exp(-jnp.abs(x))
    big = 1.0 / (1.0 + e)
    small = e * big
    pos = x >= 0
    return jnp.where(pos, big, small), jnp.where(pos, small, big)


def _cos_sin_body(ang_ref, cos_ref, sin_ref):
    a = ang_ref[...]
    cos_ref[...] = jnp.cos(a)
    sin_ref[...] = jnp.sin(a)


def _rope_cos_sin(positions):
    T = positions.size
    inv_freq = ROPE_THETA ** (-jnp.arange(0, MLA_ROPE, 2, dtype=F32) / MLA_ROPE)
    ang = (positions.astype(F32)[..., None] * inv_freq).reshape(T * HALF_ROPE // LANES, LANES)
    cos, sin = pl.pallas_call(
        _cos_sin_body,
        out_shape=(jax.ShapeDtypeStruct(ang.shape, F32),) * 2,
        name="rope_cos_sin",
    )(ang)
    return cos.reshape(T, HALF_ROPE), sin.reshape(T, HALF_ROPE)


def _rope_placement():
    pc = np.zeros((HALF_ROPE, HEAD_PAD), np.float32)
    ps = np.zeros((HALF_ROPE, 2 * HEAD_PAD), np.float32)
    for j in range(HALF_ROPE):
        pc[j, MLA_NOPE + j] = pc[j, MLA_NOPE + HALF_ROPE + j] = 1.0
        ps[j, MLA_NOPE + HALF_ROPE + j] = 1.0
        ps[j, HEAD_PAD + MLA_NOPE + j] = -1.0
    return jnp.asarray(pc, BF16), jnp.asarray(ps, BF16)


def _rope_tables(cos, sin, pc, ps):
    place = lambda a, p: sum(jnp.dot(part, p, preferred_element_type=F32) for part in _split_bf16(a))
    lane = lax.broadcasted_iota(I32, (1, HEAD_PAD), 1)
    c = place(cos, pc) + (lane < MLA_NOPE).astype(F32)
    s = place(sin, ps)
    return c, s[:, :HEAD_PAD], s[:, HEAD_PAD:]


def _rope(x, c, s1, s2, reps):
    width = reps * HEAD_PAD
    if reps > 1:
        c = jnp.concatenate([c] * reps, axis=1)
        s1 = jnp.concatenate([s1] * reps, axis=1)
        s2 = jnp.concatenate([s2] * reps, axis=1)
    return (x * c + pltpu.roll(x, HALF_ROPE, 1) * s1
            + pltpu.roll(x, width - HALF_ROPE, 1) * s2)


def _in_proj_body(x_ref, lng_ref, lnb_ref, wa_ref, wh_ref, qag_ref, wqb_ref, kvg_ref, wkv_ref, lbl_ref,
                  cos_ref, sin_ref, pc_ref, ps_ref,
                  q_out, k_out, v_out, hq_out, hk_out, hg_out, hv_out, gate_out, h_out):
    win_refs = (wa_ref, wh_ref)
    h = _layer_norm(x_ref[...], lng_ref[...], lnb_ref[...])
    h_out[...] = h
    hb = h.astype(BF16)
    p = jnp.concatenate([jnp.dot(hb, w_ref[...], preferred_element_type=F32) for w_ref in win_refs],
                        axis=1)
    c, s1, s2 = _rope_tables(cos_ref[...], sin_ref[...], pc_ref[...], ps_ref[...])

    qn = _rms_norm(p[:, :Q_LORA], qag_ref[...]).astype(BF16)
    q = jnp.dot(qn, wqb_ref[...], preferred_element_type=F32)
    scale = (MLA_NOPE + MLA_ROPE) ** -0.5 * LOG2E
    q_out[...] = (_rope(q, c, s1, s2, MLA_HEADS) * scale).astype(BF16)
    kvn = _rms_norm(p[:, Q_LORA:KR_OFF], kvg_ref[...]).astype(BF16)
    kv = jnp.dot(kvn, wkv_ref[...], preferred_element_type=F32)
    kr = _rope(p[:, KR_OFF:HQ_OFF], c, s1, s2, 1)
    k_out[...] = (kv[:, :MLA_HEADS * HEAD_PAD] + jnp.concatenate([kr] * MLA_HEADS, axis=1)).astype(BF16)
    lane = lax.broadcasted_iota(I32, (1, MLA_HEADS * HEAD_PAD), 1)
    ones_lane = (lane % HEAD_PAD == MLA_V).astype(F32)
    v_out[...] = (kv[:, MLA_HEADS * HEAD_PAD:] + ones_lane).astype(BF16)

    lbl = lbl_ref[...]
    le = jnp.exp(lbl - jnp.max(lbl, axis=0, keepdims=True))
    lb = le[0:1] / jnp.sum(le, axis=0, keepdims=True)
    hq = p[:, HQ_OFF:HQ_OFF + HG_W]
    hf = p[:, HQ_OFF + HG_W:HQ_OFF + 2 * HG_W]
    hi = p[:, HQ_OFF + 2 * HG_W:HQ_OFF + 3 * HG_W]
    hg = p[:, HQ_OFF + 3 * HG_W:HQ_OFF + 4 * HG_W]
    hq_out[...] = (hq * _sigmoid_pair(hq)[0]).astype(BF16)
    sg, one_minus_sg = _sigmoid_pair(hf)
    hk_out[...] = ((1.0 - lb) * one_minus_sg).astype(BF16)
    hg_out[...] = jnp.log(lb + (1.0 - lb) * sg)
    hv_out[...] = hi.astype(BF16)
    gate_out[...] = (hg * _sigmoid_pair(hg)[0]).astype(BF16)


def _in_proj(xt, ln_g, ln_b, w_in_p, qa_g, w_qb_p, kv_g, w_kv_p, lb_logits, cos, sin):
    T = xt.shape[0]
    pc, ps = _rope_placement()
    row = lambda w: pl.BlockSpec((TM_IN, w), lambda i: (i, 0))
    full = lambda a: pl.BlockSpec(a.shape, lambda i: (0, 0))
    outs = [(MLA_HEADS * HEAD_PAD, BF16), (MLA_HEADS * HEAD_PAD, BF16), (MLA_HEADS * HEAD_PAD, BF16),
            (HG_W, BF16), (HG_W, BF16), (HG_W, F32), (HG_W, BF16), (HG_W, BF16), (D_MODEL, F32)]
    return pl.pallas_call(
        _in_proj_body,
        grid=(T // TM_IN,),
        in_specs=[row(D_MODEL), full(ln_g), full(ln_b), *[full(w) for w in w_in_p], full(qa_g), full(w_qb_p),
                  full(kv_g), full(w_kv_p), full(lb_logits), row(HALF_ROPE), row(HALF_ROPE),
                  full(pc), full(ps)],
        out_specs=[row(w) for w, _ in outs],
        out_shape=[jax.ShapeDtypeStruct((T, w), d) for w, d in outs],
        compiler_params=_cparams(("parallel",)),
        name="in_proj",
    )(xt, ln_g, ln_b, *w_in_p, qa_g, w_qb_p, kv_g, w_kv_p, lb_logits, cos, sin, pc, ps)


def _attn_body(q_ref, k_ref, v_ref, o_ref):
    n_q = q_ref.shape[1] // TQ
    dn = (((1,), (1,)), ((), ()))
    row_c = lax.broadcasted_iota(I32, (TQ, TQ), 0) // CHUNK
    col_c = lax.broadcasted_iota(I32, (TQ, TQ), 1) // CHUNK
    diag_mask = col_c <= row_c
    heads = [slice(hh * HEAD_PAD, (hh + 1) * HEAD_PAD) for hh in range(2)]
    lane = lax.broadcasted_iota(I32, (TQ, 2 * MLA_V), 1)
    for qi in range(n_q):
        r0, kv = qi * TQ, (qi + 1) * TQ
        res = []
        for hs in heads:
            s = lax.dot_general(q_ref[0, r0:kv, hs], k_ref[0, :kv, hs], dn, preferred_element_type=F32)
            s_diag = jnp.where(diag_mask, s[:, r0:], -jnp.inf)
            m = jnp.max(s_diag, axis=-1, keepdims=True)
            if qi:
                m = jnp.maximum(m, jnp.max(s[:, :r0], axis=-1, keepdims=True))
                p = jnp.concatenate([jnp.exp2(s[:, :r0] - m), jnp.exp2(s_diag - m)], axis=1)
            else:
                p = jnp.exp2(s_diag - m)
            a = jnp.dot(p.astype(BF16), v_ref[0, :kv, hs], preferred_element_type=F32)
            res.append(a * (1.0 / a[:, MLA_V:MLA_V + 1]))
        o_ref[0, r0:kv, :] = jnp.where(lane < MLA_V, res[0], pltpu.roll(res[1], MLA_V, 1)).astype(BF16)


def _attention(q, k, v, B, S):
    w = MLA_HEADS * HEAD_PAD
    blk = pl.BlockSpec((1, S, 2 * HEAD_PAD), lambda b, h: (b, 0, h))
    out = pl.pallas_call(
        _attn_body,
        grid=(B, MLA_HEADS // 2),
        in_specs=[blk, blk, blk],
        out_specs=pl.BlockSpec((1, S, 2 * MLA_V), lambda b, h: (b, 0, h)),
        out_shape=jax.ShapeDtypeStruct((B, S, MLA_HEADS * MLA_V), BF16),
        compiler_params=_cparams(("parallel", "parallel")),
        name="mla_attention",
    )(q.reshape(B, S, w), k.reshape(B, S, w), v.reshape(B, S, w))
    return out.reshape(B * S, MLA_HEADS * MLA_V)


N_LEVELS = 6
Z_BLOCKS = N_LEVELS + 1


def _hgrn_constants():
    z = np.zeros((Z_BLOCKS, CHUNK, CHUNK), np.float32)
    masks = np.zeros((N_LEVELS, CHUNK, CHUNK), np.float32)
    idx = np.arange(CHUNK)
    for l in range(N_LEVELS):
        hs = CHUNK >> (l + 1)
        parent = idx // (2 * hs)
        right = (idx % (2 * hs)) >= hs
        mid = parent * 2 * hs + hs
        for t in range(CHUNK):
            z[l, t, :mid[t]] = 1.0
        masks[l] = (parent[:, None] == parent[None, :]) & right[:, None] & (~right)[None, :]
    z[N_LEVELS] = np.tril(np.ones((CHUNK, CHUNK), np.float32))
    z = z.reshape(Z_BLOCKS * CHUNK, CHUNK)
    tile_masks = np.zeros((N_LEVELS, LANES, LANES), np.float32)
    for c in range(LANES // CHUNK):
        tile_masks[:, c * CHUNK:(c + 1) * CHUNK, c * CHUNK:(c + 1) * CHUNK] = masks
    return np.concatenate([z, z], axis=1), tile_masks


def _hgrn_rows(r0, q_ref, k_ref, g_ref, v_ref, gate_ref, og_ref, z_ref, mask_ref, o_ref, state_ref):
    dn_t = (((1,), (1,)), ((), ()))
    n_c = HG_ROWS // CHUNK
    n_t = HG_ROWS // LANES
    rr = slice(r0, r0 + HG_ROWS)
    eye = (lax.broadcasted_iota(I32, (LANES, LANES), 0)
           == lax.broadcasted_iota(I32, (LANES, LANES), 1)).astype(F32)
    sums = []
    for c in range(n_c):
        g2 = g_ref[0, r0 + c * CHUNK:r0 + (c + 1) * CHUNK, :] * LOG2E
        g_hi, g_lo = _split_bf16(g2)
        sums.append(jnp.dot(z_ref[...], jnp.concatenate([g_hi, g_lo], axis=0),
                            preferred_element_type=F32))
    level = lambda l, cols: jnp.concatenate([s[l * CHUNK:(l + 1) * CHUNK, cols] for s in sums], axis=0)
    for h in range(HG_HEADS):
        cols = slice(h * HG_DK, (h + 1) * HG_DK)
        b = level(N_LEVELS, cols)
        q = q_ref[0, rr, cols].astype(F32)
        k = k_ref[0, rr, cols].astype(F32)
        v = v_ref[0, rr, cols]
        diag = jnp.sum(q * k, axis=-1, keepdims=True)
        a = [eye * diag[t * LANES:(t + 1) * LANES] for t in range(n_t)]
        for l in range(N_LEVELS):
            d = jnp.exp2(-jnp.abs(b - level(l, cols)))
            s = lax.dot_general((q * d).astype(BF16), (k * d).astype(BF16), dn_t,
                                preferred_element_type=F32)
            for t in range(n_t):
                ts = slice(t * LANES, (t + 1) * LANES)
                a[t] = a[t] + mask_ref[l] * s[ts, ts]
        o_intra = jnp.concatenate(
            [jnp.dot(a[t].astype(BF16), v[t * LANES:(t + 1) * LANES], preferred_element_type=F32)
             for t in range(n_t)], axis=0)
        d_in = jnp.exp2(b)
        qd = (q * d_in).astype(BF16)
        st = state_ref[h]
        o_parts = []
        for c in range(n_c):
            rows = slice(c * CHUNK, (c + 1) * CHUNK)
            last = slice((c + 1) * CHUNK - 1, (c + 1) * CHUNK)
            o_parts.append(o_intra[rows] + lax.dot_general(qd[rows], st.astype(BF16), dn_t,
                                                           preferred_element_type=F32))
            kh = (k[rows] * jnp.exp2(b[last] - b[rows])).astype(BF16)
            upd = lax.dot_general(v[rows], kh, (((0,), (0,)), ((), ())), preferred_element_type=F32)
            st = st * d_in[last] + upd
        state_ref[h] = st
        o = jnp.concatenate(o_parts, axis=0)
        o = _rms_norm(o, og_ref[:, cols]) * gate_ref[0, rr, cols].astype(F32)
        o_ref[0, rr, cols] = o.astype(BF16)


def _hgrn_body(q_ref, k_ref, g_ref, v_ref, gate_ref, og_ref, z_ref, mask_ref, o_ref, state_ref):
    @pl.when(pl.program_id(1) == 0)
    def _():
        state_ref[...] = jnp.zeros_like(state_ref)

    _hgrn_rows(0, q_ref, k_ref, g_ref, v_ref, gate_ref, og_ref, z_ref, mask_ref, o_ref, state_ref)


def _hgrn(hq, hk, hg, hv, gate, out_g, B, S):
    zc, masks = _hgrn_constants()
    zc = jnp.asarray(zc, BF16)
    masks = jnp.asarray(masks, F32)
    blk = pl.BlockSpec((1, HG_ROWS, HG_W), lambda b, i: (b, i, 0))
    r3 = lambda a: a.reshape(B, S, HG_W)
    out = pl.pallas_call(
        _hgrn_body,
        grid=(B, S // HG_ROWS),
        in_specs=[blk, blk, blk, blk, blk,
                  pl.BlockSpec((1, HG_W), lambda b, i: (0, 0)),
                  pl.BlockSpec(zc.shape, lambda b, i: (0, 0)),
                  pl.BlockSpec(masks.shape, lambda b, i: (0, 0, 0))],
        out_specs=blk,
        out_shape=jax.ShapeDtypeStruct((B, S, HG_W), BF16),
        scratch_shapes=[pltpu.VMEM((HG_HEADS, HG_DV, HG_DK), F32)],
        compiler_params=_cparams(("parallel", "arbitrary")),
        name="hgrn2",
    )(r3(hq), r3(hk), r3(hg), r3(hv), r3(gate), out_g, zc, masks)
    return out.reshape(B * S, HG_W)


def _pack_bf16_pair(lo, hi):
    lo_b = pltpu.bitcast(lo.astype(BF16).astype(F32), I32)
    hi_b = pltpu.bitcast(hi.astype(BF16).astype(F32), I32)
    return hi_b | lax.shift_right_logical(lo_b, 16)


def _unpack_bf16_pair(w):
    lo = pltpu.bitcast(lax.shift_left(w, 16), F32)
    hi = pltpu.bitcast(w & jnp.int32(-65536), F32)
    return lo, hi


def _out_proj_body(h_ref, attn_ref, hgo_ref, mg_ref, wo_ref, l1g_ref, l1b_ref, wr_ref, br_ref,
                   h1_out, h1b_out, pos_out, gate_out, cnt_out):
    for sub_tile in range(TM_OUT // TM_DISP):
        rs = slice(sub_tile * TM_DISP, (sub_tile + 1) * TM_DISP)
        _out_proj_tile(h_ref[rs, :], attn_ref[rs, :], hgo_ref[rs, :], mg_ref, wo_ref,
                       l1g_ref, l1b_ref, wr_ref, br_ref,
                       h1_out.at[rs, :], h1b_out.at[rs, :], pos_out.at[:, rs], gate_out.at[:, rs],
                       cnt_out.at[sub_tile])


def _out_proj_tile(h, attn, hgo, mg_ref, wo_ref, l1g_ref, l1b_ref, wr_ref, br_ref,
                   h1_out, h1b_out, pos_out, gate_out, cnt_out):
    attn = _rms_norm(attn.astype(F32), mg_ref[...]).astype(BF16)
    mix_in = jnp.concatenate([attn, hgo], axis=1)
    mix = jnp.dot(mix_in, wo_ref[...], preferred_element_type=F32)
    h1 = _layer_norm(DEEPNORM_ALPHA * h + mix, l1g_ref[...], l1b_ref[...])
    h1_out[...] = h1
    h1_hi, h1_lo = _split_bf16(h1)
    h1b_out[...] = h1_hi

    w_hi, w_lo = _split_bf16(wr_ref[...])
    nt_dims = (((1,), (1,)), ((), ()))
    logits_t = (lax.dot_general(w_hi, h1_hi, nt_dims, preferred_element_type=F32)
                + lax.dot_general(w_hi, h1_lo, nt_dims, preferred_element_type=F32)
                + lax.dot_general(w_lo, h1_hi, nt_dims, preferred_element_type=F32)) + br_ref[...]
    tm = h1.shape[0]
    sub = lax.broadcasted_iota(I32, (N_EXPERTS, tm), 0)
    work = logits_t
    vals, idxs = [], []
    for _ in range(TOP_K):
        mx = jnp.max(work, axis=0, keepdims=True)
        ix = jnp.min(jnp.where(work == mx, sub, N_EXPERTS), axis=0, keepdims=True)
        vals.append(mx)
        idxs.append(ix)
        work = jnp.where(sub == ix, -jnp.inf, work)
    exps = [jnp.exp(vl - vals[0]) for vl in vals]
    den = exps[0] + exps[1] + exps[2] + exps[3]

    sel = jnp.zeros((N_EXPERTS, tm), F32)
    for ix in idxs:
        sel = sel + (sub == ix).astype(F32)
    selb = sel.astype(BF16)
    earlier = (lax.broadcasted_iota(I32, (tm, tm), 0)
               < lax.broadcasted_iota(I32, (tm, tm), 1)).astype(BF16)
    lower = (lax.broadcasted_iota(I32, (N_EXPERTS, N_EXPERTS), 1)
             < lax.broadcasted_iota(I32, (N_EXPERTS, N_EXPERTS), 0)).astype(BF16)
    cnt = jnp.sum(sel, axis=1, keepdims=True)
    run = jnp.floor((cnt + (SUBLANES - 1)) * (1.0 / SUBLANES)) * SUBLANES
    run_b = jnp.broadcast_to(run, (N_EXPERTS, LANES)).astype(BF16)
    seg_off = jnp.dot(lower, run_b, preferred_element_type=F32)[:, 0:1]
    posmat = jnp.dot(selb, earlier, preferred_element_type=F32) + seg_off
    pos = [jnp.sum(jnp.where(sub == ix, posmat, 0.0), axis=0, keepdims=True) for ix in idxs]
    pos_out[...] = jnp.concatenate(pos, axis=0).astype(I32)
    gsel = jnp.zeros((N_EXPERTS, tm), F32)
    for ix, ex in zip(idxs, exps):
        gsel = gsel + jnp.where(sub == ix, ex / den, 0.0)
    gate_out[...] = gsel
    cnt_out[...] = cnt


def _out_proj(h, attn, hgo, mla_g, w_o, l1g, l1b, w_r, b_r):
    T = h.shape[0]
    row = lambda w: pl.BlockSpec((TM_OUT, w), lambda i: (i, 0))
    full = lambda a: pl.BlockSpec(a.shape, lambda i: (0, 0))
    col = pl.BlockSpec((TOP_K, TM_OUT), lambda i: (0, i))
    sub_tiles = TM_OUT // TM_DISP
    return pl.pallas_call(
        _out_proj_body,
        grid=(T // TM_OUT,),
        in_specs=[row(D_MODEL), row(MLA_HEADS * MLA_V), row(HG_W), full(mla_g),
                  full(w_o), full(l1g), full(l1b), full(w_r), full(b_r)],
        out_specs=[row(D_MODEL), row(D_MODEL), col, pl.BlockSpec((N_EXPERTS, TM_OUT), lambda i: (0, i)),
                   pl.BlockSpec((sub_tiles, N_EXPERTS, 1), lambda i: (i, 0, 0))],
        out_shape=[jax.ShapeDtypeStruct((T, D_MODEL), F32), jax.ShapeDtypeStruct((T, D_MODEL), BF16),
                   jax.ShapeDtypeStruct((TOP_K, T), I32), jax.ShapeDtypeStruct((N_EXPERTS, T), F32),
                   jax.ShapeDtypeStruct((T // TM_DISP, N_EXPERTS, 1), F32)],
        compiler_params=_cparams(("parallel",)),
        name="out_proj_router",
    )(h, attn, hgo, mla_g, w_o, l1g, l1b, w_r, b_r)


TILE_SLOTS = TOP_K * TM_DISP + N_EXPERTS * SUBLANES
ROW_WORDS = D_MODEL // 2 + LANES
TILES_PER_STEP = 2
TM_STEP = TILES_PER_STEP * TM_DISP
RUN_UNROLL = 4

def _ceil_to(n, m):
    return (n + m - 1) // m * m


def _rows(start, size):
    return pl.ds(pl.multiple_of(start, SUBLANES), size)


def _for_each_piece(n, fn, max_rows=TM_DISP):
    for kbit in reversed(range(SUBLANES.bit_length() - 1, max_rows.bit_length())):
        size = 1 << kbit

        @pl.when((n & size) != 0)
        def _():
            fn(pl.multiple_of((n >> (kbit + 1)) << (kbit + 1), SUBLANES), size)


def _pack_exact_pair(lo, hi):
    return pltpu.bitcast(hi, I32) | lax.shift_right_logical(pltpu.bitcast(lo, I32), 16)


def _zero_tail_blocks(zeros_ref, out_hbm, n_used, zsem):
    def blk_copy(g):
        return pltpu.make_async_copy(zeros_ref.at[pl.ds(0, ROW_UNIT)],
                                     out_hbm.at[_rows(g * ROW_UNIT, ROW_UNIT)], zsem)

    n_total = out_hbm.shape[0] // ROW_UNIT
    lax.fori_loop(n_used, n_total, lambda g, c: (blk_copy(g).start(), c)[1], 0)
    lax.fori_loop(n_used, n_total, lambda g, c: (blk_copy(g).wait(), c)[1], 0)


def _dispatch_body(segn_ref, segsrc_ref, segdst_ref, tot_ref, padn_ref, padd_ref, nused_ref, pos_ref, xb_ref, g_ref,
                   xs_out, stage, zeros_ref, sem, zsem):
    i = pl.program_id(0)

    @pl.when(i == 0)
    def _():
        zeros_ref[...] = jnp.zeros_like(zeros_ref)

        def pad_copy(e, off, size):
            return pltpu.make_async_copy(zeros_ref.at[pl.ds(0, size)],
                                         xs_out.at[_rows(padd_ref[e] + off, size)], zsem)

        def fill(e, c):
            _for_each_piece(padn_ref[e], lambda off, size: pad_copy(e, off, size).start())
            return c

        def drain(e, c):
            _for_each_piece(padn_ref[e], lambda off, size: pad_copy(e, off, size).wait())
            return c

        lax.fori_loop(0, N_EXPERTS, fill, 0)
        lax.fori_loop(0, N_EXPERTS, drain, 0)
        _zero_tail_blocks(zeros_ref, xs_out, nused_ref[0], zsem)

    def settle(tile, b):
        _for_each_piece(tot_ref[tile], lambda off, size: pltpu.make_async_copy(
            stage.at[b, pl.ds(0, size)], xs_out.at[pl.ds(0, size)], sem.at[b]).wait(), TILE_SLOTS)

    for b in range(TILES_PER_STEP):
        tile = i * TILES_PER_STEP + b
        ts = slice(b * TM_DISP, (b + 1) * TM_DISP)
        pos = pos_ref[:, ts]
        slot = lax.broadcasted_iota(I32, (TILE_SLOTS, TM_DISP), 0)
        sel = jnp.zeros((TILE_SLOTS, TM_DISP), F32)
        for j in range(TOP_K):
            sel = sel + (slot == pos[j:j + 1, :]).astype(F32)
        g_hi, g_lo = _split_bf16(g_ref[:, ts])
        e_sub = lax.broadcasted_iota(I32, (N_EXPERTS, LANES), 0)
        e_lane = lax.broadcasted_iota(I32, (N_EXPERTS, LANES), 1)
        tn_dims = (((0,), (0,)), ((), ()))
        gx = (lax.dot_general(g_hi, (e_lane == e_sub).astype(BF16), tn_dims, preferred_element_type=F32)
              + lax.dot_general(g_lo, (e_lane == e_sub + N_EXPERTS).astype(BF16), tn_dims,
                                preferred_element_type=F32))
        src_rows = jnp.concatenate([xb_ref[ts, :], gx.astype(BF16)], axis=1)
        rows = jnp.dot(sel.astype(BF16), src_rows, preferred_element_type=F32)
        packed = jnp.concatenate(
            [_pack_exact_pair(rows[:, :D_MODEL // 2], rows[:, D_MODEL // 2:D_MODEL]),
             pltpu.bitcast(rows[:, D_MODEL:], I32)], axis=1)

        @pl.when(i >= 1)
        def _():
            settle(tile - TILES_PER_STEP, b)

        stage[b] = packed

        def send(eb, c):
            for u in range(RUN_UNROLL):
                k = tile * N_EXPERTS + eb * RUN_UNROLL + u
                src, dst = segsrc_ref[k], segdst_ref[k]
                _for_each_piece(segn_ref[k], lambda off, size: pltpu.make_async_copy(
                    stage.at[b, _rows(src + off, size)], xs_out.at[_rows(dst + off, size)],
                    sem.at[b]).start())
            return c

        lax.fori_loop(0, N_EXPERTS // RUN_UNROLL, send, 0)

    @pl.when(i == pl.num_programs(0) - 1)
    def _():
        for b in range(TILES_PER_STEP):
            settle(i * TILES_PER_STEP + b, b)


def _dispatch(seg_n, seg_src, seg_dst, tile_tot, pad_n, pad_dst, n_used, pos_t, h1b, gates_x, n_rows):
    T = h1b.shape[0]
    grid_spec = pltpu.PrefetchScalarGridSpec(
        num_scalar_prefetch=7,
        grid=(T // TM_STEP,),
        in_specs=[pl.BlockSpec((TOP_K, TM_STEP), lambda i, *_: (0, i)),
                  pl.BlockSpec((TM_STEP, D_MODEL), lambda i, *_: (i, 0)),
                  pl.BlockSpec((N_EXPERTS, TM_STEP), lambda i, *_: (0, i))],
        out_specs=pl.BlockSpec(memory_space=pl.ANY),
        scratch_shapes=[pltpu.VMEM((TILES_PER_STEP, TILE_SLOTS, ROW_WORDS), I32),
                        pltpu.VMEM((ROW_BLOCK, ROW_WORDS), I32),
                        pltpu.SemaphoreType.DMA((TILES_PER_STEP,)), pltpu.SemaphoreType.DMA(())],
    )
    return pl.pallas_call(
        _dispatch_body,
        grid_spec=grid_spec,
        out_shape=jax.ShapeDtypeStruct((n_rows, ROW_WORDS), I32),
        compiler_params=_cparams(("arbitrary",)),
        name="moe_dispatch",
    )(seg_n, seg_src, seg_dst, tile_tot, pad_n, pad_dst, n_used, pos_t, h1b, gates_x)


def _experts_body(istart_ref, icount_ref, iunit_ref, ibig_ref, nitems_ref, uused_ref,
                  xs_hbm, wu_ref, bu_ref, wd_ref, bd_ref, ys_hbm,
                  wu_bf, wd_bf, xbuf, ybuf, sem_in, sem_out):
    e = pl.program_id(0)
    n_items = nitems_ref[0]
    nb = icount_ref[e]

    def by_size(k, fn):
        full = ibig_ref[k] == 1
        pl.when(full)(lambda: fn(ROW_BLOCK))
        pl.when(jnp.logical_not(full))(lambda: fn(ROW_UNIT))

    def fetch(k, slot, size):
        rows = _rows(iunit_ref[k] * ROW_UNIT, size)
        return pltpu.make_async_copy(xs_hbm.at[rows], xbuf.at[slot, pl.ds(0, size)], sem_in.at[slot])

    def put(k, slot, size):
        rows = _rows(iunit_ref[k] * ROW_UNIT, size)
        return pltpu.make_async_copy(ybuf.at[slot, pl.ds(0, size)], ys_hbm.at[rows], sem_out.at[slot])

    @pl.when(jnp.logical_and(e == 0, n_items > 0))
    def _():
        by_size(0, lambda size: fetch(0, 0, size).start())

    @pl.when(nb > 0)
    def _():
        wu_bf[...] = wu_ref[0].astype(BF16)
        wd_bf[...] = wd_ref[0].astype(BF16)

    def compute(k, slot, size):
        row = xbuf[slot, :size]
        lo, hi = _unpack_bf16_pair(row[:, :D_MODEL // 2])
        x = jnp.concatenate([lo, hi], axis=1).astype(BF16)
        lane = lax.broadcasted_iota(I32, (size, LANES), 1)
        mine = jnp.logical_or(lane == e, lane == e + N_EXPERTS)
        gate = jnp.sum(jnp.where(mine, pltpu.bitcast(row[:, D_MODEL // 2:], F32), 0.0),
                       axis=1, keepdims=True)
        hb = jnp.dot(x, wu_bf[...], preferred_element_type=F32) + bu_ref[0]
        glu = jnp.minimum(hb[:, :D_EXPERT], SWIGLU_LIMIT)
        lin = jnp.clip(hb[:, D_EXPERT:], -SWIGLU_LIMIT, SWIGLU_LIMIT)
        act = glu * (1.0 / (1.0 + jnp.exp(-SWIGLU_ALPHA * glu))) * (lin + 1.0)
        y = (jnp.dot(act.astype(BF16), wd_bf[...], preferred_element_type=F32) + bd_ref[0]) * gate
        ybuf[slot, :size] = _pack_bf16_pair(y[:, :D_MODEL // 2], y[:, D_MODEL // 2:])
        put(k, slot, size).start()

    def block(b, c):
        k = istart_ref[e] + b
        slot = lax.rem(k, 2)
        by_size(k, lambda size: fetch(k, slot, size).wait())

        @pl.when(k + 1 < n_items)
        def _():
            by_size(k + 1, lambda size: fetch(k + 1, 1 - slot, size).start())

        @pl.when(k >= 2)
        def _():
            by_size(k - 2, lambda size: put(k - 2, slot, size).wait())

        by_size(k, lambda size: compute(k, slot, size))
        return c

    lax.fori_loop(0, nb, block, 0)

    @pl.when(e == N_EXPERTS - 1)
    def _():
        for back in (2, 1):
            @pl.when(n_items >= back)
            def _():
                k = n_items - back
                by_size(k, lambda size: put(k, lax.rem(k, 2), size).wait())
        ybuf[0] = jnp.zeros_like(ybuf[0])
        _zero_tail_blocks(ybuf.at[0], ys_hbm, uused_ref[0], sem_out.at[0])


def _experts(item_start, item_count, item_unit, item_big, n_items, units_used, xs,
             w_up, b_up, w_down, b_down):
    n_rows = xs.shape[0]
    grid_spec = pltpu.PrefetchScalarGridSpec(
        num_scalar_prefetch=6,
        grid=(N_EXPERTS,),
        in_specs=[pl.BlockSpec(memory_space=pl.ANY),
                  pl.BlockSpec((1, D_MODEL, 2 * D_EXPERT), lambda e, *_: (e, 0, 0)),
                  pl.BlockSpec((1, 1, 2 * D_EXPERT), lambda e, *_: (e, 0, 0)),
                  pl.BlockSpec((1, D_EXPERT, D_MODEL), lambda e, *_: (e, 0, 0)),
                  pl.BlockSpec((1, 1, D_MODEL), lambda e, *_: (e, 0, 0))],
        out_specs=pl.BlockSpec(memory_space=pl.ANY),
        scratch_shapes=[pltpu.VMEM((D_MODEL, 2 * D_EXPERT), BF16), pltpu.VMEM((D_EXPERT, D_MODEL), BF16),
                        pltpu.VMEM((2, ROW_BLOCK, ROW_WORDS), I32),
                        pltpu.VMEM((2, ROW_BLOCK, D_MODEL // 2), I32),
                        pltpu.SemaphoreType.DMA((2,)), pltpu.SemaphoreType.DMA((2,))],
    )
    return pl.pallas_call(
        _experts_body,
        grid_spec=grid_spec,
        out_shape=jax.ShapeDtypeStruct((n_rows, D_MODEL // 2), I32),
        compiler_params=_cparams(("arbitrary",)),
        name="moe_experts",
    )(item_start, item_count, item_unit, item_big, n_items, units_used, xs, w_up,
      b_up.reshape(N_EXPERTS, 1, 2 * D_EXPERT), w_down, b_down.reshape(N_EXPERTS, 1, D_MODEL))


def _combine_body(segn_ref, segsrc_ref, segdst_ref, tot_ref, pos_ref, h1_ref, l2g_ref, l2b_ref, ys_hbm,
                  o_ref, stage, sem):
    i = pl.program_id(0)
    n_tiles = pl.num_programs(0) * TILES_PER_STEP

    def fetch_tile(tile, b):
        def fetch(eb, c):
            for u in range(RUN_UNROLL):
                k = tile * N_EXPERTS + eb * RUN_UNROLL + u
                src, dst = segsrc_ref[k], segdst_ref[k]
                _for_each_piece(segn_ref[k], lambda off, size: pltpu.make_async_copy(
                    ys_hbm.at[_rows(dst + off, size)], stage.at[b, _rows(src + off, size)],
                    sem.at[b]).start())
            return c

        lax.fori_loop(0, N_EXPERTS // RUN_UNROLL, fetch, 0)

    @pl.when(i == 0)
    def _():
        stage[...] = jnp.zeros_like(stage)
        fetch_tile(0, 0)

    for b in range(TILES_PER_STEP):
        tile = i * TILES_PER_STEP + b
        nxt = (b + 1) % TILES_PER_STEP

        @pl.when(tile + 1 < n_tiles)
        def _():
            fetch_tile(tile + 1, nxt)

        ts = slice(b * TM_DISP, (b + 1) * TM_DISP)
        pos = pos_ref[ts, :]
        slot = lax.broadcasted_iota(I32, (TM_DISP, TILE_SLOTS), 1)
        w = jnp.zeros((TM_DISP, TILE_SLOTS), F32)
        for j in range(TOP_K):
            w = w + (slot == pos[:, j:j + 1]).astype(F32)
        w = w.astype(BF16)
        _for_each_piece(tot_ref[tile], lambda off, size: pltpu.make_async_copy(
            ys_hbm.at[pl.ds(0, size)], stage.at[b, pl.ds(0, size)], sem.at[b]).wait(), TILE_SLOTS)
        lo, hi = _unpack_bf16_pair(stage[b])
        y = jnp.concatenate([lo, hi], axis=1).astype(BF16)
        ffn = jnp.dot(w, y, preferred_element_type=F32)
        o_ref[ts, :] = _layer_norm(DEEPNORM_ALPHA * h1_ref[ts, :] + ffn, l2g_ref[...], l2b_ref[...])


def _combine(seg_n, seg_src, seg_dst, tile_tot, pos, h1, l2g, l2b, ys):
    T = h1.shape[0]
    grid_spec = pltpu.PrefetchScalarGridSpec(
        num_scalar_prefetch=4,
        grid=(T // TM_STEP,),
        in_specs=[pl.BlockSpec((TM_STEP, TOP_K), lambda i, *_: (i, 0)),
                  pl.BlockSpec((TM_STEP, D_MODEL), lambda i, *_: (i, 0)),
                  pl.BlockSpec(l2g.shape, lambda i, *_: (0, 0)),
                  pl.BlockSpec(l2b.shape, lambda i, *_: (0, 0)),
                  pl.BlockSpec(memory_space=pl.ANY)],
        out_specs=pl.BlockSpec((TM_STEP, D_MODEL), lambda i, *_: (i, 0)),
        scratch_shapes=[pltpu.VMEM((TILES_PER_STEP, TILE_SLOTS, D_MODEL // 2), I32),
                        pltpu.SemaphoreType.DMA((TILES_PER_STEP,))],
    )
    return pl.pallas_call(
        _combine_body,
        grid_spec=grid_spec,
        out_shape=jax.ShapeDtypeStruct((T, D_MODEL), F32),
        compiler_params=_cparams(("arbitrary",)),
        name="moe_combine",
    )(seg_n, seg_src, seg_dst, tile_tot, pos, h1, l2g, l2b, ys)


def _prep_weights(w_in, w_q_b, w_kv_b):
    w_kr = jnp.pad(w_in[:, KR_OFF:KR_OFF + MLA_ROPE], ((0, 0), (MLA_NOPE, HEAD_PAD - MLA_NOPE - MLA_ROPE)))
    w_in_p = (jnp.concatenate([w_in[:, :KR_OFF], w_kr], axis=1).astype(BF16),
              w_in[:, KR_OFF + MLA_ROPE:].astype(BF16))
    qd = MLA_NOPE + MLA_ROPE
    wq = w_q_b.reshape(Q_LORA, MLA_HEADS, qd)
    wq = jnp.pad(wq, ((0, 0), (0, 0), (0, HEAD_PAD - qd))).reshape(Q_LORA, MLA_HEADS * HEAD_PAD)
    wkv = w_kv_b.reshape(KV_LORA, MLA_HEADS, MLA_NOPE + MLA_V)
    wk = jnp.pad(wkv[:, :, :MLA_NOPE], ((0, 0), (0, 0), (0, HEAD_PAD - MLA_NOPE)))
    wk = wk.reshape(KV_LORA, MLA_HEADS * HEAD_PAD)
    wv = jnp.pad(wkv[:, :, MLA_NOPE:], ((0, 0), (0, 0), (0, HEAD_PAD - MLA_V)))
    wv = wv.reshape(KV_LORA, MLA_HEADS * HEAD_PAD)
    return w_in_p, wq.astype(BF16), jnp.concatenate([wk, wv], axis=1).astype(BF16)


def kernel(x, positions, ln_in_g, ln_in_b, w_in, q_a_norm_g, w_q_b, kv_a_norm_g, w_kv_b, hgrn_lb_logits,
           mla_out_g, hgrn_out_g, w_o, ln1_g, ln1_b, w_router, b_router, w_up, b_up, w_down, b_down,
           ln2_g, ln2_b):
    B, S, D = x.shape
    assert D == D_MODEL and w_in.shape[0] == DEPTH and hgrn_lb_logits.shape[0] == DEPTH + 1
    T = B * S
    xt = x.reshape(T, D)
    r2 = lambda a: a.reshape(1, -1)

    cos, sin = _rope_cos_sin(positions)
    w_in_p, w_qb_p, w_kv_p = _prep_weights(w_in[0], w_q_b[0], w_kv_b[0])
    q, k, v, hq, hk, hg, hv, gate, h0 = _in_proj(
        xt, r2(ln_in_g), r2(ln_in_b), w_in_p, q_a_norm_g, w_qb_p, kv_a_norm_g, w_kv_p,
        hgrn_lb_logits, cos, sin)
    attn = _attention(q, k, v, B, S)
    hgo = _hgrn(hq, hk, hg, hv, gate, hgrn_out_g, B, S)
    h1, h1b, pos_t, gate_t, tile_cnt = _out_proj(
        h0, attn, hgo, mla_out_g, w_o[0].astype(BF16), ln1_g, ln1_b,
        w_router[0].T, b_router.reshape(N_EXPERTS, 1))

    n_tiles = T // TM_DISP
    n_rows = T * TOP_K + n_tiles * N_EXPERTS * (SUBLANES - 1) + N_EXPERTS * ROW_UNIT
    n_rows = _ceil_to(n_rows, ROW_BLOCK)
    tc = _ceil_to(tile_cnt[:, :, 0].astype(I32), SUBLANES)
    cnt = jnp.sum(tc, axis=0)
    padded = _ceil_to(cnt, ROW_UNIT)
    pad_start = jnp.cumsum(padded) - padded
    seg_n = tc.reshape(-1)
    seg_src = (jnp.cumsum(tc, axis=1) - tc).reshape(-1)
    seg_dst = (pad_start[None, :] + jnp.cumsum(tc, axis=0) - tc).reshape(-1)
    units_used = jnp.sum(padded, keepdims=True) // ROW_UNIT

    units = padded // ROW_UNIT
    item_count = (units + 1) // 2
    item_end = jnp.cumsum(item_count)
    item_start = item_end - item_count
    k = jnp.arange(n_rows // ROW_BLOCK + N_EXPERTS, dtype=I32)
    of_expert = (jnp.minimum(jnp.sum((item_end[None, :] <= k[:, None]).astype(I32), axis=1), N_EXPERTS - 1)
                 [:, None] == jnp.arange(N_EXPERTS, dtype=I32)[None, :])
    pick = lambda v: jnp.sum(jnp.where(of_expert, v[None, :], 0), axis=1)
    first_unit = 2 * (k - pick(item_start))
    item_unit = pick(pad_start // ROW_UNIT) + first_unit
    item_big = (first_unit + 2 <= pick(units)).astype(I32)

    tile_tot = jnp.sum(tc, axis=1)
    xs = _dispatch(seg_n, seg_src, seg_dst, tile_tot, padded - cnt, pad_start + cnt, units_used, pos_t, h1b,
                   gate_t, n_rows)
    ys = _experts(item_start, item_count, item_unit, item_big, item_end[-1:], units_used, xs,
                  w_up[0], b_up[0], w_down[0], b_down[0])
    out = _combine(seg_n, seg_src, seg_dst, tile_tot, pos_t.T, h1, ln2_g, ln2_b, ys)
    return out.reshape(B, S, D)
```

```python
import numpy as np
import jax
import jax.numpy as jnp
from jax import lax
from jax.experimental import pallas as pl
from jax.experimental.pallas import tpu as pltpu

F32 = jnp.float32
BF16 = jnp.bfloat16
I32 = jnp.int32

D_MODEL = 1024
CHUNK = 64
MLA_HEADS = 8
MLA_NOPE = 64
MLA_ROPE = 32
MLA_V = 64
Q_LORA = 384
KV_LORA = 256
ROPE_THETA = 10000.0
HG_HEADS = 4
HG_DK = 128
HG_DV = 128
HG_W = HG_HEADS * HG_DK
N_EXPERTS = 32
TOP_K = 4
D_EXPERT = 1024
SWIGLU_LIMIT = 7.0
SWIGLU_ALPHA = 1.702
LN_EPS = 1e-5
RMS_EPS = 1e-6
DEPTH = 1
DEEPNORM_ALPHA = (2.0 * DEPTH) ** 0.25
LOG2E = 1.4426950408889634

LANES = 128
SUBLANES = 8
HEAD_PAD = 128
HALF_ROPE = MLA_ROPE // 2
KR_OFF = Q_LORA + KV_LORA
HQ_OFF = KR_OFF + HEAD_PAD
IN_COLS_PAD = HQ_OFF + 4 * HG_W

TM_IN = 512
TQ = 512
HG_ROWS = 256
TM_OUT = 1024
TM_DISP = 256
ROW_BLOCK = 512
ROW_UNIT = ROW_BLOCK // 2
VMEM_LIMIT = 52 * 1024 * 1024


def _cparams(sem, vmem=VMEM_LIMIT):
    return pltpu.CompilerParams(dimension_semantics=sem, vmem_limit_bytes=vmem)


def _layer_norm(x, g, b):
    mu = jnp.mean(x, axis=-1, keepdims=True)
    xc = x - mu
    var = jnp.mean(xc * xc, axis=-1, keepdims=True)
    return xc * lax.rsqrt(var + LN_EPS) * g + b


def _rms_norm(x, g):
    return x * lax.rsqrt(jnp.mean(x * x, axis=-1, keepdims=True) + RMS_EPS) * g


def _split_bf16(a):
    hi = a.astype(BF16)
    return hi, (a - hi.astype(F32)).astype(BF16)


def _sigmoid_pair(x):
    e = jnp.exp(-jnp.abs(x))
    big = 1.0 / (1.0 + e)
    small = e * big
    pos = x >= 0
    return jnp.where(pos, big, small), jnp.where(pos, small, big)


def _cos_sin_body(ang_ref, cos_ref, sin_ref):
    a = ang_ref[...]
    cos_ref[...] = jnp.cos(a)
    sin_ref[...] = jnp.sin(a)


def _rope_cos_sin(positions):
    T = positions.size
    inv_freq = ROPE_THETA ** (-jnp.arange(0, MLA_ROPE, 2, dtype=F32) / MLA_ROPE)
    ang = (positions.astype(F32)[..., None] * inv_freq).reshape(T * HALF_ROPE // LANES, LANES)
    cos, sin = pl.pallas_call(
        _cos_sin_body,
        out_shape=(jax.ShapeDtypeStruct(ang.shape, F32),) * 2,
        name="rope_cos_sin",
    )(ang)
    return cos.reshape(T, HALF_ROPE), sin.reshape(T, HALF_ROPE)


def _rope_placement():
    pc = np.zeros((HALF_ROPE, HEAD_PAD), np.float32)
    ps = np.zeros((HALF_ROPE, 2 * HEAD_PAD), np.float32)
    for j in range(HALF_ROPE):
        pc[j, MLA_NOPE + j] = pc[j, MLA_NOPE + HALF_ROPE + j] = 1.0
        ps[j, MLA_NOPE + HALF_ROPE + j] = 1.0
        ps[j, HEAD_PAD + MLA_NOPE + j] = -1.0
    return jnp.asarray(pc, BF16), jnp.asarray(ps, BF16)


def _rope_tables(cos, sin, pc, ps):
    place = lambda a, p: sum(jnp.dot(part, p, preferred_element_type=F32) for part in _split_bf16(a))
    lane = lax.broadcasted_iota(I32, (1, HEAD_PAD), 1)
    c = place(cos, pc) + (lane < MLA_NOPE).astype(F32)
    s = place(sin, ps)
    return c, s[:, :HEAD_PAD], s[:, HEAD_PAD:]


def _rope(x, c, s1, s2, reps):
    width = reps * HEAD_PAD
    if reps > 1:
        c = jnp.concatenate([c] * reps, axis=1)
        s1 = jnp.concatenate([s1] * reps, axis=1)
        s2 = jnp.concatenate([s2] * reps, axis=1)
    return (x * c + pltpu.roll(x, HALF_ROPE, 1) * s1
            + pltpu.roll(x, width - HALF_ROPE, 1) * s2)


def _in_proj_body(x_ref, lng_ref, lnb_ref, wa_ref, wh_ref, qag_ref, wqb_ref, kvg_ref, wkv_ref, lbl_ref,
                  cos_ref, sin_ref, pc_ref, ps_ref,
                  q_out, k_out, v_out, hq_out, hk_out, hg_out, hv_out, gate_out, h_out):
    win_refs = (wa_ref, wh_ref)
    h = _layer_norm(x_ref[...], lng_ref[...], lnb_ref[...])
    h_out[...] = h
    hb = h.astype(BF16)
    p = jnp.concatenate([jnp.dot(hb, w_ref[...], preferred_element_type=F32) for w_ref in win_refs],
                        axis=1)
    c, s1, s2 = _rope_tables(cos_ref[...], sin_ref[...], pc_ref[...], ps_ref[...])

    qn = _rms_norm(p[:, :Q_LORA], qag_ref[...]).astype(BF16)
    q = jnp.dot(qn, wqb_ref[...], preferred_element_type=F32)
    scale = (MLA_NOPE + MLA_ROPE) ** -0.5 * LOG2E
    q_out[...] = (_rope(q, c, s1, s2, MLA_HEADS) * scale).astype(BF16)
    kvn = _rms_norm(p[:, Q_LORA:KR_OFF], kvg_ref[...]).astype(BF16)
    kv = jnp.dot(kvn, wkv_ref[...], preferred_element_type=F32)
    kr = _rope(p[:, KR_OFF:HQ_OFF], c, s1, s2, 1)
    k_out[...] = (kv[:, :MLA_HEADS * HEAD_PAD] + jnp.concatenate([kr] * MLA_HEADS, axis=1)).astype(BF16)
    lane = lax.broadcasted_iota(I32, (1, MLA_HEADS * HEAD_PAD), 1)
    ones_lane = (lane % HEAD_PAD == MLA_V).astype(F32)
    v_out[...] = (kv[:, MLA_HEADS * HEAD_PAD:] + ones_lane).astype(BF16)

    lbl = lbl_ref[...]
    le = jnp.exp(lbl - jnp.max(lbl, axis=0, keepdims=True))
    lb = le[0:1] / jnp.sum(le, axis=0, keepdims=True)
    hq = p[:, HQ_OFF:HQ_OFF + HG_W]
    hf = p[:, HQ_OFF + HG_W:HQ_OFF + 2 * HG_W]
    hi = p[:, HQ_OFF + 2 * HG_W:HQ_OFF + 3 * HG_W]
    hg = p[:, HQ_OFF + 3 * HG_W:HQ_OFF + 4 * HG_W]
    hq_out[...] = (hq * _sigmoid_pair(hq)[0]).astype(BF16)
    sg, one_minus_sg = _sigmoid_pair(hf)
    hk_out[...] = ((1.0 - lb) * one_minus_sg).astype(BF16)
    hg_out[...] = jnp.log(lb + (1.0 - lb) * sg)
    hv_out[...] = hi.astype(BF16)
    gate_out[...] = (hg * _sigmoid_pair(hg)[0]).astype(BF16)


def _in_proj(xt, ln_g, ln_b, w_in_p, qa_g, w_qb_p, kv_g, w_kv_p, lb_logits, cos, sin):
    T = xt.shape[0]
    pc, ps = _rope_placement()
    row = lambda w: pl.BlockSpec((TM_IN, w), lambda i: (i, 0))
    full = lambda a: pl.BlockSpec(a.shape, lambda i: (0, 0))
    outs = [(MLA_HEADS * HEAD_PAD, BF16), (MLA_HEADS * HEAD_PAD, BF16), (MLA_HEADS * HEAD_PAD, BF16),
            (HG_W, BF16), (HG_W, BF16), (HG_W, F32), (HG_W, BF16), (HG_W, BF16), (D_MODEL, F32)]
    return pl.pallas_call(
        _in_proj_body,
        grid=(T // TM_IN,),
        in_specs=[row(D_MODEL), full(ln_g), full(ln_b), *[full(w) for w in w_in_p], full(qa_g), full(w_qb_p),
                  full(kv_g), full(w_kv_p), full(lb_logits), row(HALF_ROPE), row(HALF_ROPE),
                  full(pc), full(ps)],
        out_specs=[row(w) for w, _ in outs],
        out_shape=[jax.ShapeDtypeStruct((T, w), d) for w, d in outs],
        compiler_params=_cparams(("parallel",)),
        name="in_proj",
    )(xt, ln_g, ln_b, *w_in_p, qa_g, w_qb_p, kv_g, w_kv_p, lb_logits, cos, sin, pc, ps)


def _attn_body(q_ref, k_ref, v_ref, o_ref):
    n_q = q_ref.shape[1] // TQ
    dn = (((1,), (1,)), ((), ()))
    row_c = lax.broadcasted_iota(I32, (TQ, TQ), 0) // CHUNK
    col_c = lax.broadcasted_iota(I32, (TQ, TQ), 1) // CHUNK
    diag_mask = col_c <= row_c
    heads = [slice(hh * HEAD_PAD, (hh + 1) * HEAD_PAD) for hh in range(2)]
    lane = lax.broadcasted_iota(I32, (TQ, 2 * MLA_V), 1)
    for qi in range(n_q):
        r0, kv = qi * TQ, (qi + 1) * TQ
        res = []
        for hs in heads:
            s = lax.dot_general(q_ref[0, r0:kv, hs], k_ref[0, :kv, hs], dn, preferred_element_type=F32)
            s_diag = jnp.where(diag_mask, s[:, r0:], -jnp.inf)
            m = jnp.max(s_diag, axis=-1, keepdims=True)
            if qi:
                m = jnp.maximum(m, jnp.max(s[:, :r0], axis=-1, keepdims=True))
                p = jnp.concatenate([jnp.exp2(s[:, :r0] - m), jnp.exp2(s_diag - m)], axis=1)
            else:
                p = jnp.exp2(s_diag - m)
            a = jnp.dot(p.astype(BF16), v_ref[0, :kv, hs], preferred_element_type=F32)
            res.append(a * (1.0 / a[:, MLA_V:MLA_V + 1]))
        o_ref[0, r0:kv, :] = jnp.where(lane < MLA_V, res[0], pltpu.roll(res[1], MLA_V, 1)).astype(BF16)


def _attention(q, k, v, B, S):
    w = MLA_HEADS * HEAD_PAD
    blk = pl.BlockSpec((1, S, 2 * HEAD_PAD), lambda b, h: (b, 0, h))
    out = pl.pallas_call(
        _attn_body,
        grid=(B, MLA_HEADS // 2),
        in_specs=[blk, blk, blk],
        out_specs=pl.BlockSpec((1, S, 2 * MLA_V), lambda b, h: (b, 0, h)),
        out_shape=jax.ShapeDtypeStruct((B, S, MLA_HEADS * MLA_V), BF16),
        compiler_params=_cparams(("parallel", "parallel")),
        name="mla_attention",
    )(q.reshape(B, S, w), k.reshape(B, S, w), v.reshape(B, S, w))
    return out.reshape(B * S, MLA_HEADS * MLA_V)


N_LEVELS = 6
Z_BLOCKS = N_LEVELS + 1


def _hgrn_constants():
    z = np.zeros((Z_BLOCKS, CHUNK, CHUNK), np.float32)
    masks = np.zeros((N_LEVELS, CHUNK, CHUNK), np.float32)
    idx = np.arange(CHUNK)
    for l in range(N_LEVELS):
        hs = CHUNK >> (l + 1)
        parent = idx // (2 * hs)
        right = (idx % (2 * hs)) >= hs
        mid = parent * 2 * hs + hs
        for t in range(CHUNK):
            z[l, t, :mid[t]] = 1.0
        masks[l] = (parent[:, None] == parent[None, :]) & right[:, None] & (~right)[None, :]
    z[N_LEVELS] = np.tril(np.ones((CHUNK, CHUNK), np.float32))
    z = z.reshape(Z_BLOCKS * CHUNK, CHUNK)
    tile_masks = np.zeros((N_LEVELS, LANES, LANES), np.float32)
    for c in range(LANES // CHUNK):
        tile_masks[:, c * CHUNK:(c + 1) * CHUNK, c * CHUNK:(c + 1) * CHUNK] = masks
    return np.concatenate([z, z], axis=1), tile_masks


def _hgrn_rows(r0, q_ref, k_ref, g_ref, v_ref, gate_ref, og_ref, z_ref, mask_ref, o_ref, state_ref):
    dn_t = (((1,), (1,)), ((), ()))
    n_c = HG_ROWS // CHUNK
    n_t = HG_ROWS // LANES
    rr = slice(r0, r0 + HG_ROWS)
    eye = (lax.broadcasted_iota(I32, (LANES, LANES), 0)
           == lax.broadcasted_iota(I32, (LANES, LANES), 1)).astype(F32)
    sums = []
    for c in range(n_c):
        g2 = g_ref[0, r0 + c * CHUNK:r0 + (c + 1) * CHUNK, :] * LOG2E
        g_hi, g_lo = _split_bf16(g2)
        sums.append(jnp.dot(z_ref[...], jnp.concatenate([g_hi, g_lo], axis=0),
                            preferred_element_type=F32))
    level = lambda l, cols: jnp.concatenate([s[l * CHUNK:(l + 1) * CHUNK, cols] for s in sums], axis=0)
    for h in range(HG_HEADS):
        cols = slice(h * HG_DK, (h + 1) * HG_DK)
        b = level(N_LEVELS, cols)
        q = q_ref[0, rr, cols].astype(F32)
        k = k_ref[0, rr, cols].astype(F32)
        v = v_ref[0, rr, cols]
        diag = jnp.sum(q * k, axis=-1, keepdims=True)
        a = [eye * diag[t * LANES:(t + 1) * LANES] for t in range(n_t)]
        for l in range(N_LEVELS):
            d = jnp.exp2(-jnp.abs(b - level(l, cols)))
            s = lax.dot_general((q * d).astype(BF16), (k * d).astype(BF16), dn_t,
                                preferred_element_type=F32)
            for t in range(n_t):
                ts = slice(t * LANES, (t + 1) * LANES)
                a[t] = a[t] + mask_ref[l] * s[ts, ts]
        o_intra = jnp.concatenate(
            [jnp.dot(a[t].astype(BF16), v[t * LANES:(t + 1) * LANES], preferred_element_type=F32)
             for t in range(n_t)], axis=0)
        d_in = jnp.exp2(b)
        qd = (q * d_in).astype(BF16)
        st = state_ref[h]
        o_parts = []
        for c in range(n_c):
            rows = slice(c * CHUNK, (c + 1) * CHUNK)
            last = slice((c + 1) * CHUNK - 1, (c + 1) * CHUNK)
            o_parts.append(o_intra[rows] + lax.dot_general(qd[rows], st.astype(BF16), dn_t,
                                                           preferred_element_type=F32))
            kh = (k[rows] * jnp.exp2(b[last] - b[rows])).astype(BF16)
            upd = lax.dot_general(v[rows], kh, (((0,), (0,)), ((), ())), preferred_element_type=F32)
            st = st * d_in[last] + upd
        state_ref[h] = st
        o = jnp.concatenate(o_parts, axis=0)
        o = _rms_norm(o, og_ref[:, cols]) * gate_ref[0, rr, cols].astype(F32)
        o_ref[0, rr, cols] = o.astype(BF16)


def _hgrn_body(q_ref, k_ref, g_ref, v_ref, gate_ref, og_ref, z_ref, mask_ref, o_ref, state_ref):
    @pl.when(pl.program_id(1) == 0)
    def _():
        state_ref[...] = jnp.zeros_like(state_ref)

    _hgrn_rows(0, q_ref, k_ref, g_ref, v_ref, gate_ref, og_ref, z_ref, mask_ref, o_ref, state_ref)


def _hgrn(hq, hk, hg, hv, gate, out_g, B, S):
    zc, masks = _hgrn_constants()
    zc = jnp.asarray(zc, BF16)
    masks = jnp.asarray(masks, F32)
    blk = pl.BlockSpec((1, HG_ROWS, HG_W), lambda b, i: (b, i, 0))
    r3 = lambda a: a.reshape(B, S, HG_W)
    out = pl.pallas_call(
        _hgrn_body,
        grid=(B, S // HG_ROWS),
        in_specs=[blk, blk, blk, blk, blk,
                  pl.BlockSpec((1, HG_W), lambda b, i: (0, 0)),
                  pl.BlockSpec(zc.shape, lambda b, i: (0, 0)),
                  pl.BlockSpec(masks.shape, lambda b, i: (0, 0, 0))],
        out_specs=blk,
        out_shape=jax.ShapeDtypeStruct((B, S, HG_W), BF16),
        scratch_shapes=[pltpu.VMEM((HG_HEADS, HG_DV, HG_DK), F32)],
        compiler_params=_cparams(("parallel", "arbitrary")),
        name="hgrn2",
    )(r3(hq), r3(hk), r3(hg), r3(hv), r3(gate), out_g, zc, masks)
    return out.reshape(B * S, HG_W)


def _pack_bf16_pair(lo, hi):
    lo_b = pltpu.bitcast(lo.astype(BF16).astype(F32), I32)
    hi_b = pltpu.bitcast(hi.astype(BF16).astype(F32), I32)
    return hi_b | lax.shift_right_logical(lo_b, 16)


def _unpack_bf16_pair(w):
    lo = pltpu.bitcast(lax.shift_left(w, 16), F32)
    hi = pltpu.bitcast(w & jnp.int32(-65536), F32)
    return lo, hi


def _out_proj_body(h_ref, attn_ref, hgo_ref, mg_ref, wo_ref, l1g_ref, l1b_ref, wr_ref, br_ref,
                   h1_out, h1b_out, pos_out, gate_out, cnt_out):
    for sub_tile in range(TM_OUT // TM_DISP):
        rs = slice(sub_tile * TM_DISP, (sub_tile + 1) * TM_DISP)
        _out_proj_tile(h_ref[rs, :], attn_ref[rs, :], hgo_ref[rs, :], mg_ref, wo_ref,
                       l1g_ref, l1b_ref, wr_ref, br_ref,
                       h1_out.at[rs, :], h1b_out.at[rs, :], pos_out.at[:, rs], gate_out.at[:, rs],
                       cnt_out.at[sub_tile])


def _out_proj_tile(h, attn, hgo, mg_ref, wo_ref, l1g_ref, l1b_ref, wr_ref, br_ref,
                   h1_out, h1b_out, pos_out, gate_out, cnt_out):
    attn = _rms_norm(attn.astype(F32), mg_ref[...]).astype(BF16)
    mix_in = jnp.concatenate([attn, hgo], axis=1)
    mix = jnp.dot(mix_in, wo_ref[...], preferred_element_type=F32)
    h1 = _layer_norm(DEEPNORM_ALPHA * h + mix, l1g_ref[...], l1b_ref[...])
    h1_out[...] = h1
    h1_hi, h1_lo = _split_bf16(h1)
    h1b_out[...] = h1_hi

    w_hi, w_lo = _split_bf16(wr_ref[...])
    nt_dims = (((1,), (1,)), ((), ()))
    logits_t = (lax.dot_general(w_hi, h1_hi, nt_dims, preferred_element_type=F32)
                + lax.dot_general(w_hi, h1_lo, nt_dims, preferred_element_type=F32)
                + lax.dot_general(w_lo, h1_hi, nt_dims, preferred_element_type=F32)) + br_ref[...]
    tm = h1.shape[0]
    sub = lax.broadcasted_iota(I32, (N_EXPERTS, tm), 0)
    work = logits_t
    vals, idxs = [], []
    for _ in range(TOP_K):
        mx = jnp.max(work, axis=0, keepdims=True)
        ix = jnp.min(jnp.where(work == mx, sub, N_EXPERTS), axis=0, keepdims=True)
        vals.append(mx)
        idxs.append(ix)
        work = jnp.where(sub == ix, -jnp.inf, work)
    exps = [jnp.exp(vl - vals[0]) for vl in vals]
    den = exps[0] + exps[1] + exps[2] + exps[3]

    sel = jnp.zeros((N_EXPERTS, tm), F32)
    for ix in idxs:
        sel = sel + (sub == ix).astype(F32)
    selb = sel.astype(BF16)
    earlier = (lax.broadcasted_iota(I32, (tm, tm), 0)
               < lax.broadcasted_iota(I32, (tm, tm), 1)).astype(BF16)
    lower = (lax.broadcasted_iota(I32, (N_EXPERTS, N_EXPERTS), 1)
             < lax.broadcasted_iota(I32, (N_EXPERTS, N_EXPERTS), 0)).astype(BF16)
    cnt = jnp.sum(sel, axis=1, keepdims=True)
    run = jnp.floor((cnt + (SUBLANES - 1)) * (1.0 / SUBLANES)) * SUBLANES
    run_b = jnp.broadcast_to(run, (N_EXPERTS, LANES)).astype(BF16)
    seg_off = jnp.dot(lower, run_b, preferred_element_type=F32)[:, 0:1]
    posmat = jnp.dot(selb, earlier, preferred_element_type=F32) + seg_off
    pos = [jnp.sum(jnp.where(sub == ix, posmat, 0.0), axis=0, keepdims=True) for ix in idxs]
    pos_out[...] = jnp.concatenate(pos, axis=0).astype(I32)
    gsel = jnp.zeros((N_EXPERTS, tm), F32)
    for ix, ex in zip(idxs, exps):
        gsel = gsel + jnp.where(sub == ix, ex / den, 0.0)
    gate_out[...] = gsel
    cnt_out[...] = cnt


def _out_proj(h, attn, hgo, mla_g, w_o, l1g, l1b, w_r, b_r):
    T = h.shape[0]
    row = lambda w: pl.BlockSpec((TM_OUT, w), lambda i: (i, 0))
    full = lambda a: pl.BlockSpec(a.shape, lambda i: (0, 0))
    col = pl.BlockSpec((TOP_K, TM_OUT), lambda i: (0, i))
    sub_tiles = TM_OUT // TM_DISP
    return pl.pallas_call(
        _out_proj_body,
        grid=(T // TM_OUT,),
        in_specs=[row(D_MODEL), row(MLA_HEADS * MLA_V), row(HG_W), full(mla_g),
                  full(w_o), full(l1g), full(l1b), full(w_r), full(b_r)],
        out_specs=[row(D_MODEL), row(D_MODEL), col, pl.BlockSpec((N_EXPERTS, TM_OUT), lambda i: (0, i)),
                   pl.BlockSpec((sub_tiles, N_EXPERTS, 1), lambda i: (i, 0, 0))],
        out_shape=[jax.ShapeDtypeStruct((T, D_MODEL), F32), jax.ShapeDtypeStruct((T, D_MODEL), BF16),
                   jax.ShapeDtypeStruct((TOP_K, T), I32), jax.ShapeDtypeStruct((N_EXPERTS, T), F32),
                   jax.ShapeDtypeStruct((T // TM_DISP, N_EXPERTS, 1), F32)],
        compiler_params=_cparams(("parallel",)),
        name="out_proj_router",
    )(h, attn, hgo, mla_g, w_o, l1g, l1b, w_r, b_r)


TILE_SLOTS = TOP_K * TM_DISP + N_EXPERTS * SUBLANES
ROW_WORDS = D_MODEL // 2 + LANES
TILES_PER_STEP = 2
TM_STEP = TILES_PER_STEP * TM_DISP
RUN_UNROLL = 4

def _ceil_to(n, m):
    return (n + m - 1) // m * m


def _rows(start, size):
    return pl.ds(pl.multiple_of(start, SUBLANES), size)


def _for_each_piece(n, fn, max_rows=TM_DISP):
    for kbit in reversed(range(SUBLANES.bit_length() - 1, max_rows.bit_length())):
        size = 1 << kbit

        @pl.when((n & size) != 0)
        def _():
            fn(pl.multiple_of((n >> (kbit + 1)) << (kbit + 1), SUBLANES), size)


def _pack_exact_pair(lo, hi):
    return pltpu.bitcast(hi, I32) | lax.shift_right_logical(pltpu.bitcast(lo, I32), 16)


def _zero_tail_blocks(zeros_ref, out_hbm, n_used, zsem):
    def blk_copy(g):
        return pltpu.make_async_copy(zeros_ref.at[pl.ds(0, ROW_UNIT)],
                                     out_hbm.at[_rows(g * ROW_UNIT, ROW_UNIT)], zsem)

    n_total = out_hbm.shape[0] // ROW_UNIT
    lax.fori_loop(n_used, n_total, lambda g, c: (blk_copy(g).start(), c)[1], 0)
    lax.fori_loop(n_used, n_total, lambda g, c: (blk_copy(g).wait(), c)[1], 0)


def _dispatch_body(segn_ref, segsrc_ref, segdst_ref, tot_ref, padn_ref, padd_ref, nused_ref, pos_ref, xb_ref, g_ref,
                   xs_out, stage, zeros_ref, sem, zsem):
    i = pl.program_id(0)

    @pl.when(i == 0)
    def _():
        zeros_ref[...] = jnp.zeros_like(zeros_ref)

        def pad_copy(e, off, size):
            return pltpu.make_async_copy(zeros_ref.at[pl.ds(0, size)],
                                         xs_out.at[_rows(padd_ref[e] + off, size)], zsem)

        def fill(e, c):
            _for_each_piece(padn_ref[e], lambda off, size: pad_copy(e, off, size).start())
            return c

        def drain(e, c):
            _for_each_piece(padn_ref[e], lambda off, size: pad_copy(e, off, size).wait())
            return c

        lax.fori_loop(0, N_EXPERTS, fill, 0)
        lax.fori_loop(0, N_EXPERTS, drain, 0)
        _zero_tail_blocks(zeros_ref, xs_out, nused_ref[0], zsem)

    def settle(tile, b):
        _for_each_piece(tot_ref[tile], lambda off, size: pltpu.make_async_copy(
            stage.at[b, pl.ds(0, size)], xs_out.at[pl.ds(0, size)], sem.at[b]).wait(), TILE_SLOTS)

    for b in range(TILES_PER_STEP):
        tile = i * TILES_PER_STEP + b
        ts = slice(b * TM_DISP, (b + 1) * TM_DISP)
        pos = pos_ref[:, ts]
        slot = lax.broadcasted_iota(I32, (TILE_SLOTS, TM_DISP), 0)
        sel = jnp.zeros((TILE_SLOTS, TM_DISP), F32)
        for j in range(TOP_K):
            sel = jnp.where(slot == pos[j:j + 1, :], 1.0, sel)
        g_hi, g_lo = _split_bf16(g_ref[:, ts])
        e_sub = lax.broadcasted_iota(I32, (N_EXPERTS, LANES), 0)
        e_lane = lax.broadcasted_iota(I32, (N_EXPERTS, LANES), 1)
        tn_dims = (((0,), (0,)), ((), ()))
        gx = (lax.dot_general(g_hi, (e_lane == e_sub).astype(BF16), tn_dims, preferred_element_type=F32)
              + lax.dot_general(g_lo, (e_lane == e_sub + N_EXPERTS).astype(BF16), tn_dims,
                                preferred_element_type=F32))
        src_rows = jnp.concatenate([xb_ref[ts, :], gx.astype(BF16)], axis=1)
        rows = jnp.dot(sel.astype(BF16), src_rows, preferred_element_type=F32)
        packed = jnp.concatenate(
            [_pack_exact_pair(rows[:, :D_MODEL // 2], rows[:, D_MODEL // 2:D_MODEL]),
             pltpu.bitcast(rows[:, D_MODEL:], I32)], axis=1)

        @pl.when(i >= 1)
        def _():
            settle(tile - TILES_PER_STEP, b)

        stage[b] = packed

        def send(eb, c):
            for u in range(RUN_UNROLL):
                k = tile * N_EXPERTS + eb * RUN_UNROLL + u
                src, dst = segsrc_ref[k], segdst_ref[k]
                _for_each_piece(segn_ref[k], lambda off, size: pltpu.make_async_copy(
                    stage.at[b, _rows(src + off, size)], xs_out.at[_rows(dst + off, size)],
                    sem.at[b]).start())
            return c

        lax.fori_loop(0, N_EXPERTS // RUN_UNROLL, send, 0)

    @pl.when(i == pl.num_programs(0) - 1)
    def _():
        for b in range(TILES_PER_STEP):
            settle(i * TILES_PER_STEP + b, b)


def _dispatch(seg_n, seg_src, seg_dst, tile_tot, pad_n, pad_dst, n_used, pos_t, h1b, gates_x, n_rows):
    T = h1b.shape[0]
    grid_spec = pltpu.PrefetchScalarGridSpec(
        num_scalar_prefetch=7,
        grid=(T // TM_STEP,),
        in_specs=[pl.BlockSpec((TOP_K, TM_STEP), lambda i, *_: (0, i)),
                  pl.BlockSpec((TM_STEP, D_MODEL), lambda i, *_: (i, 0)),
                  pl.BlockSpec((N_EXPERTS, TM_STEP), lambda i, *_: (0, i))],
        out_specs=pl.BlockSpec(memory_space=pl.ANY),
        scratch_shapes=[pltpu.VMEM((TILES_PER_STEP, TILE_SLOTS, ROW_WORDS), I32),
                        pltpu.VMEM((ROW_BLOCK, ROW_WORDS), I32),
                        pltpu.SemaphoreType.DMA((TILES_PER_STEP,)), pltpu.SemaphoreType.DMA(())],
    )
    return pl.pallas_call(
        _dispatch_body,
        grid_spec=grid_spec,
        out_shape=jax.ShapeDtypeStruct((n_rows, ROW_WORDS), I32),
        compiler_params=_cparams(("arbitrary",)),
        name="moe_dispatch",
    )(seg_n, seg_src, seg_dst, tile_tot, pad_n, pad_dst, n_used, pos_t, h1b, gates_x)


def _experts_body(istart_ref, icount_ref, iunit_ref, ibig_ref, nitems_ref, uused_ref,
                  xs_hbm, wu_ref, bu_ref, wd_ref, bd_ref, ys_hbm,
                  wu_bf, wd_bf, xbuf, ybuf, sem_in, sem_out):
    e = pl.program_id(0)
    n_items = nitems_ref[0]
    nb = icount_ref[e]

    def by_size(k, fn):
        full = ibig_ref[k] == 1
        pl.when(full)(lambda: fn(ROW_BLOCK))
        pl.when(jnp.logical_not(full))(lambda: fn(ROW_UNIT))

    def fetch(k, slot, size):
        rows = _rows(iunit_ref[k] * ROW_UNIT, size)
        return pltpu.make_async_copy(xs_hbm.at[rows], xbuf.at[slot, pl.ds(0, size)], sem_in.at[slot])

    def put(k, slot, size):
        rows = _rows(iunit_ref[k] * ROW_UNIT, size)
        return pltpu.make_async_copy(ybuf.at[slot, pl.ds(0, size)], ys_hbm.at[rows], sem_out.at[slot])

    @pl.when(jnp.logical_and(e == 0, n_items > 0))
    def _():
        by_size(0, lambda size: fetch(0, 0, size).start())

    @pl.when(nb > 0)
    def _():
        wu_bf[...] = wu_ref[0].astype(BF16)
        wd_bf[...] = wd_ref[0].astype(BF16)

    def compute(k, slot, size):
        row = xbuf[slot, :size]
        lo, hi = _unpack_bf16_pair(row[:, :D_MODEL // 2])
        x = jnp.concatenate([lo, hi], axis=1).astype(BF16)
        lane = lax.broadcasted_iota(I32, (size, LANES), 1)
        mine = jnp.logical_or(lane == e, lane == e + N_EXPERTS)
        gate = jnp.sum(jnp.where(mine, pltpu.bitcast(row[:, D_MODEL // 2:], F32), 0.0),
                       axis=1, keepdims=True)
        hb = jnp.dot(x, wu_bf[...], preferred_element_type=F32) + bu_ref[0]
        glu = jnp.minimum(hb[:, :D_EXPERT], SWIGLU_LIMIT)
        lin = jnp.clip(hb[:, D_EXPERT:], -SWIGLU_LIMIT, SWIGLU_LIMIT)
        act = glu * (1.0 / (1.0 + jnp.exp(-SWIGLU_ALPHA * glu))) * (lin + 1.0)
        y = (jnp.dot(act.astype(BF16), wd_bf[...], preferred_element_type=F32) + bd_ref[0]) * gate
        ybuf[slot, :size] = _pack_bf16_pair(y[:, :D_MODEL // 2], y[:, D_MODEL // 2:])
        put(k, slot, size).start()

    def block(b, c):
        k = istart_ref[e] + b
        slot = lax.rem(k, 2)
        by_size(k, lambda size: fetch(k, slot, size).wait())

        @pl.when(k + 1 < n_items)
        def _():
            by_size(k + 1, lambda size: fetch(k + 1, 1 - slot, size).start())

        @pl.when(k >= 2)
        def _():
            by_size(k - 2, lambda size: put(k - 2, slot, size).wait())

        by_size(k, lambda size: compute(k, slot, size))
        return c

    lax.fori_loop(0, nb, block, 0)

    @pl.when(e == N_EXPERTS - 1)
    def _():
        for back in (2, 1):
            @pl.when(n_items >= back)
            def _():
                k = n_items - back
                by_size(k, lambda size: put(k, lax.rem(k, 2), size).wait())
        ybuf[0] = jnp.zeros_like(ybuf[0])
        _zero_tail_blocks(ybuf.at[0], ys_hbm, uused_ref[0], sem_out.at[0])


def _experts(item_start, item_count, item_unit, item_big, n_items, units_used, xs,
             w_up, b_up, w_down, b_down):
    n_rows = xs.shape[0]
    grid_spec = pltpu.PrefetchScalarGridSpec(
        num_scalar_prefetch=6,
        grid=(N_EXPERTS,),
        in_specs=[pl.BlockSpec(memory_space=pl.ANY),
                  pl.BlockSpec((1, D_MODEL, 2 * D_EXPERT), lambda e, *_: (e, 0, 0)),
                  pl.BlockSpec((1, 1, 2 * D_EXPERT), lambda e, *_: (e, 0, 0)),
                  pl.BlockSpec((1, D_EXPERT, D_MODEL), lambda e, *_: (e, 0, 0)),
                  pl.BlockSpec((1, 1, D_MODEL), lambda e, *_: (e, 0, 0))],
        out_specs=pl.BlockSpec(memory_space=pl.ANY),
        scratch_shapes=[pltpu.VMEM((D_MODEL, 2 * D_EXPERT), BF16), pltpu.VMEM((D_EXPERT, D_MODEL), BF16),
                        pltpu.VMEM((2, ROW_BLOCK, ROW_WORDS), I32),
                        pltpu.VMEM((2, ROW_BLOCK, D_MODEL // 2), I32),
                        pltpu.SemaphoreType.DMA((2,)), pltpu.SemaphoreType.DMA((2,))],
    )
    return pl.pallas_call(
        _experts_body,
        grid_spec=grid_spec,
        out_shape=jax.ShapeDtypeStruct((n_rows, D_MODEL // 2), I32),
        compiler_params=_cparams(("arbitrary",)),
        name="moe_experts",
    )(item_start, item_count, item_unit, item_big, n_items, units_used, xs, w_up,
      b_up.reshape(N_EXPERTS, 1, 2 * D_EXPERT), w_down, b_down.reshape(N_EXPERTS, 1, D_MODEL))


def _combine_body(segn_ref, segsrc_ref, segdst_ref, tot_ref, pos_ref, h1_ref, l2g_ref, l2b_ref, ys_hbm,
                  o_ref, stage, sem):
    i = pl.program_id(0)
    n_tiles = pl.num_programs(0) * TILES_PER_STEP

    def fetch_tile(tile, b):
        def fetch(eb, c):
            for u in range(RUN_UNROLL):
                k = tile * N_EXPERTS + eb * RUN_UNROLL + u
                src, dst = segsrc_ref[k], segdst_ref[k]
                _for_each_piece(segn_ref[k], lambda off, size: pltpu.make_async_copy(
                    ys_hbm.at[_rows(dst + off, size)], stage.at[b, _rows(src + off, size)],
                    sem.at[b]).start())
            return c

        lax.fori_loop(0, N_EXPERTS // RUN_UNROLL, fetch, 0)

    @pl.when(i == 0)
    def _():
        stage[...] = jnp.zeros_like(stage)
        fetch_tile(0, 0)

    for b in range(TILES_PER_STEP):
        tile = i * TILES_PER_STEP + b
        nxt = (b + 1) % TILES_PER_STEP

        @pl.when(tile + 1 < n_tiles)
        def _():
            fetch_tile(tile + 1, nxt)

        ts = slice(b * TM_DISP, (b + 1) * TM_DISP)
        pos = pos_ref[ts, :]
        slot = lax.broadcasted_iota(I32, (TM_DISP, TILE_SLOTS), 1)
        w = jnp.zeros((TM_DISP, TILE_SLOTS), F32)
        for j in range(TOP_K):
            w = jnp.where(slot == pos[:, j:j + 1], 1.0, w)
        w = w.astype(BF16)
        _for_each_piece(tot_ref[tile], lambda off, size: pltpu.make_async_copy(
            ys_hbm.at[pl.ds(0, size)], stage.at[b, pl.ds(0, size)], sem.at[b]).wait(), TILE_SLOTS)
        lo, hi = _unpack_bf16_pair(stage[b])
        y = jnp.concatenate([lo, hi], axis=1).astype(BF16)
        ffn = jnp.dot(w, y, preferred_element_type=F32)
        o_ref[ts, :] = _layer_norm(DEEPNORM_ALPHA * h1_ref[ts, :] + ffn, l2g_ref[...], l2b_ref[...])


def _combine(seg_n, seg_src, seg_dst, tile_tot, pos, h1, l2g, l2b, ys):
    T = h1.shape[0]
    grid_spec = pltpu.PrefetchScalarGridSpec(
        num_scalar_prefetch=4,
        grid=(T // TM_STEP,),
        in_specs=[pl.BlockSpec((TM_STEP, TOP_K), lambda i, *_: (i, 0)),
                  pl.BlockSpec((TM_STEP, D_MODEL), lambda i, *_: (i, 0)),
                  pl.BlockSpec(l2g.shape, lambda i, *_: (0, 0)),
                  pl.BlockSpec(l2b.shape, lambda i, *_: (0, 0)),
                  pl.BlockSpec(memory_space=pl.ANY)],
        out_specs=pl.BlockSpec((TM_STEP, D_MODEL), lambda i, *_: (i, 0)),
        scratch_shapes=[pltpu.VMEM((TILES_PER_STEP, TILE_SLOTS, D_MODEL // 2), I32),
                        pltpu.SemaphoreType.DMA((TILES_PER_STEP,))],
    )
    return pl.pallas_call(
        _combine_body,
        grid_spec=grid_spec,
        out_shape=jax.ShapeDtypeStruct((T, D_MODEL), F32),
        compiler_params=_cparams(("arbitrary",)),
        name="moe_combine",
    )(seg_n, seg_src, seg_dst, tile_tot, pos, h1, l2g, l2b, ys)


def _prep_weights(w_in, w_q_b, w_kv_b):
    w_kr = jnp.pad(w_in[:, KR_OFF:KR_OFF + MLA_ROPE], ((0, 0), (MLA_NOPE, HEAD_PAD - MLA_NOPE - MLA_ROPE)))
    w_in_p = (jnp.concatenate([w_in[:, :KR_OFF], w_kr], axis=1).astype(BF16),
              w_in[:, KR_OFF + MLA_ROPE:].astype(BF16))
    qd = MLA_NOPE + MLA_ROPE
    wq = w_q_b.reshape(Q_LORA, MLA_HEADS, qd)
    wq = jnp.pad(wq, ((0, 0), (0, 0), (0, HEAD_PAD - qd))).reshape(Q_LORA, MLA_HEADS * HEAD_PAD)
    wkv = w_kv_b.reshape(KV_LORA, MLA_HEADS, MLA_NOPE + MLA_V)
    wk = jnp.pad(wkv[:, :, :MLA_NOPE], ((0, 0), (0, 0), (0, HEAD_PAD - MLA_NOPE)))
    wk = wk.reshape(KV_LORA, MLA_HEADS * HEAD_PAD)
    wv = jnp.pad(wkv[:, :, MLA_NOPE:], ((0, 0), (0, 0), (0, HEAD_PAD - MLA_V)))
    wv = wv.reshape(KV_LORA, MLA_HEADS * HEAD_PAD)
    return w_in_p, wq.astype(BF16), jnp.concatenate([wk, wv], axis=1).astype(BF16)


def kernel(x, positions, ln_in_g, ln_in_b, w_in, q_a_norm_g, w_q_b, kv_a_norm_g, w_kv_b, hgrn_lb_logits,
           mla_out_g, hgrn_out_g, w_o, ln1_g, ln1_b, w_router, b_router, w_up, b_up, w_down, b_down,
           ln2_g, ln2_b):
    B, S, D = x.shape
    assert D == D_MODEL and w_in.shape[0] == DEPTH and hgrn_lb_logits.shape[0] == DEPTH + 1
    T = B * S
    xt = x.reshape(T, D)
    r2 = lambda a: a.reshape(1, -1)

    cos, sin = _rope_cos_sin(positions)
    w_in_p, w_qb_p, w_kv_p = _prep_weights(w_in[0], w_q_b[0], w_kv_b[0])
    q, k, v, hq, hk, hg, hv, gate, h0 = _in_proj(
        xt, r2(ln_in_g), r2(ln_in_b), w_in_p, q_a_norm_g, w_qb_p, kv_a_norm_g, w_kv_p,
        hgrn_lb_logits, cos, sin)
    attn = _attention(q, k, v, B, S)
    hgo = _hgrn(hq, hk, hg, hv, gate, hgrn_out_g, B, S)
    h1, h1b, pos_t, gate_t, tile_cnt = _out_proj(
        h0, attn, hgo, mla_out_g, w_o[0].astype(BF16), ln1_g, ln1_b,
        w_router[0].T, b_router.reshape(N_EXPERTS, 1))

    n_tiles = T // TM_DISP
    n_rows = T * TOP_K + n_tiles * N_EXPERTS * (SUBLANES - 1) + N_EXPERTS * ROW_UNIT
    n_rows = _ceil_to(n_rows, ROW_BLOCK)
    tc = _ceil_to(tile_cnt[:, :, 0].astype(I32), SUBLANES)
    cnt = jnp.sum(tc, axis=0)
    padded = _ceil_to(cnt, ROW_UNIT)
    pad_start = jnp.cumsum(padded) - padded
    seg_n = tc.reshape(-1)
    seg_src = (jnp.cumsum(tc, axis=1) - tc).reshape(-1)
    seg_dst = (pad_start[None, :] + jnp.cumsum(tc, axis=0) - tc).reshape(-1)
    units_used = jnp.sum(padded, keepdims=True) // ROW_UNIT

    units = padded // ROW_UNIT
    item_count = (units + 1) // 2
    item_end = jnp.cumsum(item_count)
    item_start = item_end - item_count
    k = jnp.arange(n_rows // ROW_BLOCK + N_EXPERTS, dtype=I32)
    of_expert = (jnp.minimum(jnp.sum((item_end[None, :] <= k[:, None]).astype(I32), axis=1), N_EXPERTS - 1)
                 [:, None] == jnp.arange(N_EXPERTS, dtype=I32)[None, :])
    pick = lambda v: jnp.sum(jnp.where(of_expert, v[None, :], 0), axis=1)
    first_unit = 2 * (k - pick(item_start))
    item_unit = pick(pad_start // ROW_UNIT) + first_unit
    item_big = (first_unit + 2 <= pick(units)).astype(I32)

    tile_tot = jnp.sum(tc, axis=1)
    xs = _dispatch(seg_n, seg_src, seg_dst, tile_tot, padded - cnt, pad_start + cnt, units_used, pos_t, h1b,
                   gate_t, n_rows)
    ys = _experts(item_start, item_count, item_unit, item_big, item_end[-1:], units_used, xs,
                  w_up[0], b_up[0], w_down[0], b_down[0])
    out = _combine(seg_n, seg_src, seg_dst, tile_tot, pos_t.T, h1, ln2_g, ln2_b, ys)
    return out.reshape(B, S, D)
```

```python
import numpy as np
import jax
import jax.numpy as jnp
from jax import lax
from jax.experimental import pallas as pl
from jax.experimental.pallas import tpu as pltpu

F32 = jnp.float32
BF16 = jnp.bfloat16
I32 = jnp.int32

D_MODEL = 1024
CHUNK = 64
MLA_HEADS = 8
MLA_NOPE = 64
MLA_ROPE = 32
MLA_V = 64
Q_LORA = 384
KV_LORA = 256
ROPE_THETA = 10000.0
HG_HEADS = 4
HG_DK = 128
HG_DV = 128
HG_W = HG_HEADS * HG_DK
N_EXPERTS = 32
TOP_K = 4
D_EXPERT = 1024
SWIGLU_LIMIT = 7.0
SWIGLU_ALPHA = 1.702
LN_EPS = 1e-5
RMS_EPS = 1e-6
DEPTH = 1
DEEPNORM_ALPHA = (2.0 * DEPTH) ** 0.25
LOG2E = 1.4426950408889634

LANES = 128
SUBLANES = 8
HEAD_PAD = 128
HALF_ROPE = MLA_ROPE // 2
KR_OFF = Q_LORA + KV_LORA
HQ_OFF = KR_OFF + HEAD_PAD
IN_COLS_PAD = HQ_OFF + 4 * HG_W

TM_IN = 512
TQ = 512
HG_ROWS = 256
TM_OUT = 1024
TM_DISP = 256
ROW_BLOCK = 512
ROW_UNIT = ROW_BLOCK // 4
VMEM_LIMIT = 52 * 1024 * 1024


def _cparams(sem, vmem=VMEM_LIMIT):
    return pltpu.CompilerParams(dimension_semantics=sem, vmem_limit_bytes=vmem)


def _layer_norm(x, g, b):
    mu = jnp.mean(x, axis=-1, keepdims=True)
    xc = x - mu
    var = jnp.mean(xc * xc, axis=-1, keepdims=True)
    return xc * lax.rsqrt(var + LN_EPS) * g + b


def _rms_norm(x, g):
    return x * lax.rsqrt(jnp.mean(x * x, axis=-1, keepdims=True) + RMS_EPS) * g


def _split_bf16(a):
    hi = a.astype(BF16)
    return hi, (a - hi.astype(F32)).astype(BF16)


def _silu(x):
    return x * (1.0 / (1.0 + jnp.exp(-x)))


def _sigmoid_pair(x):
    e = jnp.exp(-jnp.abs(x))
    big = 1.0 / (1.0 + e)
    small = e * big
    pos = x >= 0
    return jnp.where(pos, big, small), jnp.where(pos, small, big)


def _cos_sin_body(ang_ref, cos_ref, sin_ref):
    a = ang_ref[...]
    cos_ref[...] = jnp.cos(a)
    sin_ref[...] = jnp.sin(a)


def _rope_cos_sin(positions):
    T = positions.size
    inv_freq = ROPE_THETA ** (-jnp.arange(0, MLA_ROPE, 2, dtype=F32) / MLA_ROPE)
    ang = (positions.astype(F32)[..., None] * inv_freq).reshape(T * HALF_ROPE // LANES, LANES)
    cos, sin = pl.pallas_call(
        _cos_sin_body,
        out_shape=(jax.ShapeDtypeStruct(ang.shape, F32),) * 2,
        name="rope_cos_sin",
    )(ang)
    return cos.reshape(T, HALF_ROPE), sin.reshape(T, HALF_ROPE)


def _rope_placement():
    pc = np.zeros((HALF_ROPE, HEAD_PAD), np.float32)
    ps = np.zeros((HALF_ROPE, 2 * HEAD_PAD), np.float32)
    for j in range(HALF_ROPE):
        pc[j, MLA_NOPE + j] = pc[j, MLA_NOPE + HALF_ROPE + j] = 1.0
        ps[j, MLA_NOPE + HALF_ROPE + j] = 1.0
        ps[j, HEAD_PAD + MLA_NOPE + j] = -1.0
    return jnp.asarray(pc, BF16), jnp.asarray(ps, BF16)


def _rope_tables(cos, sin, pc, ps):
    place = lambda a, p: sum(jnp.dot(part, p, preferred_element_type=F32) for part in _split_bf16(a))
    lane = lax.broadcasted_iota(I32, (1, HEAD_PAD), 1)
    c = place(cos, pc) + (lane < MLA_NOPE).astype(F32)
    s = place(sin, ps)
    return c, s[:, :HEAD_PAD], s[:, HEAD_PAD:]


def _rope(x, c, s1, s2, reps):
    width = reps * HEAD_PAD
    if reps > 1:
        c = jnp.concatenate([c] * reps, axis=1)
        s1 = jnp.concatenate([s1] * reps, axis=1)
        s2 = jnp.concatenate([s2] * reps, axis=1)
    return (x * c + pltpu.roll(x, HALF_ROPE, 1) * s1
            + pltpu.roll(x, width - HALF_ROPE, 1) * s2)


def _in_proj_body(x_ref, lng_ref, lnb_ref, wa_ref, wh_ref, qag_ref, wqb_ref, kvg_ref, wkv_ref, lbl_ref,
                  cos_ref, sin_ref, pc_ref, ps_ref,
                  q_out, k_out, v_out, hq_out, hk_out, hg_out, hv_out, gate_out, h_out):
    win_refs = (wa_ref, wh_ref)
    h = _layer_norm(x_ref[...], lng_ref[...], lnb_ref[...])
    h_out[...] = h
    hb = h.astype(BF16)
    p = jnp.concatenate([jnp.dot(hb, w_ref[...], preferred_element_type=F32) for w_ref in win_refs],
                        axis=1)
    c, s1, s2 = _rope_tables(cos_ref[...], sin_ref[...], pc_ref[...], ps_ref[...])

    qn = _rms_norm(p[:, :Q_LORA], qag_ref[...]).astype(BF16)
    q = jnp.dot(qn, wqb_ref[...], preferred_element_type=F32)
    scale = (MLA_NOPE + MLA_ROPE) ** -0.5 * LOG2E
    q_out[...] = (_rope(q, c, s1, s2, MLA_HEADS) * scale).astype(BF16)
    kvn = _rms_norm(p[:, Q_LORA:KR_OFF], kvg_ref[...]).astype(BF16)
    kv = jnp.dot(kvn, wkv_ref[...], preferred_element_type=F32)
    kr = _rope(p[:, KR_OFF:HQ_OFF], c, s1, s2, 1)
    k_out[...] = (kv[:, :MLA_HEADS * HEAD_PAD] + jnp.concatenate([kr] * MLA_HEADS, axis=1)).astype(BF16)
    lane = lax.broadcasted_iota(I32, (1, MLA_HEADS * HEAD_PAD), 1)
    ones_lane = (lane % HEAD_PAD == MLA_V).astype(F32)
    v_out[...] = (kv[:, MLA_HEADS * HEAD_PAD:] + ones_lane).astype(BF16)

    lbl = lbl_ref[...]
    le = jnp.exp(lbl - jnp.max(lbl, axis=0, keepdims=True))
    lb = le[0:1] / jnp.sum(le, axis=0, keepdims=True)
    hq = p[:, HQ_OFF:HQ_OFF + HG_W]
    hf = p[:, HQ_OFF + HG_W:HQ_OFF + 2 * HG_W]
    hi = p[:, HQ_OFF + 2 * HG_W:HQ_OFF + 3 * HG_W]
    hg = p[:, HQ_OFF + 3 * HG_W:HQ_OFF + 4 * HG_W]
    hq_out[...] = _silu(hq).astype(BF16)
    sg, one_minus_sg = _sigmoid_pair(hf)
    hk_out[...] = ((1.0 - lb) * one_minus_sg).astype(BF16)
    hg_out[...] = jnp.log(lb + (1.0 - lb) * sg)
    hv_out[...] = hi.astype(BF16)
    gate_out[...] = _silu(hg).astype(BF16)


def _in_proj(xt, ln_g, ln_b, w_in_p, qa_g, w_qb_p, kv_g, w_kv_p, lb_logits, cos, sin):
    T = xt.shape[0]
    pc, ps = _rope_placement()
    row = lambda w: pl.BlockSpec((TM_IN, w), lambda i: (i, 0))
    full = lambda a: pl.BlockSpec(a.shape, lambda i: (0, 0))
    outs = [(MLA_HEADS * HEAD_PAD, BF16), (MLA_HEADS * HEAD_PAD, BF16), (MLA_HEADS * HEAD_PAD, BF16),
            (HG_W, BF16), (HG_W, BF16), (HG_W, F32), (HG_W, BF16), (HG_W, BF16), (D_MODEL, F32)]
    return pl.pallas_call(
        _in_proj_body,
        grid=(T // TM_IN,),
        in_specs=[row(D_MODEL), full(ln_g), full(ln_b), *[full(w) for w in w_in_p], full(qa_g), full(w_qb_p),
                  full(kv_g), full(w_kv_p), full(lb_logits), row(HALF_ROPE), row(HALF_ROPE),
                  full(pc), full(ps)],
        out_specs=[row(w) for w, _ in outs],
        out_shape=[jax.ShapeDtypeStruct((T, w), d) for w, d in outs],
        compiler_params=_cparams(("parallel",)),
        name="in_proj",
    )(xt, ln_g, ln_b, *w_in_p, qa_g, w_qb_p, kv_g, w_kv_p, lb_logits, cos, sin, pc, ps)


def _attn_body(q_ref, k_ref, v_ref, o_ref):
    n_q = q_ref.shape[1] // TQ
    dn = (((1,), (1,)), ((), ()))
    row_c = lax.broadcasted_iota(I32, (TQ, TQ), 0) // CHUNK
    col_c = lax.broadcasted_iota(I32, (TQ, TQ), 1) // CHUNK
    diag_mask = col_c <= row_c
    heads = [slice(hh * HEAD_PAD, (hh + 1) * HEAD_PAD) for hh in range(2)]
    lane = lax.broadcasted_iota(I32, (TQ, 2 * MLA_V), 1)
    for qi in range(n_q):
        r0, kv = qi * TQ, (qi + 1) * TQ
        res = []
        for hs in heads:
            s = lax.dot_general(q_ref[0, r0:kv, hs], k_ref[0, :kv, hs], dn, preferred_element_type=F32)
            s_diag = jnp.where(diag_mask, s[:, r0:], -jnp.inf)
            m = jnp.max(s_diag, axis=-1, keepdims=True)
            if qi:
                m = jnp.maximum(m, jnp.max(s[:, :r0], axis=-1, keepdims=True))
                p = jnp.concatenate([jnp.exp2(s[:, :r0] - m), jnp.exp2(s_diag - m)], axis=1)
            else:
                p = jnp.exp2(s_diag - m)
            a = jnp.dot(p.astype(BF16), v_ref[0, :kv, hs], preferred_element_type=F32)
            res.append(a * (1.0 / a[:, MLA_V:MLA_V + 1]))
        o_ref[0, r0:kv, :] = jnp.where(lane < MLA_V, res[0], pltpu.roll(res[1], MLA_V, 1)).astype(BF16)


def _attention(q, k, v, B, S):
    w = MLA_HEADS * HEAD_PAD
    blk = pl.BlockSpec((1, S, 2 * HEAD_PAD), lambda b, h: (b, 0, h))
    out = pl.pallas_call(
        _attn_body,
        grid=(B, MLA_HEADS // 2),
        in_specs=[blk, blk, blk],
        out_specs=pl.BlockSpec((1, S, 2 * MLA_V), lambda b, h: (b, 0, h)),
        out_shape=jax.ShapeDtypeStruct((B, S, MLA_HEADS * MLA_V), BF16),
        compiler_params=_cparams(("parallel", "parallel")),
        name="mla_attention",
    )(q.reshape(B, S, w), k.reshape(B, S, w), v.reshape(B, S, w))
    return out.reshape(B * S, MLA_HEADS * MLA_V)


N_LEVELS = 6
Z_BLOCKS = N_LEVELS + 1


def _hgrn_constants():
    z = np.zeros((Z_BLOCKS, CHUNK, CHUNK), np.float32)
    masks = np.zeros((N_LEVELS, CHUNK, CHUNK), np.float32)
    idx = np.arange(CHUNK)
    for l in range(N_LEVELS):
        hs = CHUNK >> (l + 1)
        parent = idx // (2 * hs)
        right = (idx % (2 * hs)) >= hs
        mid = parent * 2 * hs + hs
        for t in range(CHUNK):
            z[l, t, :mid[t]] = 1.0
        masks[l] = (parent[:, None] == parent[None, :]) & right[:, None] & (~right)[None, :]
    z[N_LEVELS] = np.tril(np.ones((CHUNK, CHUNK), np.float32))
    z = z.reshape(Z_BLOCKS * CHUNK, CHUNK)
    tile_masks = np.zeros((N_LEVELS, LANES, LANES), np.float32)
    for c in range(LANES // CHUNK):
        tile_masks[:, c * CHUNK:(c + 1) * CHUNK, c * CHUNK:(c + 1) * CHUNK] = masks
    return np.concatenate([z, z], axis=1), tile_masks


def _hgrn_rows(r0, q_ref, k_ref, g_ref, v_ref, gate_ref, og_ref, z_ref, mask_ref, o_ref, state_ref):
    dn_t = (((1,), (1,)), ((), ()))
    n_c = HG_ROWS // CHUNK
    n_t = HG_ROWS // LANES
    rr = slice(r0, r0 + HG_ROWS)
    eye = (lax.broadcasted_iota(I32, (LANES, LANES), 0)
           == lax.broadcasted_iota(I32, (LANES, LANES), 1)).astype(F32)
    sums = []
    for c in range(n_c):
        g2 = g_ref[0, r0 + c * CHUNK:r0 + (c + 1) * CHUNK, :] * LOG2E
        g_hi, g_lo = _split_bf16(g2)
        sums.append(jnp.dot(z_ref[...], jnp.concatenate([g_hi, g_lo], axis=0),
                            preferred_element_type=F32))
    level = lambda l, cols: jnp.concatenate([s[l * CHUNK:(l + 1) * CHUNK, cols] for s in sums], axis=0)
    for h in range(HG_HEADS):
        cols = slice(h * HG_DK, (h + 1) * HG_DK)
        b = level(N_LEVELS, cols)
        q = q_ref[0, rr, cols].astype(F32)
        k = k_ref[0, rr, cols].astype(F32)
        v = v_ref[0, rr, cols]
        diag = jnp.sum(q * k, axis=-1, keepdims=True)
        a = [eye * diag[t * LANES:(t + 1) * LANES] for t in range(n_t)]
        for l in range(N_LEVELS):
            d = jnp.exp2(-jnp.abs(b - level(l, cols)))
            s = lax.dot_general((q * d).astype(BF16), (k * d).astype(BF16), dn_t,
                                preferred_element_type=F32)
            for t in range(n_t):
                ts = slice(t * LANES, (t + 1) * LANES)
                a[t] = a[t] + mask_ref[l] * s[ts, ts]
        o_intra = jnp.concatenate(
            [jnp.dot(a[t].astype(BF16), v[t * LANES:(t + 1) * LANES], preferred_element_type=F32)
             for t in range(n_t)], axis=0)
        d_in = jnp.exp2(b)
        qd = (q * d_in).astype(BF16)
        st = state_ref[h]
        o_parts = []
        for c in range(n_c):
            rows = slice(c * CHUNK, (c + 1) * CHUNK)
            last = slice((c + 1) * CHUNK - 1, (c + 1) * CHUNK)
            o_parts.append(o_intra[rows] + lax.dot_general(qd[rows], st.astype(BF16), dn_t,
                                                           preferred_element_type=F32))
            kh = (k[rows] * jnp.exp2(b[last] - b[rows])).astype(BF16)
            upd = lax.dot_general(v[rows], kh, (((0,), (0,)), ((), ())), preferred_element_type=F32)
            st = st * d_in[last] + upd
        state_ref[h] = st
        o = jnp.concatenate(o_parts, axis=0)
        o = _rms_norm(o, og_ref[:, cols]) * gate_ref[0, rr, cols].astype(F32)
        o_ref[0, rr, cols] = o.astype(BF16)


def _hgrn_body(q_ref, k_ref, g_ref, v_ref, gate_ref, og_ref, z_ref, mask_ref, o_ref, state_ref):
    @pl.when(pl.program_id(1) == 0)
    def _():
        state_ref[...] = jnp.zeros_like(state_ref)

    _hgrn_rows(0, q_ref, k_ref, g_ref, v_ref, gate_ref, og_ref, z_ref, mask_ref, o_ref, state_ref)


def _hgrn(hq, hk, hg, hv, gate, out_g, B, S):
    zc, masks = _hgrn_constants()
    zc = jnp.asarray(zc, BF16)
    masks = jnp.asarray(masks, F32)
    blk = pl.BlockSpec((1, HG_ROWS, HG_W), lambda b, i: (b, i, 0))
    r3 = lambda a: a.reshape(B, S, HG_W)
    out = pl.pallas_call(
        _hgrn_body,
        grid=(B, S // HG_ROWS),
        in_specs=[blk, blk, blk, blk, blk,
                  pl.BlockSpec((1, HG_W), lambda b, i: (0, 0)),
                  pl.BlockSpec(zc.shape, lambda b, i: (0, 0)),
                  pl.BlockSpec(masks.shape, lambda b, i: (0, 0, 0))],
        out_specs=blk,
        out_shape=jax.ShapeDtypeStruct((B, S, HG_W), BF16),
        scratch_shapes=[pltpu.VMEM((HG_HEADS, HG_DV, HG_DK), F32)],
        compiler_params=_cparams(("parallel", "arbitrary")),
        name="hgrn2",
    )(r3(hq), r3(hk), r3(hg), r3(hv), r3(gate), out_g, zc, masks)
    return out.reshape(B * S, HG_W)


def _pack_bf16_pair(lo, hi):
    lo_b = pltpu.bitcast(lo.astype(BF16).astype(F32), I32)
    hi_b = pltpu.bitcast(hi.astype(BF16).astype(F32), I32)
    return hi_b | lax.shift_right_logical(lo_b, 16)


def _unpack_bf16_pair(w):
    lo = pltpu.bitcast(lax.shift_left(w, 16), F32)
    hi = pltpu.bitcast(w & jnp.int32(-65536), F32)
    return lo, hi


def _out_proj_body(h_ref, attn_ref, hgo_ref, mg_ref, wo_ref, l1g_ref, l1b_ref, wr_ref, br_ref,
                   h1_out, h1b_out, pos_out, gate_out, cnt_out):
    for sub_tile in range(TM_OUT // TM_DISP):
        rs = slice(sub_tile * TM_DISP, (sub_tile + 1) * TM_DISP)
        _out_proj_tile(h_ref[rs, :], attn_ref[rs, :], hgo_ref[rs, :], mg_ref, wo_ref,
                       l1g_ref, l1b_ref, wr_ref, br_ref,
                       h1_out.at[rs, :], h1b_out.at[rs, :], pos_out.at[:, rs], gate_out.at[:, rs],
                       cnt_out.at[sub_tile])


def _out_proj_tile(h, attn, hgo, mg_ref, wo_ref, l1g_ref, l1b_ref, wr_ref, br_ref,
                   h1_out, h1b_out, pos_out, gate_out, cnt_out):
    attn = _rms_norm(attn.astype(F32), mg_ref[...]).astype(BF16)
    mix_in = jnp.concatenate([attn, hgo], axis=1)
    mix = jnp.dot(mix_in, wo_ref[...], preferred_element_type=F32)
    h1 = _layer_norm(DEEPNORM_ALPHA * h + mix, l1g_ref[...], l1b_ref[...])
    h1_out[...] = h1
    h1_hi, h1_lo = _split_bf16(h1)
    h1b_out[...] = h1_hi

    w_hi, w_lo = _split_bf16(wr_ref[...])
    nt_dims = (((1,), (1,)), ((), ()))
    logits_t = (lax.dot_general(w_hi, h1_hi, nt_dims, preferred_element_type=F32)
                + lax.dot_general(w_hi, h1_lo, nt_dims, preferred_element_type=F32)
                + lax.dot_general(w_lo, h1_hi, nt_dims, preferred_element_type=F32)) + br_ref[...]
    tm = h1.shape[0]
    sub = lax.broadcasted_iota(I32, (N_EXPERTS, tm), 0)
    work = logits_t
    vals, idxs = [], []
    for _ in range(TOP_K):
        mx = jnp.max(work, axis=0, keepdims=True)
        ix = jnp.min(jnp.where(work == mx, sub, N_EXPERTS), axis=0, keepdims=True)
        vals.append(mx)
        idxs.append(ix)
        work = jnp.where(sub == ix, -jnp.inf, work)
    exps = [jnp.exp(vl - vals[0]) for vl in vals]
    den = exps[0] + exps[1] + exps[2] + exps[3]

    sel = jnp.zeros((N_EXPERTS, tm), F32)
    for ix in idxs:
        sel = sel + (sub == ix).astype(F32)
    selb = sel.astype(BF16)
    earlier = (lax.broadcasted_iota(I32, (tm, tm), 0)
               < lax.broadcasted_iota(I32, (tm, tm), 1)).astype(BF16)
    lower = (lax.broadcasted_iota(I32, (N_EXPERTS, N_EXPERTS), 1)
             < lax.broadcasted_iota(I32, (N_EXPERTS, N_EXPERTS), 0)).astype(BF16)
    cnt = jnp.sum(sel, axis=1, keepdims=True)
    run = jnp.floor((cnt + (SUBLANES - 1)) * (1.0 / SUBLANES)) * SUBLANES
    run_b = jnp.broadcast_to(run, (N_EXPERTS, LANES)).astype(BF16)
    seg_off = jnp.dot(lower, run_b, preferred_element_type=F32)[:, 0:1]
    posmat = jnp.dot(selb, earlier, preferred_element_type=F32) + seg_off
    pos = [jnp.sum(jnp.where(sub == ix, posmat, 0.0), axis=0, keepdims=True) for ix in idxs]
    pos_out[...] = jnp.concatenate(pos, axis=0).astype(I32)
    gsel = jnp.zeros((N_EXPERTS, tm), F32)
    for ix, ex in zip(idxs, exps):
        gsel = gsel + jnp.where(sub == ix, ex / den, 0.0)
    gate_out[...] = gsel
    cnt_out[...] = cnt


def _out_proj(h, attn, hgo, mla_g, w_o, l1g, l1b, w_r, b_r):
    T = h.shape[0]
    row = lambda w: pl.BlockSpec((TM_OUT, w), lambda i: (i, 0))
    full = lambda a: pl.BlockSpec(a.shape, lambda i: (0, 0))
    col = pl.BlockSpec((TOP_K, TM_OUT), lambda i: (0, i))
    sub_tiles = TM_OUT // TM_DISP
    return pl.pallas_call(
        _out_proj_body,
        grid=(T // TM_OUT,),
        in_specs=[row(D_MODEL), row(MLA_HEADS * MLA_V), row(HG_W), full(mla_g),
                  full(w_o), full(l1g), full(l1b), full(w_r), full(b_r)],
        out_specs=[row(D_MODEL), row(D_MODEL), col, pl.BlockSpec((N_EXPERTS, TM_OUT), lambda i: (0, i)),
                   pl.BlockSpec((sub_tiles, N_EXPERTS, 1), lambda i: (i, 0, 0))],
        out_shape=[jax.ShapeDtypeStruct((T, D_MODEL), F32), jax.ShapeDtypeStruct((T, D_MODEL), BF16),
                   jax.ShapeDtypeStruct((TOP_K, T), I32), jax.ShapeDtypeStruct((N_EXPERTS, T), F32),
                   jax.ShapeDtypeStruct((T // TM_DISP, N_EXPERTS, 1), F32)],
        compiler_params=_cparams(("parallel",)),
        name="out_proj_router",
    )(h, attn, hgo, mla_g, w_o, l1g, l1b, w_r, b_r)


TILE_SLOTS = TOP_K * TM_DISP + N_EXPERTS * SUBLANES
ROW_WORDS = D_MODEL // 2 + LANES
TILES_PER_STEP = 2
TM_STEP = TILES_PER_STEP * TM_DISP
RUN_UNROLL = 4

def _ceil_to(n, m):
    return (n + m - 1) // m * m


def _rows(start, size):
    return pl.ds(pl.multiple_of(start, SUBLANES), size)


def _for_each_piece(n, fn, max_rows=TM_DISP):
    for kbit in reversed(range(SUBLANES.bit_length() - 1, max_rows.bit_length())):
        size = 1 << kbit

        @pl.when((n & size) != 0)
        def _():
            fn(pl.multiple_of((n >> (kbit + 1)) << (kbit + 1), SUBLANES), size)


def _pack_exact_pair(lo, hi):
    return pltpu.bitcast(hi, I32) | lax.shift_right_logical(pltpu.bitcast(lo, I32), 16)


def _zero_tail_blocks(zeros_ref, out_hbm, n_used, zsem):
    def blk_copy(g):
        return pltpu.make_async_copy(zeros_ref.at[pl.ds(0, ROW_UNIT)],
                                     out_hbm.at[_rows(g * ROW_UNIT, ROW_UNIT)], zsem)

    n_total = out_hbm.shape[0] // ROW_UNIT
    lax.fori_loop(n_used, n_total, lambda g, c: (blk_copy(g).start(), c)[1], 0)
    lax.fori_loop(n_used, n_total, lambda g, c: (blk_copy(g).wait(), c)[1], 0)


def _dispatch_body(segn_ref, segsrc_ref, segdst_ref, tot_ref, padn_ref, padd_ref, nused_ref, pos_ref, xb_ref, g_ref,
                   xs_out, stage, zeros_ref, sem, zsem):
    i = pl.program_id(0)

    @pl.when(i == 0)
    def _():
        zeros_ref[...] = jnp.zeros_like(zeros_ref)

        def pad_copy(e, off, size):
            return pltpu.make_async_copy(zeros_ref.at[pl.ds(0, size)],
                                         xs_out.at[_rows(padd_ref[e] + off, size)], zsem)

        def fill(e, c):
            _for_each_piece(padn_ref[e], lambda off, size: pad_copy(e, off, size).start())
            return c

        def drain(e, c):
            _for_each_piece(padn_ref[e], lambda off, size: pad_copy(e, off, size).wait())
            return c

        lax.fori_loop(0, N_EXPERTS, fill, 0)
        lax.fori_loop(0, N_EXPERTS, drain, 0)
        _zero_tail_blocks(zeros_ref, xs_out, nused_ref[0], zsem)

    def settle(tile, b):
        _for_each_piece(tot_ref[tile], lambda off, size: pltpu.make_async_copy(
            stage.at[b, pl.ds(0, size)], xs_out.at[pl.ds(0, size)], sem.at[b]).wait(), TILE_SLOTS)

    for b in range(TILES_PER_STEP):
        tile = i * TILES_PER_STEP + b
        ts = slice(b * TM_DISP, (b + 1) * TM_DISP)
        pos = pos_ref[:, ts]
        slot = lax.broadcasted_iota(I32, (TILE_SLOTS, TM_DISP), 0)
        sel = jnp.zeros((TILE_SLOTS, TM_DISP), F32)
        for j in range(TOP_K):
            sel = jnp.where(slot == pos[j:j + 1, :], 1.0, sel)
        g_hi, g_lo = _split_bf16(g_ref[:, ts])
        e_sub = lax.broadcasted_iota(I32, (N_EXPERTS, LANES), 0)
        e_lane = lax.broadcasted_iota(I32, (N_EXPERTS, LANES), 1)
        tn_dims = (((0,), (0,)), ((), ()))
        gx = (lax.dot_general(g_hi, (e_lane == e_sub).astype(BF16), tn_dims, preferred_element_type=F32)
              + lax.dot_general(g_lo, (e_lane == e_sub + N_EXPERTS).astype(BF16), tn_dims,
                                preferred_element_type=F32))
        src_rows = jnp.concatenate([xb_ref[ts, :], gx.astype(BF16)], axis=1)
        rows = jnp.dot(sel.astype(BF16), src_rows, preferred_element_type=F32)
        packed = jnp.concatenate(
            [_pack_exact_pair(rows[:, :D_MODEL // 2], rows[:, D_MODEL // 2:D_MODEL]),
             pltpu.bitcast(rows[:, D_MODEL:], I32)], axis=1)

        @pl.when(i >= 1)
        def _():
            settle(tile - TILES_PER_STEP, b)

        stage[b] = packed

        def send(eb, c):
            for u in range(RUN_UNROLL):
                k = tile * N_EXPERTS + eb * RUN_UNROLL + u
                src, dst = segsrc_ref[k], segdst_ref[k]
                _for_each_piece(segn_ref[k], lambda off, size: pltpu.make_async_copy(
                    stage.at[b, _rows(src + off, size)], xs_out.at[_rows(dst + off, size)],
                    sem.at[b]).start())
            return c

        lax.fori_loop(0, N_EXPERTS // RUN_UNROLL, send, 0)

    @pl.when(i == pl.num_programs(0) - 1)
    def _():
        for b in range(TILES_PER_STEP):
            settle(i * TILES_PER_STEP + b, b)


def _dispatch(seg_n, seg_src, seg_dst, tile_tot, pad_n, pad_dst, n_used, pos_t, h1b, gates_x, n_rows):
    T = h1b.shape[0]
    grid_spec = pltpu.PrefetchScalarGridSpec(
        num_scalar_prefetch=7,
        grid=(T // TM_STEP,),
        in_specs=[pl.BlockSpec((TOP_K, TM_STEP), lambda i, *_: (0, i)),
                  pl.BlockSpec((TM_STEP, D_MODEL), lambda i, *_: (i, 0)),
                  pl.BlockSpec((N_EXPERTS, TM_STEP), lambda i, *_: (0, i))],
        out_specs=pl.BlockSpec(memory_space=pl.ANY),
        scratch_shapes=[pltpu.VMEM((TILES_PER_STEP, TILE_SLOTS, ROW_WORDS), I32),
                        pltpu.VMEM((ROW_BLOCK, ROW_WORDS), I32),
                        pltpu.SemaphoreType.DMA((TILES_PER_STEP,)), pltpu.SemaphoreType.DMA(())],
    )
    return pl.pallas_call(
        _dispatch_body,
        grid_spec=grid_spec,
        out_shape=jax.ShapeDtypeStruct((n_rows, ROW_WORDS), I32),
        compiler_params=_cparams(("arbitrary",)),
        name="moe_dispatch",
    )(seg_n, seg_src, seg_dst, tile_tot, pad_n, pad_dst, n_used, pos_t, h1b, gates_x)


def _experts_body(istart_ref, icount_ref, iunit_ref, ibig_ref, nitems_ref, uused_ref,
                  xs_hbm, wu_ref, bu_ref, wd_ref, bd_ref, ys_hbm,
                  wu_bf, wd_bf, xbuf, ybuf, sem_in, sem_out):
    e = pl.program_id(0)
    n_items = nitems_ref[0]
    nb = icount_ref[e]

    def by_size(k, fn):
        code = ibig_ref[k]
        for c in range(3):
            pl.when(code == c)(lambda c=c: fn(ROW_UNIT << c))

    def fetch(k, slot, size):
        rows = _rows(iunit_ref[k] * ROW_UNIT, size)
        return pltpu.make_async_copy(xs_hbm.at[rows], xbuf.at[slot, pl.ds(0, size)], sem_in.at[slot])

    def put(k, slot, size):
        rows = _rows(iunit_ref[k] * ROW_UNIT, size)
        return pltpu.make_async_copy(ybuf.at[slot, pl.ds(0, size)], ys_hbm.at[rows], sem_out.at[slot])

    @pl.when(jnp.logical_and(e == 0, n_items > 0))
    def _():
        by_size(0, lambda size: fetch(0, 0, size).start())

    @pl.when(nb > 0)
    def _():
        wu_bf[...] = wu_ref[0].astype(BF16)
        wd_bf[...] = wd_ref[0].astype(BF16)

    def compute(k, slot, size):
        row = xbuf[slot, :size]
        lo, hi = _unpack_bf16_pair(row[:, :D_MODEL // 2])
        x = jnp.concatenate([lo, hi], axis=1).astype(BF16)
        lane = lax.broadcasted_iota(I32, (size, LANES), 1)
        mine = jnp.logical_or(lane == e, lane == e + N_EXPERTS)
        gate = jnp.sum(jnp.where(mine, pltpu.bitcast(row[:, D_MODEL // 2:], F32), 0.0),
                       axis=1, keepdims=True)
        hb = jnp.dot(x, wu_bf[...], preferred_element_type=F32) + bu_ref[0]
        glu = jnp.minimum(hb[:, :D_EXPERT], SWIGLU_LIMIT)
        lin = jnp.clip(hb[:, D_EXPERT:], -SWIGLU_LIMIT, SWIGLU_LIMIT)
        act = glu * (1.0 / (1.0 + jnp.exp(-SWIGLU_ALPHA * glu))) * (lin + 1.0)
        y = (jnp.dot(act.astype(BF16), wd_bf[...], preferred_element_type=F32) + bd_ref[0]) * gate
        ybuf[slot, :size] = _pack_bf16_pair(y[:, :D_MODEL // 2], y[:, D_MODEL // 2:])
        put(k, slot, size).start()

    def block(b, c):
        k = istart_ref[e] + b
        slot = lax.rem(k, 2)
        by_size(k, lambda size: fetch(k, slot, size).wait())

        @pl.when(k + 1 < n_items)
        def _():
            by_size(k + 1, lambda size: fetch(k + 1, 1 - slot, size).start())

        @pl.when(k >= 2)
        def _():
            by_size(k - 2, lambda size: put(k - 2, slot, size).wait())

        by_size(k, lambda size: compute(k, slot, size))
        return c

    lax.fori_loop(0, nb, block, 0)

    @pl.when(e == N_EXPERTS - 1)
    def _():
        for back in (2, 1):
            @pl.when(n_items >= back)
            def _():
                k = n_items - back
                by_size(k, lambda size: put(k, lax.rem(k, 2), size).wait())
        ybuf[0] = jnp.zeros_like(ybuf[0])
        _zero_tail_blocks(ybuf.at[0], ys_hbm, uused_ref[0], sem_out.at[0])


def _experts(item_start, item_count, item_unit, item_big, n_items, units_used, xs,
             w_up, b_up, w_down, b_down):
    n_rows = xs.shape[0]
    grid_spec = pltpu.PrefetchScalarGridSpec(
        num_scalar_prefetch=6,
        grid=(N_EXPERTS,),
        in_specs=[pl.BlockSpec(memory_space=pl.ANY),
                  pl.BlockSpec((1, D_MODEL, 2 * D_EXPERT), lambda e, *_: (e, 0, 0)),
                  pl.BlockSpec((1, 1, 2 * D_EXPERT), lambda e, *_: (e, 0, 0)),
                  pl.BlockSpec((1, D_EXPERT, D_MODEL), lambda e, *_: (e, 0, 0)),
                  pl.BlockSpec((1, 1, D_MODEL), lambda e, *_: (e, 0, 0))],
        out_specs=pl.BlockSpec(memory_space=pl.ANY),
        scratch_shapes=[pltpu.VMEM((D_MODEL, 2 * D_EXPERT), BF16), pltpu.VMEM((D_EXPERT, D_MODEL), BF16),
                        pltpu.VMEM((2, ROW_BLOCK, ROW_WORDS), I32),
                        pltpu.VMEM((2, ROW_BLOCK, D_MODEL // 2), I32),
                        pltpu.SemaphoreType.DMA((2,)), pltpu.SemaphoreType.DMA((2,))],
    )
    return pl.pallas_call(
        _experts_body,
        grid_spec=grid_spec,
        out_shape=jax.ShapeDtypeStruct((n_rows, D_MODEL // 2), I32),
        compiler_params=_cparams(("arbitrary",)),
        name="moe_experts",
    )(item_start, item_count, item_unit, item_big, n_items, units_used, xs, w_up,
      b_up.reshape(N_EXPERTS, 1, 2 * D_EXPERT), w_down, b_down.reshape(N_EXPERTS, 1, D_MODEL))


def _combine_body(segn_ref, segsrc_ref, segdst_ref, tot_ref, pos_ref, h1_ref, l2g_ref, l2b_ref, ys_hbm,
                  o_ref, stage, sem):
    i = pl.program_id(0)
    n_tiles = pl.num_programs(0) * TILES_PER_STEP

    def fetch_tile(tile, b):
        def fetch(eb, c):
            for u in range(RUN_UNROLL):
                k = tile * N_EXPERTS + eb * RUN_UNROLL + u
                src, dst = segsrc_ref[k], segdst_ref[k]
                _for_each_piece(segn_ref[k], lambda off, size: pltpu.make_async_copy(
                    ys_hbm.at[_rows(dst + off, size)], stage.at[b, _rows(src + off, size)],
                    sem.at[b]).start())
            return c

        lax.fori_loop(0, N_EXPERTS // RUN_UNROLL, fetch, 0)

    @pl.when(i == 0)
    def _():
        stage[...] = jnp.zeros_like(stage)
        fetch_tile(0, 0)

    for b in range(TILES_PER_STEP):
        tile = i * TILES_PER_STEP + b
        nxt = (b + 1) % TILES_PER_STEP

        @pl.when(tile + 1 < n_tiles)
        def _():
            fetch_tile(tile + 1, nxt)

        ts = slice(b * TM_DISP, (b + 1) * TM_DISP)
        pos = pos_ref[ts, :]
        slot = lax.broadcasted_iota(I32, (TM_DISP, TILE_SLOTS), 1)
        w = jnp.zeros((TM_DISP, TILE_SLOTS), F32)
        for j in range(TOP_K):
            w = jnp.where(slot == pos[:, j:j + 1], 1.0, w)
        w = w.astype(BF16)
        _for_each_piece(tot_ref[tile], lambda off, size: pltpu.make_async_copy(
            ys_hbm.at[pl.ds(0, size)], stage.at[b, pl.ds(0, size)], sem.at[b]).wait(), TILE_SLOTS)
        lo, hi = _unpack_bf16_pair(stage[b])
        y = jnp.concatenate([lo, hi], axis=1).astype(BF16)
        ffn = jnp.dot(w, y, preferred_element_type=F32)
        o_ref[ts, :] = _layer_norm(DEEPNORM_ALPHA * h1_ref[ts, :] + ffn, l2g_ref[...], l2b_ref[...])


def _combine(seg_n, seg_src, seg_dst, tile_tot, pos, h1, l2g, l2b, ys):
    T = h1.shape[0]
    grid_spec = pltpu.PrefetchScalarGridSpec(
        num_scalar_prefetch=4,
        grid=(T // TM_STEP,),
        in_specs=[pl.BlockSpec((TM_STEP, TOP_K), lambda i, *_: (i, 0)),
                  pl.BlockSpec((TM_STEP, D_MODEL), lambda i, *_: (i, 0)),
                  pl.BlockSpec(l2g.shape, lambda i, *_: (0, 0)),
                  pl.BlockSpec(l2b.shape, lambda i, *_: (0, 0)),
                  pl.BlockSpec(memory_space=pl.ANY)],
        out_specs=pl.BlockSpec((TM_STEP, D_MODEL), lambda i, *_: (i, 0)),
        scratch_shapes=[pltpu.VMEM((TILES_PER_STEP, TILE_SLOTS, D_MODEL // 2), I32),
                        pltpu.SemaphoreType.DMA((TILES_PER_STEP,))],
    )
    return pl.pallas_call(
        _combine_body,
        grid_spec=grid_spec,
        out_shape=jax.ShapeDtypeStruct((T, D_MODEL), F32),
        compiler_params=_cparams(("arbitrary",)),
        name="moe_combine",
    )(seg_n, seg_src, seg_dst, tile_tot, pos, h1, l2g, l2b, ys)


def _prep_weights(w_in, w_q_b, w_kv_b):
    w_kr = jnp.pad(w_in[:, KR_OFF:KR_OFF + MLA_ROPE], ((0, 0), (MLA_NOPE, HEAD_PAD - MLA_NOPE - MLA_ROPE)))
    w_in_p = (jnp.concatenate([w_in[:, :KR_OFF], w_kr], axis=1).astype(BF16),
              w_in[:, KR_OFF + MLA_ROPE:].astype(BF16))
    qd = MLA_NOPE + MLA_ROPE
    wq = w_q_b.reshape(Q_LORA, MLA_HEADS, qd)
    wq = jnp.pad(wq, ((0, 0), (0, 0), (0, HEAD_PAD - qd))).reshape(Q_LORA, MLA_HEADS * HEAD_PAD)
    wkv = w_kv_b.reshape(KV_LORA, MLA_HEADS, MLA_NOPE + MLA_V)
    wk = jnp.pad(wkv[:, :, :MLA_NOPE], ((0, 0), (0, 0), (0, HEAD_PAD - MLA_NOPE)))
    wk = wk.reshape(KV_LORA, MLA_HEADS * HEAD_PAD)
    wv = jnp.pad(wkv[:, :, MLA_NOPE:], ((0, 0), (0, 0), (0, HEAD_PAD - MLA_V)))
    wv = wv.reshape(KV_LORA, MLA_HEADS * HEAD_PAD)
    return w_in_p, wq.astype(BF16), jnp.concatenate([wk, wv], axis=1).astype(BF16)


def kernel(x, positions, ln_in_g, ln_in_b, w_in, q_a_norm_g, w_q_b, kv_a_norm_g, w_kv_b, hgrn_lb_logits,
           mla_out_g, hgrn_out_g, w_o, ln1_g, ln1_b, w_router, b_router, w_up, b_up, w_down, b_down,
           ln2_g, ln2_b):
    B, S, D = x.shape
    assert D == D_MODEL and w_in.shape[0] == DEPTH and hgrn_lb_logits.shape[0] == DEPTH + 1
    T = B * S
    xt = x.reshape(T, D)
    r2 = lambda a: a.reshape(1, -1)

    cos, sin = _rope_cos_sin(positions)
    w_in_p, w_qb_p, w_kv_p = _prep_weights(w_in[0], w_q_b[0], w_kv_b[0])
    q, k, v, hq, hk, hg, hv, gate, h0 = _in_proj(
        xt, r2(ln_in_g), r2(ln_in_b), w_in_p, q_a_norm_g, w_qb_p, kv_a_norm_g, w_kv_p,
        hgrn_lb_logits, cos, sin)
    attn = _attention(q, k, v, B, S)
    hgo = _hgrn(hq, hk, hg, hv, gate, hgrn_out_g, B, S)
    h1, h1b, pos_t, gate_t, tile_cnt = _out_proj(
        h0, attn, hgo, mla_out_g, w_o[0].astype(BF16), ln1_g, ln1_b,
        w_router[0].T, b_router.reshape(N_EXPERTS, 1))

    n_tiles = T // TM_DISP
    n_rows = T * TOP_K + n_tiles * N_EXPERTS * (SUBLANES - 1) + N_EXPERTS * ROW_UNIT
    n_rows = _ceil_to(n_rows, ROW_BLOCK)
    tc = _ceil_to(tile_cnt[:, :, 0].astype(I32), SUBLANES)
    cnt = jnp.sum(tc, axis=0)
    padded = _ceil_to(cnt, ROW_UNIT)
    pad_start = jnp.cumsum(padded) - padded
    seg_n = tc.reshape(-1)
    seg_src = (jnp.cumsum(tc, axis=1) - tc).reshape(-1)
    seg_dst = (pad_start[None, :] + jnp.cumsum(tc, axis=0) - tc).reshape(-1)
    units_used = jnp.sum(padded, keepdims=True) // ROW_UNIT

    units = padded // ROW_UNIT
    n_full, left = units // 4, units % 4
    item_count = n_full + left // 2 + left % 2
    item_end = jnp.cumsum(item_count)
    item_start = item_end - item_count
    k = jnp.arange(n_rows // ROW_BLOCK + 2 * N_EXPERTS, dtype=I32)
    of_expert = (jnp.minimum(jnp.sum((item_end[None, :] <= k[:, None]).astype(I32), axis=1), N_EXPERTS - 1)
                 [:, None] == jnp.arange(N_EXPERTS, dtype=I32)[None, :])
    pick = lambda v: jnp.sum(jnp.where(of_expert, v[None, :], 0), axis=1)
    local, full_k, left_k = k - pick(item_start), pick(n_full), pick(left)
    half_first = jnp.logical_and(local == full_k, left_k >= 2)
    item_big = jnp.where(local < full_k, 2, half_first.astype(I32))
    item_unit = pick(pad_start // ROW_UNIT) + jnp.where(
        local <= full_k, 4 * local, 4 * full_k + 2)

    tile_tot = jnp.sum(tc, axis=1)
    xs = _dispatch(seg_n, seg_src, seg_dst, tile_tot, padded - cnt, pad_start + cnt, units_used, pos_t, h1b,
                   gate_t, n_rows)
    ys = _experts(item_start, item_count, item_unit, item_big, item_end[-1:], units_used, xs,
                  w_up[0], b_up[0], w_down[0], b_down[0])
    out = _combine(seg_n, seg_src, seg_dst, tile_tot, pos_t.T, h1, ln2_g, ln2_b, ys)
    return out.reshape(B, S, D)
```

```python
import numpy as np
import jax
import jax.numpy as jnp
from jax import lax
from jax.experimental import pallas as pl
from jax.experimental.pallas import tpu as pltpu

F32 = jnp.float32
BF16 = jnp.bfloat16
I32 = jnp.int32

D_MODEL = 1024
CHUNK = 64
MLA_HEADS = 8
MLA_NOPE = 64
MLA_ROPE = 32
MLA_V = 64
Q_LORA = 384
KV_LORA = 256
ROPE_THETA = 10000.0
HG_HEADS = 4
HG_DK = 128
HG_DV = 128
HG_W = HG_HEADS * HG_DK
N_EXPERTS = 32
TOP_K = 4
D_EXPERT = 1024
SWIGLU_LIMIT = 7.0
SWIGLU_ALPHA = 1.702
LN_EPS = 1e-5
RMS_EPS = 1e-6
DEPTH = 1
DEEPNORM_ALPHA = (2.0 * DEPTH) ** 0.25
LOG2E = 1.4426950408889634

LANES = 128
SUBLANES = 8
HEAD_PAD = 128
HALF_ROPE = MLA_ROPE // 2
KR_OFF = Q_LORA + KV_LORA
HQ_OFF = KR_OFF + HEAD_PAD
IN_COLS_PAD = HQ_OFF + 4 * HG_W

TM_IN = 512
TQ = 512
HG_ROWS = 256
TM_OUT = 1024
TM_DISP = 256
ROW_BLOCK = 512
ROW_UNIT = ROW_BLOCK // 4
VMEM_LIMIT = 52 * 1024 * 1024


def _cparams(sem, vmem=VMEM_LIMIT):
    return pltpu.CompilerParams(dimension_semantics=sem, vmem_limit_bytes=vmem)


def _layer_norm(x, g, b):
    mu = jnp.mean(x, axis=-1, keepdims=True)
    xc = x - mu
    var = jnp.mean(xc * xc, axis=-1, keepdims=True)
    return xc * lax.rsqrt(var + LN_EPS) * g + b


def _rms_norm(x, g):
    return x * lax.rsqrt(jnp.mean(x * x, axis=-1, keepdims=True) + RMS_EPS) * g


def _split_bf16(a):
    hi = a.astype(BF16)
    return hi, (a - hi.astype(F32)).astype(BF16)


def _silu(x):
    return x * (1.0 / (1.0 + jnp.exp(-x)))


def _sigmoid_pair(x):
    e = jnp.exp(-jnp.abs(x))
    big = 1.0 / (1.0 + e)
    small = e * big
    pos = x >= 0
    return jnp.where(pos, big, small), jnp.where(pos, small, big)


def _cos_sin_body(ang_ref, cos_ref, sin_ref):
    a = ang_ref[...]
    cos_ref[...] = jnp.cos(a)
    sin_ref[...] = jnp.sin(a)


def _rope_cos_sin(positions):
    T = positions.size
    inv_freq = ROPE_THETA ** (-jnp.arange(0, MLA_ROPE, 2, dtype=F32) / MLA_ROPE)
    per_row = LANES // HALF_ROPE
    pos = jnp.repeat(positions.reshape(T // per_row, per_row).astype(F32), HALF_ROPE, axis=1)
    ang = pos * jnp.tile(inv_freq, per_row)
    cos, sin = pl.pallas_call(
        _cos_sin_body,
        out_shape=(jax.ShapeDtypeStruct(ang.shape, F32),) * 2,
        name="rope_cos_sin",
    )(ang)
    return cos.reshape(T, HALF_ROPE), sin.reshape(T, HALF_ROPE)


def _rope_placement():
    pc = np.zeros((HALF_ROPE, HEAD_PAD), np.float32)
    ps = np.zeros((HALF_ROPE, 2 * HEAD_PAD), np.float32)
    for j in range(HALF_ROPE):
        pc[j, MLA_NOPE + j] = pc[j, MLA_NOPE + HALF_ROPE + j] = 1.0
        ps[j, MLA_NOPE + HALF_ROPE + j] = 1.0
        ps[j, HEAD_PAD + MLA_NOPE + j] = -1.0
    return jnp.asarray(pc, BF16), jnp.asarray(ps, BF16)


def _rope_tables(cos, sin, pc, ps):
    place = lambda a, p: sum(jnp.dot(part, p, preferred_element_type=F32) for part in _split_bf16(a))
    lane = lax.broadcasted_iota(I32, (1, HEAD_PAD), 1)
    c = place(cos, pc) + (lane < MLA_NOPE).astype(F32)
    s = place(sin, ps)
    return c, s[:, :HEAD_PAD], s[:, HEAD_PAD:]


def _rope(x, c, s1, s2, reps):
    width = reps * HEAD_PAD
    if reps > 1:
        c = jnp.concatenate([c] * reps, axis=1)
        s1 = jnp.concatenate([s1] * reps, axis=1)
        s2 = jnp.concatenate([s2] * reps, axis=1)
    return (x * c + pltpu.roll(x, HALF_ROPE, 1) * s1
            + pltpu.roll(x, width - HALF_ROPE, 1) * s2)


def _in_proj_body(x_ref, lng_ref, lnb_ref, wa_ref, wh_ref, qag_ref, wqb_ref, kvg_ref, wkv_ref, lbl_ref,
                  cos_ref, sin_ref, pc_ref, ps_ref,
                  q_out, k_out, v_out, hq_out, hk_out, hg_out, hv_out, gate_out, h_out):
    win_refs = (wa_ref, wh_ref)
    h = _layer_norm(x_ref[...], lng_ref[...], lnb_ref[...])
    h_out[...] = h
    hb = h.astype(BF16)
    p = jnp.concatenate([jnp.dot(hb, w_ref[...], preferred_element_type=F32) for w_ref in win_refs],
                        axis=1)
    c, s1, s2 = _rope_tables(cos_ref[...], sin_ref[...], pc_ref[...], ps_ref[...])

    qn = _rms_norm(p[:, :Q_LORA], qag_ref[...]).astype(BF16)
    q = jnp.dot(qn, wqb_ref[...], preferred_element_type=F32)
    scale = (MLA_NOPE + MLA_ROPE) ** -0.5 * LOG2E
    q_out[...] = (_rope(q, c, s1, s2, MLA_HEADS) * scale).astype(BF16)
    kvn = _rms_norm(p[:, Q_LORA:KR_OFF], kvg_ref[...]).astype(BF16)
    kv = jnp.dot(kvn, wkv_ref[...], preferred_element_type=F32)
    kr = _rope(p[:, KR_OFF:HQ_OFF], c, s1, s2, 1)
    k_out[...] = (kv[:, :MLA_HEADS * HEAD_PAD] + jnp.concatenate([kr] * MLA_HEADS, axis=1)).astype(BF16)
    lane = lax.broadcasted_iota(I32, (1, MLA_HEADS * HEAD_PAD), 1)
    ones_lane = (lane % HEAD_PAD == MLA_V).astype(F32)
    v_out[...] = (kv[:, MLA_HEADS * HEAD_PAD:] + ones_lane).astype(BF16)

    lbl = lbl_ref[...]
    le = jnp.exp(lbl - jnp.max(lbl, axis=0, keepdims=True))
    lb = le[0:1] / jnp.sum(le, axis=0, keepdims=True)
    hq = p[:, HQ_OFF:HQ_OFF + HG_W]
    hf = p[:, HQ_OFF + HG_W:HQ_OFF + 2 * HG_W]
    hi = p[:, HQ_OFF + 2 * HG_W:HQ_OFF + 3 * HG_W]
    hg = p[:, HQ_OFF + 3 * HG_W:HQ_OFF + 4 * HG_W]
    hq_out[...] = _silu(hq).astype(BF16)
    sg, one_minus_sg = _sigmoid_pair(hf)
    hk_out[...] = ((1.0 - lb) * one_minus_sg).astype(BF16)
    hg_out[...] = jnp.log(lb + (1.0 - lb) * sg)
    hv_out[...] = hi.astype(BF16)
    gate_out[...] = _silu(hg).astype(BF16)


def _in_proj(xt, ln_g, ln_b, w_in_p, qa_g, w_qb_p, kv_g, w_kv_p, lb_logits, cos, sin):
    T = xt.shape[0]
    pc, ps = _rope_placement()
    row = lambda w: pl.BlockSpec((TM_IN, w), lambda i: (i, 0))
    full = lambda a: pl.BlockSpec(a.shape, lambda i: (0, 0))
    outs = [(MLA_HEADS * HEAD_PAD, BF16), (MLA_HEADS * HEAD_PAD, BF16), (MLA_HEADS * HEAD_PAD, BF16),
            (HG_W, BF16), (HG_W, BF16), (HG_W, F32), (HG_W, BF16), (HG_W, BF16), (D_MODEL, F32)]
    return pl.pallas_call(
        _in_proj_body,
        grid=(T // TM_IN,),
        in_specs=[row(D_MODEL), full(ln_g), full(ln_b), *[full(w) for w in w_in_p], full(qa_g), full(w_qb_p),
                  full(kv_g), full(w_kv_p), full(lb_logits), row(HALF_ROPE), row(HALF_ROPE),
                  full(pc), full(ps)],
        out_specs=[row(w) for w, _ in outs],
        out_shape=[jax.ShapeDtypeStruct((T, w), d) for w, d in outs],
        compiler_params=_cparams(("parallel",)),
        name="in_proj",
    )(xt, ln_g, ln_b, *w_in_p, qa_g, w_qb_p, kv_g, w_kv_p, lb_logits, cos, sin, pc, ps)


def _attn_body(q_ref, k_ref, v_ref, o_ref):
    n_q = q_ref.shape[1] // TQ
    dn = (((1,), (1,)), ((), ()))
    row_c = lax.broadcasted_iota(I32, (TQ, TQ), 0) // CHUNK
    col_c = lax.broadcasted_iota(I32, (TQ, TQ), 1) // CHUNK
    diag_mask = col_c <= row_c
    heads = [slice(hh * HEAD_PAD, (hh + 1) * HEAD_PAD) for hh in range(2)]
    lane = lax.broadcasted_iota(I32, (TQ, 2 * MLA_V), 1)
    for qi in range(n_q):
        r0, kv = qi * TQ, (qi + 1) * TQ
        res = []
        for hs in heads:
            s = lax.dot_general(q_ref[0, r0:kv, hs], k_ref[0, :kv, hs], dn, preferred_element_type=F32)
            s_diag = jnp.where(diag_mask, s[:, r0:], -jnp.inf)
            m = jnp.max(s_diag, axis=-1, keepdims=True)
            if qi:
                m = jnp.maximum(m, jnp.max(s[:, :r0], axis=-1, keepdims=True))
                p = jnp.concatenate([jnp.exp2(s[:, :r0] - m), jnp.exp2(s_diag - m)], axis=1)
            else:
                p = jnp.exp2(s_diag - m)
            a = jnp.dot(p.astype(BF16), v_ref[0, :kv, hs], preferred_element_type=F32)
            res.append(a * (1.0 / a[:, MLA_V:MLA_V + 1]))
        o_ref[0, r0:kv, :] = jnp.where(lane < MLA_V, res[0], pltpu.roll(res[1], MLA_V, 1)).astype(BF16)


def _attention(q, k, v, B, S):
    w = MLA_HEADS * HEAD_PAD
    blk = pl.BlockSpec((1, S, 2 * HEAD_PAD), lambda b, h: (b, 0, h))
    out = pl.pallas_call(
        _attn_body,
        grid=(B, MLA_HEADS // 2),
        in_specs=[blk, blk, blk],
        out_specs=pl.BlockSpec((1, S, 2 * MLA_V), lambda b, h: (b, 0, h)),
        out_shape=jax.ShapeDtypeStruct((B, S, MLA_HEADS * MLA_V), BF16),
        compiler_params=_cparams(("parallel", "parallel")),
        name="mla_attention",
    )(q.reshape(B, S, w), k.reshape(B, S, w), v.reshape(B, S, w))
    return out.reshape(B * S, MLA_HEADS * MLA_V)


N_LEVELS = 6
Z_BLOCKS = N_LEVELS + 1


def _hgrn_constants():
    z = np.zeros((Z_BLOCKS, CHUNK, CHUNK), np.float32)
    masks = np.zeros((N_LEVELS, CHUNK, CHUNK), np.float32)
    idx = np.arange(CHUNK)
    for l in range(N_LEVELS):
        hs = CHUNK >> (l + 1)
        parent = idx // (2 * hs)
        right = (idx % (2 * hs)) >= hs
        mid = parent * 2 * hs + hs
        for t in range(CHUNK):
            z[l, t, :mid[t]] = 1.0
        masks[l] = (parent[:, None] == parent[None, :]) & right[:, None] & (~right)[None, :]
    z[N_LEVELS] = np.tril(np.ones((CHUNK, CHUNK), np.float32))
    z = z.reshape(Z_BLOCKS * CHUNK, CHUNK)
    tile_masks = np.zeros((N_LEVELS, LANES, LANES), np.float32)
    for c in range(LANES // CHUNK):
        tile_masks[:, c * CHUNK:(c + 1) * CHUNK, c * CHUNK:(c + 1) * CHUNK] = masks
    return np.concatenate([z, z], axis=1), tile_masks


def _hgrn_rows(r0, q_ref, k_ref, g_ref, v_ref, gate_ref, og_ref, z_ref, mask_ref, o_ref, state_ref):
    dn_t = (((1,), (1,)), ((), ()))
    n_c = HG_ROWS // CHUNK
    n_t = HG_ROWS // LANES
    rr = slice(r0, r0 + HG_ROWS)
    eye = (lax.broadcasted_iota(I32, (LANES, LANES), 0)
           == lax.broadcasted_iota(I32, (LANES, LANES), 1)).astype(F32)
    sums = []
    for c in range(n_c):
        g2 = g_ref[0, r0 + c * CHUNK:r0 + (c + 1) * CHUNK, :] * LOG2E
        g_hi, g_lo = _split_bf16(g2)
        sums.append(jnp.dot(z_ref[...], jnp.concatenate([g_hi, g_lo], axis=0),
                            preferred_element_type=F32))
    level = lambda l, cols: jnp.concatenate([s[l * CHUNK:(l + 1) * CHUNK, cols] for s in sums], axis=0)
    for h in range(HG_HEADS):
        cols = slice(h * HG_DK, (h + 1) * HG_DK)
        b = level(N_LEVELS, cols)
        q = q_ref[0, rr, cols].astype(F32)
        k = k_ref[0, rr, cols].astype(F32)
        v = v_ref[0, rr, cols]
        diag = jnp.sum(q * k, axis=-1, keepdims=True)
        a = [eye * diag[t * LANES:(t + 1) * LANES] for t in range(n_t)]
        for l in range(N_LEVELS):
            d = jnp.exp2(-jnp.abs(b - level(l, cols)))
            s = lax.dot_general((q * d).astype(BF16), (k * d).astype(BF16), dn_t,
                                preferred_element_type=F32)
            for t in range(n_t):
                ts = slice(t * LANES, (t + 1) * LANES)
                a[t] = a[t] + mask_ref[l] * s[ts, ts]
        o_intra = jnp.concatenate(
            [jnp.dot(a[t].astype(BF16), v[t * LANES:(t + 1) * LANES], preferred_element_type=F32)
             for t in range(n_t)], axis=0)
        d_in = jnp.exp2(b)
        qd = (q * d_in).astype(BF16)
        st = state_ref[h]
        o_parts = []
        for c in range(n_c):
            rows = slice(c * CHUNK, (c + 1) * CHUNK)
            last = slice((c + 1) * CHUNK - 1, (c + 1) * CHUNK)
            o_parts.append(o_intra[rows] + lax.dot_general(qd[rows], st.astype(BF16), dn_t,
                                                           preferred_element_type=F32))
            kh = (k[rows] * jnp.exp2(b[last] - b[rows])).astype(BF16)
            upd = lax.dot_general(v[rows], kh, (((0,), (0,)), ((), ())), preferred_element_type=F32)
            st = st * d_in[last] + upd
        state_ref[h] = st
        o = jnp.concatenate(o_parts, axis=0)
        o = _rms_norm(o, og_ref[:, cols]) * gate_ref[0, rr, cols].astype(F32)
        o_ref[0, rr, cols] = o.astype(BF16)


def _hgrn_body(q_ref, k_ref, g_ref, v_ref, gate_ref, og_ref, z_ref, mask_ref, o_ref, state_ref):
    @pl.when(pl.program_id(1) == 0)
    def _():
        state_ref[...] = jnp.zeros_like(state_ref)

    _hgrn_rows(0, q_ref, k_ref, g_ref, v_ref, gate_ref, og_ref, z_ref, mask_ref, o_ref, state_ref)


def _hgrn(hq, hk, hg, hv, gate, out_g, B, S):
    zc, masks = _hgrn_constants()
    zc = jnp.asarray(zc, BF16)
    masks = jnp.asarray(masks, F32)
    blk = pl.BlockSpec((1, HG_ROWS, HG_W), lambda b, i: (b, i, 0))
    r3 = lambda a: a.reshape(B, S, HG_W)
    out = pl.pallas_call(
        _hgrn_body,
        grid=(B, S // HG_ROWS),
        in_specs=[blk, blk, blk, blk, blk,
                  pl.BlockSpec((1, HG_W), lambda b, i: (0, 0)),
                  pl.BlockSpec(zc.shape, lambda b, i: (0, 0)),
                  pl.BlockSpec(masks.shape, lambda b, i: (0, 0, 0))],
        out_specs=blk,
        out_shape=jax.ShapeDtypeStruct((B, S, HG_W), BF16),
        scratch_shapes=[pltpu.VMEM((HG_HEADS, HG_DV, HG_DK), F32)],
        compiler_params=_cparams(("parallel", "arbitrary")),
        name="hgrn2",
    )(r3(hq), r3(hk), r3(hg), r3(hv), r3(gate), out_g, zc, masks)
    return out.reshape(B * S, HG_W)


def _pack_bf16_pair(lo, hi):
    lo_b = pltpu.bitcast(lo.astype(BF16).astype(F32), I32)
    hi_b = pltpu.bitcast(hi.astype(BF16).astype(F32), I32)
    return hi_b | lax.shift_right_logical(lo_b, 16)


def _unpack_bf16_pair(w):
    lo = pltpu.bitcast(lax.shift_left(w, 16), F32)
    hi = pltpu.bitcast(w & jnp.int32(-65536), F32)
    return lo, hi


def _out_proj_body(h_ref, attn_ref, hgo_ref, mg_ref, wo_ref, l1g_ref, l1b_ref, wr_ref, br_ref,
                   h1_out, h1b_out, pos_out, cnt_out):
    for sub_tile in range(TM_OUT // TM_DISP):
        rs = slice(sub_tile * TM_DISP, (sub_tile + 1) * TM_DISP)
        _out_proj_tile(h_ref[rs, :], attn_ref[rs, :], hgo_ref[rs, :], mg_ref, wo_ref,
                       l1g_ref, l1b_ref, wr_ref, br_ref,
                       h1_out.at[rs, :], h1b_out.at[rs, :], pos_out.at[:, rs], cnt_out.at[sub_tile])


def _out_proj_tile(h, attn, hgo, mg_ref, wo_ref, l1g_ref, l1b_ref, wr_ref, br_ref,
                   h1_out, h1b_out, pos_out, cnt_out):
    attn = _rms_norm(attn.astype(F32), mg_ref[...]).astype(BF16)
    mix_in = jnp.concatenate([attn, hgo], axis=1)
    mix = jnp.dot(mix_in, wo_ref[...], preferred_element_type=F32)
    h1 = _layer_norm(DEEPNORM_ALPHA * h + mix, l1g_ref[...], l1b_ref[...])
    h1_out[...] = h1
    h1_hi, h1_lo = _split_bf16(h1)

    w_hi, w_lo = _split_bf16(wr_ref[...])
    nt_dims = (((1,), (1,)), ((), ()))
    logits_t = (lax.dot_general(w_hi, h1_hi, nt_dims, preferred_element_type=F32)
                + lax.dot_general(w_hi, h1_lo, nt_dims, preferred_element_type=F32)
                + lax.dot_general(w_lo, h1_hi, nt_dims, preferred_element_type=F32)) + br_ref[...]
    tm = h1.shape[0]
    sub = lax.broadcasted_iota(I32, (N_EXPERTS, tm), 0)
    work = logits_t
    vals, idxs = [], []
    for _ in range(TOP_K):
        mx = jnp.max(work, axis=0, keepdims=True)
        ix = jnp.min(jnp.where(work == mx, sub, N_EXPERTS), axis=0, keepdims=True)
        vals.append(mx)
        idxs.append(ix)
        work = jnp.where(sub == ix, -jnp.inf, work)
    exps = [jnp.exp(vl - vals[0]) for vl in vals]
    den = exps[0] + exps[1] + exps[2] + exps[3]

    sel = jnp.zeros((N_EXPERTS, tm), F32)
    for ix in idxs:
        sel = sel + (sub == ix).astype(F32)
    selb = sel.astype(BF16)
    earlier = (lax.broadcasted_iota(I32, (tm, tm), 0)
               < lax.broadcasted_iota(I32, (tm, tm), 1)).astype(BF16)
    lower = (lax.broadcasted_iota(I32, (N_EXPERTS, N_EXPERTS), 1)
             < lax.broadcasted_iota(I32, (N_EXPERTS, N_EXPERTS), 0)).astype(BF16)
    cnt = jnp.sum(sel, axis=1, keepdims=True)
    run = jnp.floor((cnt + (SUBLANES - 1)) * (1.0 / SUBLANES)) * SUBLANES
    run_b = jnp.broadcast_to(run, (N_EXPERTS, LANES)).astype(BF16)
    seg_off = jnp.dot(lower, run_b, preferred_element_type=F32)[:, 0:1]
    posmat = jnp.dot(selb, earlier, preferred_element_type=F32) + seg_off
    pos = [jnp.sum(jnp.where(sub == ix, posmat, 0.0), axis=0, keepdims=True) for ix in idxs]
    pos_out[...] = jnp.concatenate(pos, axis=0).astype(I32)
    gsel = jnp.zeros((N_EXPERTS, tm), F32)
    for ix, ex in zip(idxs, exps):
        gsel = gsel + jnp.where(sub == ix, ex / den, 0.0)
    g_hi, g_lo = _split_bf16(gsel)
    e_sub = lax.broadcasted_iota(I32, (N_EXPERTS, LANES), 0)
    e_lane = lax.broadcasted_iota(I32, (N_EXPERTS, LANES), 1)
    tn_dims = (((0,), (0,)), ((), ()))
    gx = (lax.dot_general(g_hi, (e_lane == e_sub).astype(BF16), tn_dims, preferred_element_type=F32)
          + lax.dot_general(g_lo, (e_lane == e_sub + N_EXPERTS).astype(BF16), tn_dims,
                            preferred_element_type=F32))
    h1b_out[...] = jnp.concatenate([h1_hi, gx.astype(BF16)], axis=1)
    cnt_out[...] = cnt


def _out_proj(h, attn, hgo, mla_g, w_o, l1g, l1b, w_r, b_r):
    T = h.shape[0]
    row = lambda w: pl.BlockSpec((TM_OUT, w), lambda i: (i, 0))
    full = lambda a: pl.BlockSpec(a.shape, lambda i: (0, 0))
    col = pl.BlockSpec((TOP_K, TM_OUT), lambda i: (0, i))
    sub_tiles = TM_OUT // TM_DISP
    return pl.pallas_call(
        _out_proj_body,
        grid=(T // TM_OUT,),
        in_specs=[row(D_MODEL), row(MLA_HEADS * MLA_V), row(HG_W), full(mla_g),
                  full(w_o), full(l1g), full(l1b), full(w_r), full(b_r)],
        out_specs=[row(D_MODEL), row(D_MODEL + LANES), col,
                   pl.BlockSpec((sub_tiles, N_EXPERTS, 1), lambda i: (i, 0, 0))],
        out_shape=[jax.ShapeDtypeStruct((T, D_MODEL), F32), jax.ShapeDtypeStruct((T, D_MODEL + LANES), BF16),
                   jax.ShapeDtypeStruct((TOP_K, T), I32),
                   jax.ShapeDtypeStruct((T // TM_DISP, N_EXPERTS, 1), F32)],
        compiler_params=_cparams(("parallel",)),
        name="out_proj_router",
    )(h, attn, hgo, mla_g, w_o, l1g, l1b, w_r, b_r)


TILE_SLOTS = TOP_K * TM_DISP + N_EXPERTS * SUBLANES
ROW_WORDS = D_MODEL // 2 + LANES
TILES_PER_STEP = 2
TM_STEP = TILES_PER_STEP * TM_DISP
RUN_UNROLL = 4

def _ceil_to(n, m):
    return (n + m - 1) // m * m


def _rows(start, size):
    return pl.ds(pl.multiple_of(start, SUBLANES), size)


def _for_each_piece(n, fn, max_rows=TM_DISP):
    for kbit in reversed(range(SUBLANES.bit_length() - 1, max_rows.bit_length())):
        size = 1 << kbit

        @pl.when((n & size) != 0)
        def _():
            fn(pl.multiple_of((n >> (kbit + 1)) << (kbit + 1), SUBLANES), size)


def _pack_exact_pair(lo, hi):
    return pltpu.bitcast(hi, I32) | lax.shift_right_logical(pltpu.bitcast(lo, I32), 16)


def _zero_tail_blocks(zeros_ref, out_hbm, n_used, zsem):
    def blk_copy(g):
        return pltpu.make_async_copy(zeros_ref.at[pl.ds(0, ROW_UNIT)],
                                     out_hbm.at[_rows(g * ROW_UNIT, ROW_UNIT)], zsem)

    n_total = out_hbm.shape[0] // ROW_UNIT
    lax.fori_loop(n_used, n_total, lambda g, c: (blk_copy(g).start(), c)[1], 0)
    lax.fori_loop(n_used, n_total, lambda g, c: (blk_copy(g).wait(), c)[1], 0)


def _dispatch_body(segn_ref, segsrc_ref, segdst_ref, tot_ref, padn_ref, padd_ref, nused_ref, pos_ref, xb_ref,
                   xs_out, stage, zeros_ref, sem, zsem):
    i = pl.program_id(0)

    @pl.when(i == 0)
    def _():
        zeros_ref[...] = jnp.zeros_like(zeros_ref)

        def pad_copy(e, off, size):
            return pltpu.make_async_copy(zeros_ref.at[pl.ds(0, size)],
                                         xs_out.at[_rows(padd_ref[e] + off, size)], zsem)

        def fill(e, c):
            _for_each_piece(padn_ref[e], lambda off, size: pad_copy(e, off, size).start())
            return c

        def drain(e, c):
            _for_each_piece(padn_ref[e], lambda off, size: pad_copy(e, off, size).wait())
            return c

        lax.fori_loop(0, N_EXPERTS, fill, 0)
        lax.fori_loop(0, N_EXPERTS, drain, 0)
        _zero_tail_blocks(zeros_ref, xs_out, nused_ref[0], zsem)

    def settle(tile, b):
        _for_each_piece(tot_ref[tile], lambda off, size: pltpu.make_async_copy(
            stage.at[b, pl.ds(0, size)], xs_out.at[pl.ds(0, size)], sem.at[b]).wait(), TILE_SLOTS)

    for b in range(TILES_PER_STEP):
        tile = i * TILES_PER_STEP + b
        ts = slice(b * TM_DISP, (b + 1) * TM_DISP)
        pos = pos_ref[:, ts]
        slot = lax.broadcasted_iota(I32, (TILE_SLOTS, TM_DISP), 0)
        sel = jnp.zeros((TILE_SLOTS, TM_DISP), F32)
        for j in range(TOP_K):
            sel = jnp.where(slot == pos[j:j + 1, :], 1.0, sel)
        rows = jnp.dot(sel.astype(BF16), xb_ref[ts, :], preferred_element_type=F32)
        packed = jnp.concatenate(
            [_pack_exact_pair(rows[:, :D_MODEL // 2], rows[:, D_MODEL // 2:D_MODEL]),
             pltpu.bitcast(rows[:, D_MODEL:], I32)], axis=1)

        @pl.when(i >= 1)
        def _():
            settle(tile - TILES_PER_STEP, b)

        stage[b] = packed

        def send(eb, c):
            for u in range(RUN_UNROLL):
                k = tile * N_EXPERTS + eb * RUN_UNROLL + u
                src, dst = segsrc_ref[k], segdst_ref[k]
                _for_each_piece(segn_ref[k], lambda off, size: pltpu.make_async_copy(
                    stage.at[b, _rows(src + off, size)], xs_out.at[_rows(dst + off, size)],
                    sem.at[b]).start(priority=size.bit_length() % 2))
            return c

        lax.fori_loop(0, N_EXPERTS // RUN_UNROLL, send, 0)

    @pl.when(i == pl.num_programs(0) - 1)
    def _():
        for b in range(TILES_PER_STEP):
            settle(i * TILES_PER_STEP + b, b)


def _dispatch(seg_n, seg_src, seg_dst, tile_tot, pad_n, pad_dst, n_used, pos_t, h1b, n_rows):
    T = h1b.shape[0]
    grid_spec = pltpu.PrefetchScalarGridSpec(
        num_scalar_prefetch=7,
        grid=(T // TM_STEP,),
        in_specs=[pl.BlockSpec((TOP_K, TM_STEP), lambda i, *_: (0, i)),
                  pl.BlockSpec((TM_STEP, D_MODEL + LANES), lambda i, *_: (i, 0))],
        out_specs=pl.BlockSpec(memory_space=pl.ANY),
        scratch_shapes=[pltpu.VMEM((TILES_PER_STEP, TILE_SLOTS, ROW_WORDS), I32),
                        pltpu.VMEM((ROW_BLOCK, ROW_WORDS), I32),
                        pltpu.SemaphoreType.DMA((TILES_PER_STEP,)), pltpu.SemaphoreType.DMA(())],
    )
    return pl.pallas_call(
        _dispatch_body,
        grid_spec=grid_spec,
        out_shape=jax.ShapeDtypeStruct((n_rows, ROW_WORDS), I32),
        compiler_params=_cparams(("arbitrary",)),
        name="moe_dispatch",
    )(seg_n, seg_src, seg_dst, tile_tot, pad_n, pad_dst, n_used, pos_t, h1b)


def _experts_body(istart_ref, icount_ref, iunit_ref, ibig_ref, nitems_ref, uused_ref,
                  xs_hbm, wu_ref, bu_ref, wd_ref, bd_ref, ys_hbm,
                  wu_bf, wd_bf, xbuf, ybuf, sem_in, sem_out):
    e = pl.program_id(0)
    n_items = nitems_ref[0]
    nb = icount_ref[e]

    def by_size(k, fn):
        code = ibig_ref[k]
        for c in range(3):
            pl.when(code == c)(lambda c=c: fn(ROW_UNIT << c))

    def fetch(k, slot, size):
        rows = _rows(iunit_ref[k] * ROW_UNIT, size)
        return pltpu.make_async_copy(xs_hbm.at[rows], xbuf.at[slot, pl.ds(0, size)], sem_in.at[slot])

    def put(k, slot, size):
        rows = _rows(iunit_ref[k] * ROW_UNIT, size)
        return pltpu.make_async_copy(ybuf.at[slot, pl.ds(0, size)], ys_hbm.at[rows], sem_out.at[slot])

    @pl.when(jnp.logical_and(e == 0, n_items > 0))
    def _():
        by_size(0, lambda size: fetch(0, 0, size).start())

    @pl.when(nb > 0)
    def _():
        wu_bf[...] = wu_ref[0].astype(BF16)
        wd_bf[...] = wd_ref[0].astype(BF16)

    def compute(k, slot, size):
        row = xbuf[slot, :size]
        lo, hi = _unpack_bf16_pair(row[:, :D_MODEL // 2])
        x = jnp.concatenate([lo, hi], axis=1).astype(BF16)
        lane = lax.broadcasted_iota(I32, (size, LANES), 1)
        mine = jnp.logical_or(lane == e, lane == e + N_EXPERTS)
        gate = jnp.sum(jnp.where(mine, pltpu.bitcast(row[:, D_MODEL // 2:], F32), 0.0),
                       axis=1, keepdims=True)
        hb = jnp.dot(x, wu_bf[...], preferred_element_type=F32) + bu_ref[0]
        glu = jnp.minimum(hb[:, :D_EXPERT], SWIGLU_LIMIT)
        lin = jnp.clip(hb[:, D_EXPERT:], -SWIGLU_LIMIT, SWIGLU_LIMIT)
        act = glu * (1.0 / (1.0 + jnp.exp(-SWIGLU_ALPHA * glu))) * (lin + 1.0)
        y = (jnp.dot(act.astype(BF16), wd_bf[...], preferred_element_type=F32) + bd_ref[0]) * gate
        ybuf[slot, :size] = _pack_bf16_pair(y[:, :D_MODEL // 2], y[:, D_MODEL // 2:])
        put(k, slot, size).start()

    def block(b, c):
        k = istart_ref[e] + b
        slot = lax.rem(k, 2)
        by_size(k, lambda size: fetch(k, slot, size).wait())

        @pl.when(k + 1 < n_items)
        def _():
            by_size(k + 1, lambda size: fetch(k + 1, 1 - slot, size).start())

        @pl.when(k >= 2)
        def _():
            by_size(k - 2, lambda size: put(k - 2, slot, size).wait())

        by_size(k, lambda size: compute(k, slot, size))
        return c

    lax.fori_loop(0, nb, block, 0)

    @pl.when(e == N_EXPERTS - 1)
    def _():
        for back in (2, 1):
            @pl.when(n_items >= back)
            def _():
                k = n_items - back
                by_size(k, lambda size: put(k, lax.rem(k, 2), size).wait())
        ybuf[0] = jnp.zeros_like(ybuf[0])
        _zero_tail_blocks(ybuf.at[0], ys_hbm, uused_ref[0], sem_out.at[0])


def _experts(item_start, item_count, item_unit, item_big, n_items, units_used, xs,
             w_up, b_up, w_down, b_down):
    n_rows = xs.shape[0]
    grid_spec = pltpu.PrefetchScalarGridSpec(
        num_scalar_prefetch=6,
        grid=(N_EXPERTS,),
        in_specs=[pl.BlockSpec(memory_space=pl.ANY),
                  pl.BlockSpec((1, D_MODEL, 2 * D_EXPERT), lambda e, *_: (e, 0, 0)),
                  pl.BlockSpec((1, 1, 2 * D_EXPERT), lambda e, *_: (e, 0, 0)),
                  pl.BlockSpec((1, D_EXPERT, D_MODEL), lambda e, *_: (e, 0, 0)),
                  pl.BlockSpec((1, 1, D_MODEL), lambda e, *_: (e, 0, 0))],
        out_specs=pl.BlockSpec(memory_space=pl.ANY),
        scratch_shapes=[pltpu.VMEM((D_MODEL, 2 * D_EXPERT), BF16), pltpu.VMEM((D_EXPERT, D_MODEL), BF16),
                        pltpu.VMEM((2, ROW_BLOCK, ROW_WORDS), I32),
                        pltpu.VMEM((2, ROW_BLOCK, D_MODEL // 2), I32),
                        pltpu.SemaphoreType.DMA((2,)), pltpu.SemaphoreType.DMA((2,))],
    )
    return pl.pallas_call(
        _experts_body,
        grid_spec=grid_spec,
        out_shape=jax.ShapeDtypeStruct((n_rows, D_MODEL // 2), I32),
        compiler_params=_cparams(("arbitrary",)),
        name="moe_experts",
    )(item_start, item_count, item_unit, item_big, n_items, units_used, xs, w_up,
      b_up.reshape(N_EXPERTS, 1, 2 * D_EXPERT), w_down, b_down.reshape(N_EXPERTS, 1, D_MODEL))


def _combine_body(segn_ref, segsrc_ref, segdst_ref, tot_ref, pos_ref, h1_ref, l2g_ref, l2b_ref, ys_hbm,
                  o_ref, stage, sem):
    i = pl.program_id(0)
    n_tiles = pl.num_programs(0) * TILES_PER_STEP

    def fetch_tile(tile, b):
        def fetch(eb, c):
            for u in range(RUN_UNROLL):
                k = tile * N_EXPERTS + eb * RUN_UNROLL + u
                src, dst = segsrc_ref[k], segdst_ref[k]
                _for_each_piece(segn_ref[k], lambda off, size: pltpu.make_async_copy(
                    ys_hbm.at[_rows(dst + off, size)], stage.at[b, _rows(src + off, size)],
                    sem.at[b]).start(priority=size.bit_length() % 2))
            return c

        lax.fori_loop(0, N_EXPERTS // RUN_UNROLL, fetch, 0)

    @pl.when(i == 0)
    def _():
        stage[...] = jnp.zeros_like(stage)
        fetch_tile(0, 0)

    for b in range(TILES_PER_STEP):
        tile = i * TILES_PER_STEP + b
        nxt = (b + 1) % TILES_PER_STEP

        @pl.when(tile + 1 < n_tiles)
        def _():
            fetch_tile(tile + 1, nxt)

        ts = slice(b * TM_DISP, (b + 1) * TM_DISP)
        pos = pos_ref[ts, :]
        slot = lax.broadcasted_iota(I32, (TM_DISP, TILE_SLOTS), 1)
        w = jnp.zeros((TM_DISP, TILE_SLOTS), F32)
        for j in range(TOP_K):
            w = jnp.where(slot == pos[:, j:j + 1], 1.0, w)
        w = w.astype(BF16)
        _for_each_piece(tot_ref[tile], lambda off, size: pltpu.make_async_copy(
            ys_hbm.at[pl.ds(0, size)], stage.at[b, pl.ds(0, size)], sem.at[b]).wait(), TILE_SLOTS)
        lo, hi = _unpack_bf16_pair(stage[b])
        y = jnp.concatenate([lo, hi], axis=1).astype(BF16)
        ffn = jnp.dot(w, y, preferred_element_type=F32)
        o_ref[ts, :] = _layer_norm(DEEPNORM_ALPHA * h1_ref[ts, :] + ffn, l2g_ref[...], l2b_ref[...])


def _combine(seg_n, seg_src, seg_dst, tile_tot, pos, h1, l2g, l2b, ys):
    T = h1.shape[0]
    grid_spec = pltpu.PrefetchScalarGridSpec(
        num_scalar_prefetch=4,
        grid=(T // TM_STEP,),
        in_specs=[pl.BlockSpec((TM_STEP, TOP_K), lambda i, *_: (i, 0)),
                  pl.BlockSpec((TM_STEP, D_MODEL), lambda i, *_: (i, 0)),
                  pl.BlockSpec(l2g.shape, lambda i, *_: (0, 0)),
                  pl.BlockSpec(l2b.shape, lambda i, *_: (0, 0)),
                  pl.BlockSpec(memory_space=pl.ANY)],
        out_specs=pl.BlockSpec((TM_STEP, D_MODEL), lambda i, *_: (i, 0)),
        scratch_shapes=[pltpu.VMEM((TILES_PER_STEP, TILE_SLOTS, D_MODEL // 2), I32),
                        pltpu.SemaphoreType.DMA((TILES_PER_STEP,))],
    )
    return pl.pallas_call(
        _combine_body,
        grid_spec=grid_spec,
        out_shape=jax.ShapeDtypeStruct((T, D_MODEL), F32),
        compiler_params=_cparams(("arbitrary",)),
        name="moe_combine",
    )(seg_n, seg_src, seg_dst, tile_tot, pos, h1, l2g, l2b, ys)


def _prep_weights(w_in, w_q_b, w_kv_b):
    w_kr = jnp.pad(w_in[:, KR_OFF:KR_OFF + MLA_ROPE], ((0, 0), (MLA_NOPE, HEAD_PAD - MLA_NOPE - MLA_ROPE)))
    w_in_p = (jnp.concatenate([w_in[:, :KR_OFF], w_kr], axis=1).astype(BF16),
              w_in[:, KR_OFF + MLA_ROPE:].astype(BF16))
    qd = MLA_NOPE + MLA_ROPE
    wq = w_q_b.reshape(Q_LORA, MLA_HEADS, qd)
    wq = jnp.pad(wq, ((0, 0), (0, 0), (0, HEAD_PAD - qd))).reshape(Q_LORA, MLA_HEADS * HEAD_PAD)
    wkv = w_kv_b.reshape(KV_LORA, MLA_HEADS, MLA_NOPE + MLA_V)
    wk = jnp.pad(wkv[:, :, :MLA_NOPE], ((0, 0), (0, 0), (0, HEAD_PAD - MLA_NOPE)))
    wk = wk.reshape(KV_LORA, MLA_HEADS * HEAD_PAD)
    wv = jnp.pad(wkv[:, :, MLA_NOPE:], ((0, 0), (0, 0), (0, HEAD_PAD - MLA_V)))
    wv = wv.reshape(KV_LORA, MLA_HEADS * HEAD_PAD)
    return w_in_p, wq.astype(BF16), jnp.concatenate([wk, wv], axis=1).astype(BF16)


def kernel(x, positions, ln_in_g, ln_in_b, w_in, q_a_norm_g, w_q_b, kv_a_norm_g, w_kv_b, hgrn_lb_logits,
           mla_out_g, hgrn_out_g, w_o, ln1_g, ln1_b, w_router, b_router, w_up, b_up, w_down, b_down,
           ln2_g, ln2_b):
    B, S, D = x.shape
    assert D == D_MODEL and w_in.shape[0] == DEPTH and hgrn_lb_logits.shape[0] == DEPTH + 1
    T = B * S
    xt = x.reshape(T, D)
    r2 = lambda a: a.reshape(1, -1)

    cos, sin = _rope_cos_sin(positions)
    w_in_p, w_qb_p, w_kv_p = _prep_weights(w_in[0], w_q_b[0], w_kv_b[0])
    q, k, v, hq, hk, hg, hv, gate, h0 = _in_proj(
        xt, r2(ln_in_g), r2(ln_in_b), w_in_p, q_a_norm_g, w_qb_p, kv_a_norm_g, w_kv_p,
        hgrn_lb_logits, cos, sin)
    attn = _attention(q, k, v, B, S)
    hgo = _hgrn(hq, hk, hg, hv, gate, hgrn_out_g, B, S)
    h1, h1b, pos_t, tile_cnt = _out_proj(
        h0, attn, hgo, mla_out_g, w_o[0].astype(BF16), ln1_g, ln1_b,
        w_router[0].T, b_router.reshape(N_EXPERTS, 1))

    n_tiles = T // TM_DISP
    n_rows = T * TOP_K + n_tiles * N_EXPERTS * (SUBLANES - 1) + N_EXPERTS * ROW_UNIT
    n_rows = _ceil_to(n_rows, ROW_BLOCK)
    tc = _ceil_to(tile_cnt[:, :, 0].astype(I32), SUBLANES)
    cnt = jnp.sum(tc, axis=0)
    padded = _ceil_to(cnt, ROW_UNIT)
    pad_start = jnp.cumsum(padded) - padded
    seg_n = tc.reshape(-1)
    seg_src = (jnp.cumsum(tc, axis=1) - tc).reshape(-1)
    seg_dst = (pad_start[None, :] + jnp.cumsum(tc, axis=0) - tc).reshape(-1)
    units_used = jnp.sum(padded, keepdims=True) // ROW_UNIT

    units = padded // ROW_UNIT
    n_full, left = units // 4, units % 4
    item_count = n_full + left // 2 + left % 2
    item_end = jnp.cumsum(item_count)
    item_start = item_end - item_count
    k = jnp.arange(n_rows // ROW_BLOCK + 2 * N_EXPERTS, dtype=I32)
    of_expert = (jnp.minimum(jnp.sum((item_end[None, :] <= k[:, None]).astype(I32), axis=1), N_EXPERTS - 1)
                 [:, None] == jnp.arange(N_EXPERTS, dtype=I32)[None, :])
    pick = lambda v: jnp.sum(jnp.where(of_expert, v[None, :], 0), axis=1)
    local, full_k, left_k = k - pick(item_start), pick(n_full), pick(left)
    half_first = jnp.logical_and(local == full_k, left_k >= 2)
    item_big = jnp.where(local < full_k, 2, half_first.astype(I32))
    item_unit = pick(pad_start // ROW_UNIT) + jnp.where(
        local <= full_k, 4 * local, 4 * full_k + 2)

    tile_tot = jnp.sum(tc, axis=1)
    xs = _dispatch(seg_n, seg_src, seg_dst, tile_tot, padded - cnt, pad_start + cnt, units_used, pos_t, h1b,
                   n_rows)
    ys = _experts(item_start, item_count, item_unit, item_big, item_end[-1:], units_used, xs,
                  w_up[0], b_up[0], w_down[0], b_down[0])
    out = _combine(seg_n, seg_src, seg_dst, tile_tot, pos_t.T, h1, ln2_g, ln2_b, ys)
    return out.reshape(B, S, D)
```
